```python
import math
import jax, jax.numpy as jnp
from jax import lax
import numpy as np

D_MODEL = 1024
BATCH = 16
SEQ = 256
DEPTH = 2
DEC_BATCH = 8
DEC_SEQ = 2048
PAST_LEN = 512

GRID_W = 64
MIX_WIDTH = D_MODEL
ATT_HEADS = 4
ATT_QK_DIM = 32
ATT_V_DIM = 2 * ATT_QK_DIM
ATT_WIDTH = ATT_HEADS * ATT_V_DIM
QK_WIDTH = ATT_HEADS * 2 * ATT_QK_DIM
ROPE_AXIS_DIM = ATT_QK_DIM // 2
ROPE_BASE = 10000.0
Q_BLOCK = 128
CONV_CH = MIX_WIDTH // 4
CONV_K = 31
SSM_WIDTH = MIX_WIDTH - ATT_WIDTH - CONV_CH
SSM_HEAD_DIM = 64
SSM_HEADS = SSM_WIDTH // SSM_HEAD_DIM
SSM_GROUPS = 2
SSM_STATE = 64
SSM_CONV_K = 5
SSM_CHUNK = 128
SSM_XBC = SSM_WIDTH + 2 * SSM_GROUPS * SSM_STATE
_S_Q = QK_WIDTH
_S_K = _S_Q + QK_WIDTH
_S_V = _S_K + ATT_WIDTH
_S_GLU = _S_V + 2 * CONV_CH
_S_Z = _S_GLU + SSM_WIDTH
_S_XBC = _S_Z + SSM_XBC
IN_SPLITS = (_S_Q, _S_K, _S_V, _S_GLU, _S_Z, _S_XBC)
N_IN = _S_XBC + 2 * SSM_HEADS
N_EXPERTS = 32
TOP_K = 4
D_FF = D_MODEL
SWIGLU_LIMIT = 7.0
SWIGLU_ALPHA = 1.702
MOE_BLOCK = 256
EPS = 1e-6

kernel_name = 'hybrid_diffusion_prefix_step'


def rms_norm(x, g):
    xf = x.astype(jnp.float32)
    y = xf * lax.rsqrt(jnp.mean(xf * xf, axis=-1, keepdims=True) + EPS)
    return (y * g.astype(jnp.float32)).astype(x.dtype)


def layer_norm(x, g, b):
    xf = x.astype(jnp.float32)
    xc = xf - jnp.mean(xf, axis=-1, keepdims=True)
    var = jnp.mean(xc * xc, axis=-1, keepdims=True)
    return (xc * lax.rsqrt(var + EPS) * g.astype(jnp.float32) + b.astype(jnp.float32)).astype(x.dtype)


def depthwise_conv(x, w, b):
    width = w.shape[0]
    y = lax.conv_general_dilated(x, w[:, None, :].astype(x.dtype), window_strides=(1,),
                                 padding=[(width // 2, width // 2)],
                                 dimension_numbers=('NWC', 'WIO', 'NWC'),
                                 feature_group_count=x.shape[-1])
    return y + b.astype(x.dtype)


def axial_rope_tables(n_tokens):
    rows = n_tokens // GRID_W
    row = jnp.broadcast_to(jnp.arange(rows, dtype=jnp.float32)[:, None], (rows, GRID_W)).reshape(-1)
    col = jnp.broadcast_to(jnp.arange(GRID_W, dtype=jnp.float32)[None, :], (rows, GRID_W)).reshape(-1)
    inv_freq = 1.0 / (ROPE_BASE ** (jnp.arange(0, ROPE_AXIS_DIM, 2, dtype=jnp.float32) / ROPE_AXIS_DIM))
    ang_r = row[:, None] * inv_freq[None, :]
    ang_c = col[:, None] * inv_freq[None, :]
    ang = jnp.concatenate([ang_r, ang_r, ang_c, ang_c], axis=-1)
    return jnp.cos(ang), jnp.sin(ang)


def _rotate_half(u):
    u1, u2 = jnp.split(u, 2, axis=-1)
    return jnp.concatenate([-u2, u1], axis=-1)


def apply_axial_rope(x, cos, sin):
    xf = x.astype(jnp.float32)
    rot = jnp.concatenate([_rotate_half(xf[..., :ROPE_AXIS_DIM]), _rotate_half(xf[..., ROPE_AXIS_DIM:])], axis=-1)
    return (xf * cos + rot * sin).astype(x.dtype)


def diff_attention(q, k, v, lam):
    b, h, _, tq, d = q.shape
    nb = tq // Q_BLOCK
    qb = q.reshape(b, h, 2, nb, Q_BLOCK, d).transpose(3, 0, 1, 2, 4, 5)
    scale = d ** -0.5

    def block(qi):
        s = jnp.einsum('bhcqd,bhckd->bhcqk', qi, k, preferred_element_type=jnp.float32) * scale
        p = jax.nn.softmax(s, axis=-1)
        a = p[:, :, 0] - lam * p[:, :, 1]
        return jnp.einsum('bhqk,bhkv->bhqv', a.astype(v.dtype), v)

    o = lax.map(block, qb)
    return o.transpose(1, 2, 0, 3, 4).reshape(b, h, tq, v.shape[-1])


def ssd_chunked(x, dt, a, bh, ch, init_state):
    b, t, h, p = x.shape
    n = bh.shape[-1]
    nc, l = t // SSM_CHUNK, SSM_CHUNK
    f32 = jnp.float32
    xdt = (x.astype(f32) * dt[..., None]).reshape(b, nc, l, h, p)
    bc = bh.astype(f32).reshape(b, nc, l, h, n)
    cc = ch.astype(f32).reshape(b, nc, l, h, n)
    da = (dt * a).reshape(b, nc, l, h).transpose(0, 3, 1, 2)
    a_cum = jnp.cumsum(da, axis=-1)
    seg = a_cum[..., :, None] - a_cum[..., None, :]
    lower = jnp.tril(jnp.ones((l, l), dtype=bool))
    decay = jnp.exp(jnp.where(lower, seg, -jnp.inf))
    y_diag = jnp.einsum('bclhn,bcshn,bhcls,bcshp->bclhp', cc, bc, decay, xdt)
    decay_to_end = jnp.exp(a_cum[..., -1:] - a_cum)
    chunk_states = jnp.einsum('bclhn,bhcl,bclhp->bchpn', bc, decay_to_end, xdt)
    chunk_decay = jnp.exp(a_cum[..., -1])

    def step(s, inp):
        st, dec = inp
        return dec[..., None, None] * s + st, s

    final, prev = lax.scan(step, init_state.astype(f32),
                           (chunk_states.transpose(1, 0, 2, 3, 4), chunk_decay.transpose(2, 0, 1)))
    prev = prev.transpose(1, 0, 2, 3, 4)
    y_off = jnp.einsum('bclhn,bchpn,bhcl->bclhp', cc, prev, jnp.exp(a_cum))
    y = (y_diag + y_off).reshape(b, t, h, p)
    return y.astype(x.dtype), final


def ssd_mixer(z, xbc, dt_raw, lp, init_states):
    b, t, _ = z.shape
    xbc = jax.nn.silu(depthwise_conv(xbc, lp['ssm_conv_w'], lp['ssm_conv_b']))
    gn = SSM_GROUPS * SSM_STATE
    xs = xbc[..., :SSM_WIDTH].reshape(b, t, SSM_HEADS, SSM_HEAD_DIM)
    rep = SSM_HEADS // SSM_GROUPS
    bh = jnp.repeat(xbc[..., SSM_WIDTH:SSM_WIDTH + gn].reshape(b, t, SSM_GROUPS, SSM_STATE), rep, axis=2)
    ch = jnp.repeat(xbc[..., SSM_WIDTH + gn:].reshape(b, t, SSM_GROUPS, SSM_STATE), rep, axis=2)
    dt = jax.nn.softplus(dt_raw.reshape(b, t, 2, SSM_HEADS).astype(jnp.float32)
                         + lp['ssm_dt_bias'].astype(jnp.float32))
    a = -jnp.exp(lp['ssm_a_log'].astype(jnp.float32))
    y_f, s_f = ssd_chunked(xs, dt[:, :, 0], a[0], bh, ch, init_states[:, 0])
    y_b, s_b = ssd_chunked(jnp.flip(xs, 1), jnp.flip(dt[:, :, 1], 1), a[1],
                           jnp.flip(bh, 1), jnp.flip(ch, 1), init_states[:, 1])
    y = y_f + jnp.flip(y_b, 1) + lp['ssm_d'].astype(xs.dtype)[None, None, :, None] * xs
    y = rms_norm(y.reshape(b, t, SSM_WIDTH) * jax.nn.silu(z), lp['ssm_norm_g'])
    return y, jnp.stack([s_f, s_b], axis=1)


def conformer_conv(glu_in, lp):
    a, g = jnp.split(glu_in, 2, axis=-1)
    u = depthwise_conv(a * jax.nn.sigmoid(g), lp['conv_dw_w'], lp['conv_dw_b'])
    return jax.nn.silu(layer_norm(u, lp['conv_ln_g'], lp['conv_ln_b']))


def moe_ffn(h, lp):
    t, d = h.shape
    logits = jnp.dot(h, lp['router_w'], preferred_element_type=jnp.float32) + lp['router_b'].astype(jnp.float32)
    top_val, top_idx = lax.top_k(logits, TOP_K)
    gate_w = jax.nn.softmax(top_val, axis=-1)
    n_assign = t * TOP_K
    flat_e = top_idx.reshape(-1)
    order = jnp.argsort(flat_e)
    sorted_e = flat_e[order]
    token_idx = order // TOP_K
    counts = jnp.bincount(flat_e, length=N_EXPERTS)
    padded = (counts + MOE_BLOCK - 1) // MOE_BLOCK * MOE_BLOCK
    pad_end = jnp.cumsum(padded)
    pad_start = pad_end - padded
    grp_start = jnp.cumsum(counts) - counts
    dest = pad_start[sorted_e] + (jnp.arange(n_assign) - grp_start[sorted_e])
    n_blocks = -(-n_assign // MOE_BLOCK) + N_EXPERTS
    buf = jnp.zeros((n_blocks * MOE_BLOCK, d), h.dtype).at[dest].set(h[token_idx])
    block_e = jnp.minimum(jnp.searchsorted(pad_end, jnp.arange(n_blocks) * MOE_BLOCK, side='right'),
                          N_EXPERTS - 1)

    def expert_block(args):
        xb, e = args
        g = jnp.dot(xb, lp['w_gate'][e]) + lp['b_gate'][e]
        u = jnp.dot(xb, lp['w_up'][e]) + lp['b_up'][e]
        g = jnp.minimum(g, SWIGLU_LIMIT)
        u = jnp.clip(u, -SWIGLU_LIMIT, SWIGLU_LIMIT)
        act = g * jax.nn.sigmoid(SWIGLU_ALPHA * g) * (u + 1.0)
        return jnp.dot(act, lp['w_down'][e]) + lp['b_down'][e]

    out = lax.map(expert_block, (buf.reshape(n_blocks, MOE_BLOCK, d), block_e))
    out = out.reshape(-1, d)[dest]
    w_sorted = gate_w.reshape(-1)[order]
    y = jnp.zeros((t, d), jnp.float32).at[token_idx].add(out.astype(jnp.float32) * w_sorted[:, None])
    return y.astype(h.dtype)


def trunk_layer(x, cond, l, lp, ctx):
    b, t, _ = x.shape
    mod = jnp.dot(jax.nn.silu(cond), lp['w_ada']) + lp['b_ada']
    shift1, scale1, gate1, shift2, scale2, gate2 = jnp.split(mod, 6, axis=-1)
    h = rms_norm(x, lp['norm1_g']) * (1.0 + scale1) + shift1
    proj = jnp.dot(h, lp['w_in'])
    q, k, v, glu, z, xbc, dt_raw = jnp.split(proj, IN_SPLITS, axis=-1)
    q = q.reshape(b, t, ATT_HEADS, 2, ATT_QK_DIM).transpose(0, 2, 3, 1, 4)
    k = k.reshape(b, t, ATT_HEADS, 2, ATT_QK_DIM).transpose(0, 2, 3, 1, 4)
    v = v.reshape(b, t, ATT_HEADS, ATT_V_DIM).transpose(0, 2, 1, 3)
    if ctx is None:
        k_keys, v_keys = k, v
        ssm_init = jnp.zeros((b, 2, SSM_HEADS, SSM_HEAD_DIM, SSM_STATE), jnp.float32)
    else:
        k_ctx, v_ctx, ssm_init = ctx
        cos, sin = axial_rope_tables(t)
        q = apply_axial_rope(q, cos, sin)
        k = apply_axial_rope(k, cos, sin)
        k_keys = jnp.concatenate([k, k_ctx.astype(k.dtype)], axis=3)
        v_keys = jnp.concatenate([v, v_ctx.astype(v.dtype)], axis=2)
    lam_init = 0.8 - 0.6 * math.exp(-0.3 * l)
    lq1, lk1, lq2, lk2 = jnp.split(lp['attn_lambda'].astype(jnp.float32), 4, axis=0)
    lam = jnp.exp(jnp.sum(lq1 * lk1)) - jnp.exp(jnp.sum(lq2 * lk2)) + lam_init
    attn = diff_attention(q, k_keys, v_keys, lam)
    attn = rms_norm(attn, lp['attn_subln_g']) * (1.0 - lam_init)
    attn = attn.transpose(0, 2, 1, 3).reshape(b, t, ATT_WIDTH)
    conv = conformer_conv(glu, lp)
    ssm, ssm_final = ssd_mixer(z, xbc, dt_raw, lp, ssm_init)
    mix = jnp.dot(jnp.concatenate([attn, conv, ssm], axis=-1), lp['w_out'])
    x = x + gate1 * mix
    h2 = rms_norm(x, lp['norm2_g']) * (1.0 + scale2) + shift2
    ffn = moe_ffn(h2.reshape(b * t, D_MODEL), lp).reshape(b, t, D_MODEL)
    x = x + gate2 * ffn
    return x, (k, v, ssm_final)


def setup_inputs(seed: int = 0) -> dict:
    key = jax.random.key(seed)
    ks = iter(jax.random.split(key, 48))
    f32 = jnp.float32

    def nrm(shape, scale):
        return jax.random.normal(next(ks), shape, f32) * scale

    dt0 = jnp.exp(jax.random.uniform(next(ks), (DEPTH, 2, SSM_HEADS), f32, math.log(1e-3), math.log(1e-1)))
    return {
        'x_prompt': nrm((BATCH, SEQ, D_MODEL), 1.0),
        'x_sample': nrm((DEC_BATCH, DEC_SEQ, D_MODEL), 1.0),
        'c': nrm((DEC_BATCH, D_MODEL), 1.0),
        'cache_k': nrm((DEC_BATCH, DEPTH, ATT_HEADS, 2, PAST_LEN, ATT_QK_DIM), 1.0),
        'cache_v': nrm((DEC_BATCH, DEPTH, ATT_HEADS, PAST_LEN, ATT_V_DIM), 1.0),
        'state_ssm': nrm((DEC_BATCH, DEPTH, 2, SSM_HEADS, SSM_HEAD_DIM, SSM_STATE), 0.5),
        'c_ctx': nrm((D_MODEL,), 1.0),
        'norm1_g': 1.0 + nrm((DEPTH, D_MODEL), 0.02),
        'norm2_g': 1.0 + nrm((DEPTH, D_MODEL), 0.02),
        'w_ada': nrm((DEPTH, D_MODEL, 6 * D_MODEL), 0.5 * D_MODEL ** -0.5),
        'b_ada': nrm((DEPTH, 6 * D_MODEL), 0.02),
        'w_in': nrm((DEPTH, D_MODEL, N_IN), D_MODEL ** -0.5),
        'w_out': nrm((DEPTH, MIX_WIDTH, D_MODEL), MIX_WIDTH ** -0.5),
        'attn_lambda': nrm((DEPTH, 4, ATT_QK_DIM), 0.1),
        'attn_subln_g': 1.0 + nrm((DEPTH, ATT_V_DIM), 0.02),
        'conv_dw_w': nrm((DEPTH, CONV_K, CONV_CH), CONV_K ** -0.5),
        'conv_dw_b': nrm((DEPTH, CONV_CH), 0.02),
        'conv_ln_g': 1.0 + nrm((DEPTH, CONV_CH), 0.02),
        'conv_ln_b': nrm((DEPTH, CONV_CH), 0.02),
        'ssm_conv_w': nrm((DEPTH, SSM_CONV_K, SSM_XBC), SSM_CONV_K ** -0.5),
        'ssm_conv_b': nrm((DEPTH, SSM_XBC), 0.02),
        'ssm_dt_bias': dt0 + jnp.log(-jnp.expm1(-dt0)),
        'ssm_a_log': jnp.log(jax.random.uniform(next(ks), (DEPTH, 2, SSM_HEADS), f32, 1.0, 16.0)),
        'ssm_d': 1.0 + nrm((DEPTH, SSM_HEADS), 0.1),
        'ssm_norm_g': 1.0 + nrm((DEPTH, SSM_WIDTH), 0.02),
        'router_w': nrm((DEPTH, D_MODEL, N_EXPERTS), D_MODEL ** -0.5),
        'router_b': nrm((DEPTH, N_EXPERTS), 0.01),
        'w_gate': nrm((DEPTH, N_EXPERTS, D_MODEL, D_FF), D_MODEL ** -0.5),
        'b_gate': nrm((DEPTH, N_EXPERTS, D_FF), 0.01),
        'w_up': nrm((DEPTH, N_EXPERTS, D_MODEL, D_FF), D_MODEL ** -0.5),
        'b_up': nrm((DEPTH, N_EXPERTS, D_FF), 0.01),
        'w_down': nrm((DEPTH, N_EXPERTS, D_FF, D_MODEL), D_FF ** -0.5),
        'b_down': nrm((DEPTH, N_EXPERTS, D_MODEL), 0.01),
        'final_g': 1.0 + nrm((D_MODEL,), 0.02),
    }


def reference(x_prompt, x_sample, c, cache_k, cache_v, state_ssm, c_ctx, norm1_g, norm2_g,
              w_ada, b_ada, w_in, w_out, attn_lambda, attn_subln_g, conv_dw_w, conv_dw_b,
              conv_ln_g, conv_ln_b, ssm_conv_w, ssm_conv_b, ssm_dt_bias, ssm_a_log, ssm_d,
              ssm_norm_g, router_w, router_b, w_gate, b_gate, w_up, b_up, w_down, b_down,
              final_g):
    def layer_params(l):
        return {
            'norm1_g': norm1_g[l], 'norm2_g': norm2_g[l], 'w_ada': w_ada[l], 'b_ada': b_ada[l],
            'w_in': w_in[l], 'w_out': w_out[l], 'attn_lambda': attn_lambda[l],
            'attn_subln_g': attn_subln_g[l], 'conv_dw_w': conv_dw_w[l], 'conv_dw_b': conv_dw_b[l],
            'conv_ln_g': conv_ln_g[l], 'conv_ln_b': conv_ln_b[l], 'ssm_conv_w': ssm_conv_w[l],
            'ssm_conv_b': ssm_conv_b[l], 'ssm_dt_bias': ssm_dt_bias[l], 'ssm_a_log': ssm_a_log[l],
            'ssm_d': ssm_d[l], 'ssm_norm_g': ssm_norm_g[l], 'router_w': router_w[l],
            'router_b': router_b[l], 'w_gate': w_gate[l], 'b_gate': b_gate[l], 'w_up': w_up[l],
            'b_up': b_up[l], 'w_down': w_down[l], 'b_down': b_down[l],
        }

    x_p = x_prompt
    ctx_cond = c_ctx[None, None, :]
    ks, vs, ss = [], [], []
    for l in range(DEPTH):
        x_p, (k_l, v_l, s_l) = trunk_layer(x_p, ctx_cond, l, layer_params(l), None)
        ks.append(k_l)
        vs.append(v_l)
        ss.append(s_l)

    x_s = x_sample
    lat_cond = c[:, None, :]
    for l in range(DEPTH):
        x_s, _ = trunk_layer(x_s, lat_cond, l, layer_params(l),
                             (cache_k[:, l], cache_v[:, l], state_ssm[:, l]))

    y_prompt = rms_norm(x_p, final_g)
    y_sample = rms_norm(x_s, final_g)
    new_cache_k = jnp.stack(ks, axis=1)
    new_cache_v = jnp.stack(vs, axis=1)
    new_state_ssm = jnp.stack(ss, axis=1).astype(x_prompt.dtype)
    return (y_prompt, y_sample, new_cache_k, new_cache_v, new_state_ssm)
```

```python
import functools
import math

import numpy as np
import jax
import jax.numpy as jnp
from jax import lax
from jax.experimental import pallas as pl
from jax.experimental.pallas import tpu as pltpu

f32 = jnp.float32
bf16 = jnp.bfloat16
HIGHEST = lax.Precision.HIGHEST

D_MODEL = 1024
DEPTH = 2
GRID_W = 64
ATT_HEADS = 4
ATT_QK_DIM = 32
ATT_V_DIM = 64
ATT_WIDTH = 256
QK_WIDTH = 256
ROPE_AXIS_DIM = 16
ROPE_BASE = 10000.0
CONV_CH = 256
CONV_K = 31
SSM_WIDTH = 512
SSM_HEAD_DIM = 64
SSM_HEADS = 8
SSM_GROUPS = 2
SSM_STATE = 64
SSM_CONV_K = 5
SSM_XBC = 768
N_MAIN = 2560
N_EXPERTS = 32
TOP_K = 4
SWIGLU_LIMIT = 7.0
SWIGLU_ALPHA = 1.702
EPS = 1e-6

LANES = 128
SUBLANES = 8
TOK_TILE = 256
SSD_CHUNK = 128
CONV_HALO = 16
SSM_HALO = 8
MOE_TILE = 512
COND_ROWS = 16
VMEM_LIMIT = 56 * 1024 * 1024


def _cparams(*sem):
    return pltpu.CompilerParams(dimension_semantics=sem, vmem_limit_bytes=VMEM_LIMIT)


def _silu(x):
    return x * jax.nn.sigmoid(x)


def _ada_kernel(cond_ref, w_ref, b_ref, o_ref):
    s = _silu(cond_ref[...]).astype(bf16)
    o_ref[0] = jnp.dot(s, w_ref[0].astype(bf16), preferred_element_type=f32) + b_ref[0]


def _ada(cond, w_ada, b_ada):
    depth, d, n6 = w_ada.shape
    tn = 1536
    return pl.pallas_call(
        _ada_kernel,
        grid=(depth, n6 // tn),
        in_specs=[
            pl.BlockSpec((COND_ROWS, d), lambda l, j: (0, 0)),
            pl.BlockSpec((1, d, tn), lambda l, j: (l, 0, j)),
            pl.BlockSpec((1, 1, tn), lambda l, j: (l, 0, j)),
        ],
        out_specs=pl.BlockSpec((1, COND_ROWS, tn), lambda l, j: (l, 0, j)),
        out_shape=jax.ShapeDtypeStruct((depth, COND_ROWS, n6), f32),
        compiler_params=_cparams("arbitrary", "arbitrary"),
        name="ada",
    )(cond, w_ada, b_ada.reshape(depth, 1, n6))


class _Geom:
    def __init__(self, n_ctx_seq, ctx_len, n_lat_seq, lat_len):
        self.n_ctx_seq, self.ctx_len, self.n_lat_seq, self.lat_len = n_ctx_seq, ctx_len, n_lat_seq, lat_len
        self.n_ctx = n_ctx_seq * ctx_len
        self.n_lat = n_lat_seq * lat_len
        self.n_tok = self.n_ctx + self.n_lat
        assert ctx_len % TOK_TILE == 0 and lat_len % TOK_TILE == 0
        self.ctx_tiles = self.n_ctx // TOK_TILE
        self.tiles_per_ctx = ctx_len // TOK_TILE
        self.tiles_per_lat = lat_len // TOK_TILE
        self.n_tiles = self.n_tok // TOK_TILE

    def cond_row(self, i):
        return jnp.where(i < self.ctx_tiles, 0, 1 + (i - self.ctx_tiles) // self.tiles_per_lat)

    def seq_pos(self, i):
        j_ctx = i % self.tiles_per_ctx
        j_lat = (i - self.ctx_tiles) % self.tiles_per_lat
        is_ctx = i < self.ctx_tiles
        first = jnp.where(is_ctx, j_ctx == 0, j_lat == 0)
        last = jnp.where(is_ctx, j_ctx == self.tiles_per_ctx - 1, j_lat == self.tiles_per_lat - 1)
        return first, last

    def seq_index(self, i):
        return jnp.where(i < self.ctx_tiles, i // self.tiles_per_ctx,
                         self.n_ctx_seq + (i - self.ctx_tiles) // self.tiles_per_lat)


def _rope_tables(geom):
    t = np.arange(geom.lat_len)
    row = (t // GRID_W).astype(np.float64)
    col = (t % GRID_W).astype(np.float64)
    inv_freq = 1.0 / (ROPE_BASE ** (np.arange(0, ROPE_AXIS_DIM, 2, dtype=np.float64) / ROPE_AXIS_DIM))
    ang_r = row[:, None] * inv_freq[None, :]
    ang_c = col[:, None] * inv_freq[None, :]
    ang = np.concatenate([ang_r, ang_r, ang_c, ang_c], axis=-1)
    cos, sin = np.cos(ang), np.sin(ang)
    half = ROPE_AXIS_DIM // 2
    lane = np.arange(ATT_QK_DIM)
    first_half = (lane % ROPE_AXIS_DIM) < half
    sin_a = np.where(first_half[None, :], -sin, 0.0)
    sin_b = np.where(first_half[None, :], 0.0, sin)
    rep = LANES // ATT_QK_DIM

    def full(tab, ident):
        tab = np.tile(tab, (1, rep))
        return jnp.asarray(np.concatenate([np.full((TOK_TILE, LANES), ident), tab], axis=0), f32)

    return full(cos, 1.0), full(sin_a, 0.0), full(sin_b, 0.0)


def _in_proj_kernel(x_ref, mod_ref, g_ref, w_ref, wdt_ref, cos_ref, sa_ref, sb_ref,
                    q_ref, k_ref, v_ref, glu_ref, z_ref, xbc_ref, dt_ref):
    x = x_ref[...]
    y = x * lax.rsqrt(jnp.mean(x * x, axis=-1, keepdims=True) + EPS) * g_ref[...]
    h = (y * (1.0 + mod_ref[1:2, :]) + mod_ref[0:1, :]).astype(bf16)
    p = jnp.dot(h, w_ref[...], preferred_element_type=f32)
    dt_ref[...] = jnp.dot(h, wdt_ref[...], preferred_element_type=f32)
    cos, sa, sb = cos_ref[...], sa_ref[...], sb_ref[...]
    half = ROPE_AXIS_DIM // 2

    def rope(u):
        return u * cos + pltpu.roll(u, LANES - half, 1) * sa + pltpu.roll(u, half, 1) * sb

    scale = ATT_QK_DIM ** -0.5
    for j in range(QK_WIDTH // LANES):
        lo = j * LANES
        q_ref[:, lo:lo + LANES] = (rope(p[:, lo:lo + LANES]) * scale).astype(bf16)
        k_ref[:, lo:lo + LANES] = rope(p[:, QK_WIDTH + lo:QK_WIDTH + lo + LANES])
    o = 2 * QK_WIDTH
    v_ref[...] = p[:, o:o + ATT_WIDTH]
    o += ATT_WIDTH
    glu_ref[...] = p[:, o:o + 2 * CONV_CH]
    o += 2 * CONV_CH
    z_ref[...] = p[:, o:o + SSM_WIDTH]
    o += SSM_WIDTH
    xbc_ref[...] = p[:, o:o + SSM_XBC]


def _in_proj(geom, layer, x, mod, norm_g, w_main, w_dt, tables):
    n = geom.n_tok
    tile = lambda w: pl.BlockSpec((TOK_TILE, w), lambda i: (i, 0))
    const = lambda a: pl.BlockSpec(a.shape, lambda i: (0,) * a.ndim)
    tab_spec = pl.BlockSpec(
        (TOK_TILE, LANES),
        lambda i: (jnp.where(i < geom.ctx_tiles, 0, 1 + (i - geom.ctx_tiles) % geom.tiles_per_lat), 0))
    widths = (QK_WIDTH, QK_WIDTH, ATT_WIDTH, 2 * CONV_CH, SSM_WIDTH, SSM_XBC, LANES)
    dtypes = (bf16, f32, f32, f32, f32, f32, f32)
    return pl.pallas_call(
        _in_proj_kernel,
        grid=(geom.n_tiles,),
        in_specs=[
            tile(D_MODEL),
            pl.BlockSpec((None, None, 6, D_MODEL), lambda i: (layer, geom.cond_row(i), 0, 0)),
            const(norm_g), const(w_main), const(w_dt), tab_spec, tab_spec, tab_spec,
        ],
        out_specs=[tile(w) for w in widths],
        out_shape=[jax.ShapeDtypeStruct((n, w), dt) for w, dt in zip(widths, dtypes)],
        compiler_params=_cparams("arbitrary"),
        name="in_proj",
    )(x, mod, norm_g, w_main, w_dt, *tables)


def _attn_kernel(lam_init, al_ref, g_ref, q_ref, k_ref, v_ref, *rest):
    o_ref = rest[-1]
    al = al_ref[...]
    lam = (jnp.exp(jnp.sum(al[0:1] * al[1:2], axis=-1, keepdims=True))
           - jnp.exp(jnp.sum(al[2:3] * al[3:4], axis=-1, keepdims=True)) + lam_init)
    lane = lax.broadcasted_iota(jnp.int32, (1, LANES), 1)
    heads_per_slab = LANES // ATT_V_DIM
    for j in range(ATT_WIDTH // LANES):
        sl = slice(j * LANES, (j + 1) * LANES)
        qs = q_ref[:, sl]
        ks = k_ref[:, sl].astype(bf16)
        vs = v_ref[:, sl].astype(bf16)
        a_heads = []
        for hl in range(heads_per_slab):
            comps = []
            for c in range(2):
                lo = hl * ATT_V_DIM + c * ATT_QK_DIM
                qm = jnp.where((lane >= lo) & (lane < lo + ATT_QK_DIM), qs, jnp.zeros_like(qs))
                s = lax.dot_general(qm, ks, (((1,), (1,)), ((), ())), preferred_element_type=f32)
                e = jnp.exp(s - jnp.max(s, axis=-1, keepdims=True))
                l = jnp.sum(e, axis=-1, keepdims=True)
                comps.append(jnp.dot(e.astype(bf16), vs, preferred_element_type=f32) / l)
            a_heads.append(comps[0] - lam * comps[1])
        a = jnp.where(lane < ATT_V_DIM, a_heads[0], a_heads[1])
        a2 = a * a
        ms0 = jnp.sum(jnp.where(lane < ATT_V_DIM, a2, 0.0), axis=-1, keepdims=True)
        ms1 = jnp.sum(jnp.where(lane < ATT_V_DIM, 0.0, a2), axis=-1, keepdims=True)
        ms = jnp.where(lane < ATT_V_DIM, ms0, ms1) * (1.0 / ATT_V_DIM)
        o_ref[:, sl] = a * lax.rsqrt(ms + EPS) * g_ref[...] * (1.0 - lam_init)


def _attention(lam_init, attn_lambda, subln_g, q, k, v, n_seq, q_len, k_len, q_row0, kv_is_3d, n_out, prev_out=None):
    tq = TOK_TILE
    qb = q_len // tq
    q0 = q_row0 // tq
    q_spec = pl.BlockSpec((tq, ATT_WIDTH), lambda b, i: (q0 + b * qb + i, 0))
    if kv_is_3d:
        kv_spec = pl.BlockSpec((None, k_len, ATT_WIDTH), lambda b, i: (b, 0, 0))
    else:
        kv_spec = pl.BlockSpec((k_len, ATT_WIDTH), lambda b, i: (q_row0 // k_len + b, 0))
    in_specs = [
        pl.BlockSpec(attn_lambda.shape, lambda b, i: (0, 0)),
        pl.BlockSpec(subln_g.shape, lambda b, i: (0, 0)),
        q_spec, kv_spec, kv_spec,
    ]
    args = [attn_lambda, subln_g, q, k, v]
    aliases = {}
    if prev_out is not None:
        in_specs.append(pl.BlockSpec(memory_space=pl.ANY))
        args.append(prev_out)
        aliases = {5: 0}
    return pl.pallas_call(
        functools.partial(_attn_kernel, lam_init),
        grid=(n_seq, qb),
        in_specs=in_specs,
        out_specs=q_spec,
        out_shape=jax.ShapeDtypeStruct((n_out, ATT_WIDTH), f32),
        input_output_aliases=aliases,
        compiler_params=_cparams("arbitrary", "arbitrary"),
        name="attention",
    )(*args)


def _halo_specs(geom, width, halo):
    per = TOK_TILE // halo
    n_halo = geom.n_tok // halo
    prev = pl.BlockSpec((halo, width), lambda i: (jnp.maximum(i * per - 1, 0), 0))
    nxt = pl.BlockSpec((halo, width), lambda i: (jnp.minimum((i + 1) * per, n_halo - 1), 0))
    return prev, nxt


def _cconv_kernel(geom, cur_ref, prev_ref, next_ref, w_ref, b_ref, lg_ref, lb_ref, o_ref, pad_ref):
    first, last = geom.seq_pos(pl.program_id(0))

    def glu(ref):
        return ref[:, :CONV_CH] * jax.nn.sigmoid(ref[:, CONV_CH:])

    pad_ref[0:CONV_HALO, :] = jnp.where(first, 0.0, glu(prev_ref))
    pad_ref[CONV_HALO:CONV_HALO + TOK_TILE, :] = glu(cur_ref)
    pad_ref[CONV_HALO + TOK_TILE:, :] = jnp.where(last, 0.0, glu(next_ref))
    rows = 64
    base = CONV_HALO - CONV_K // 2
    for r0 in range(0, TOK_TILE, rows):
        acc = jnp.broadcast_to(b_ref[...], (rows, CONV_CH))
        for k in range(CONV_K):
            acc = acc + w_ref[k:k + 1, :] * pad_ref[r0 + base + k:r0 + base + k + rows, :]
        xc = acc - jnp.mean(acc, axis=-1, keepdims=True)
        var = jnp.mean(xc * xc, axis=-1, keepdims=True)
        o_ref[r0:r0 + rows, :] = _silu(xc * lax.rsqrt(var + EPS) * lg_ref[...] + lb_ref[...])


def _cconv(geom, glu, w, b, ln_g, ln_b):
    prev, nxt = _halo_specs(geom, 2 * CONV_CH, CONV_HALO)
    const = lambda a: pl.BlockSpec(a.shape, lambda i: (0,) * a.ndim)
    return pl.pallas_call(
        functools.partial(_cconv_kernel, geom),
        grid=(geom.n_tiles,),
        in_specs=[pl.BlockSpec((TOK_TILE, 2 * CONV_CH), lambda i: (i, 0)), prev, nxt,
                  const(w), const(b), const(ln_g), const(ln_b)],
        out_specs=pl.BlockSpec((TOK_TILE, CONV_CH), lambda i: (i, 0)),
        out_shape=jax.ShapeDtypeStruct((geom.n_tok, CONV_CH), f32),
        scratch_shapes=[pltpu.VMEM((TOK_TILE + 2 * CONV_HALO, CONV_CH), f32)],
        compiler_params=_cparams("arbitrary"),
        name="conformer_conv",
    )(glu, glu, glu, w, b, ln_g, ln_b)


def _softplus(x):
    return jnp.maximum(x, 0.0) + jnp.log1p(jnp.exp(-jnp.abs(x)))


def _ssd_kernel(geom, direction, cur_ref, prev_ref, next_ref, dt_ref, init_ref, cw_ref, cb_ref, dtb_ref,
                alog_ref, dskip_ref, y_ref, fin_ref, pad_ref, xc_ref, st_ref):
    fwd = direction == 0
    i = pl.program_id(0)
    t = i if fwd else geom.n_tiles - 1 - i
    first, last = geom.seq_pos(t)
    enter, leave = (first, last) if fwd else (last, first)
    is_lat = t >= geom.ctx_tiles

    @pl.when(enter)
    def _():
        st_ref[...] = jnp.where(is_lat, init_ref[...], 0.0)

    pad_ref[0:SSM_HALO, :] = jnp.where(first, 0.0, prev_ref[...])
    pad_ref[SSM_HALO:SSM_HALO + TOK_TILE, :] = cur_ref[...]
    pad_ref[SSM_HALO + TOK_TILE:, :] = jnp.where(last, 0.0, next_ref[...])
    base = SSM_HALO - SSM_CONV_K // 2
    acc = jnp.broadcast_to(cb_ref[...], (TOK_TILE, SSM_XBC))
    for k in range(SSM_CONV_K):
        acc = acc + cw_ref[k:k + 1, :] * pad_ref[base + k:base + k + TOK_TILE, :]
    xc_ref[...] = _silu(acc)

    L = SSD_CHUNK
    a_row = -jnp.exp(alog_ref[...])
    ri = lax.broadcasted_iota(jnp.int32, (L, L), 0)
    ci = lax.broadcasted_iota(jnp.int32, (L, L), 1)
    keep = (ci <= ri) if fwd else (ci >= ri)
    tri_col = keep.astype(f32)
    tri_row = ((ri <= ci) if fwd else (ri >= ci)).astype(f32)
    lane = lax.broadcasted_iota(jnp.int32, (1, LANES), 1)
    er = lax.broadcasted_iota(jnp.int32, (LANES, SSM_WIDTH), 0)
    ec = lax.broadcasted_iota(jnp.int32, (LANES, SSM_WIDTH), 1)
    expand = (er == direction * SSM_HEADS + ec // SSM_HEAD_DIM).astype(f32)
    heads_per_group = SSM_HEADS // SSM_GROUPS
    heads_per_slab = LANES // SSM_HEAD_DIM
    tot_at = L - 1 if fwd else 0

    chunks = range(TOK_TILE // L)
    for cidx in (chunks if fwd else reversed(chunks)):
        r0 = cidx * L
        xs = xc_ref[r0:r0 + L, 0:SSM_WIDTH]
        bm = xc_ref[r0:r0 + L, SSM_WIDTH:SSM_WIDTH + LANES]
        cm = xc_ref[r0:r0 + L, SSM_WIDTH + LANES:SSM_XBC]
        dt = _softplus(dt_ref[r0:r0 + L, :] + dtb_ref[...])
        da = dt * a_row
        acum_col = jnp.dot(tri_col, da, precision=HIGHEST, preferred_element_type=f32)
        acum_row = jnp.dot(da.T, tri_row, precision=HIGHEST, preferred_element_type=f32)
        tot_row = acum_col[tot_at:tot_at + 1, :]
        tot_col = acum_row[:, tot_at:tot_at + 1]
        decay_out = jnp.exp(tot_col - acum_row)
        decay_in = jnp.exp(acum_col)
        chunk_decay = jnp.exp(tot_row)
        xdt = (xs * jnp.dot(dt, expand, precision=HIGHEST, preferred_element_type=f32)).astype(bf16)
        bt = bm.T
        bb = bm.astype(bf16)
        ys = []
        for g in range(SSM_GROUPS):
            in_group = (lane >= g * SSM_STATE) & (lane < (g + 1) * SSM_STATE)
            cg = jnp.where(in_group, cm, 0.0)
            gmat = lax.dot_general(cg.astype(bf16), bb, (((1,), (1,)), ((), ())), preferred_element_type=f32)
            for hh in range(heads_per_group):
                h = g * heads_per_group + hh
                hl = direction * SSM_HEADS + h
                slab = xdt[:, (h // heads_per_slab) * LANES:(h // heads_per_slab + 1) * LANES]
                seg = acum_col[:, hl:hl + 1] - acum_row[hl:hl + 1, :]
                m = (gmat * jnp.exp(jnp.where(keep, seg, -jnp.inf))).astype(bf16)
                y_h = jnp.dot(m, slab, preferred_element_type=f32)
                c_in = (cg * decay_in[:, hl:hl + 1]).astype(bf16)
                state = st_ref[h]
                y_h = y_h + jnp.dot(c_in, state.astype(bf16), preferred_element_type=f32)
                b_out = (bt * decay_out[hl:hl + 1, :]).astype(bf16)
                st_ref[h] = chunk_decay[:, hl:hl + 1] * state + jnp.dot(b_out, slab, preferred_element_type=f32)
                ys.append(y_h)
        for j in range(SSM_WIDTH // LANES):
            y_slab = jnp.where(lane < SSM_HEAD_DIM, ys[heads_per_slab * j], ys[heads_per_slab * j + 1])
            if fwd:
                y_slab = y_slab + dskip_ref[:, j * LANES:(j + 1) * LANES] * xs[:, j * LANES:(j + 1) * LANES]
            y_ref[r0:r0 + L, j * LANES:(j + 1) * LANES] = y_slab

    @pl.when(leave)
    def _():
        fin_ref[...] = st_ref[...]


def _ssd(geom, direction, xbc, dt, init, conv_w, conv_b, dt_bias, a_log, d_skip):
    nt = geom.n_tiles
    tidx = (lambda i: i) if direction == 0 else (lambda i: nt - 1 - i)
    per = TOK_TILE // SSM_HALO
    n_halo = geom.n_tok // SSM_HALO
    n_seq = geom.n_ctx_seq + geom.n_lat_seq
    const = lambda a: pl.BlockSpec(a.shape, lambda i: (0,) * a.ndim)
    st_block = (None, SSM_HEADS, LANES, LANES)
    y, fin = pl.pallas_call(
        functools.partial(_ssd_kernel, geom, direction),
        grid=(nt,),
        in_specs=[
            pl.BlockSpec((TOK_TILE, SSM_XBC), lambda i: (tidx(i), 0)),
            pl.BlockSpec((SSM_HALO, SSM_XBC), lambda i: (jnp.maximum(tidx(i) * per - 1, 0), 0)),
            pl.BlockSpec((SSM_HALO, SSM_XBC), lambda i: (jnp.minimum((tidx(i) + 1) * per, n_halo - 1), 0)),
            pl.BlockSpec((TOK_TILE, LANES), lambda i: (tidx(i), 0)),
            pl.BlockSpec(st_block, lambda i: (jnp.maximum(geom.seq_index(tidx(i)) - geom.n_ctx_seq, 0), 0, 0, 0)),
            const(conv_w), const(conv_b), const(dt_bias), const(a_log), const(d_skip),
        ],
        out_specs=[
            pl.BlockSpec((TOK_TILE, SSM_WIDTH), lambda i: (tidx(i), 0)),
            pl.BlockSpec(st_block, lambda i: (geom.seq_index(tidx(i)), 0, 0, 0)),
        ],
        out_shape=[
            jax.ShapeDtypeStruct((geom.n_tok, SSM_WIDTH), f32),
            jax.ShapeDtypeStruct((n_seq, SSM_HEADS, LANES, LANES), f32),
        ],
        scratch_shapes=[
            pltpu.VMEM((TOK_TILE + 2 * SSM_HALO, SSM_XBC), f32),
            pltpu.VMEM((TOK_TILE, SSM_XBC), f32),
            pltpu.VMEM((SSM_HEADS, LANES, LANES), f32),
        ],
        compiler_params=_cparams("arbitrary"),
        name="ssd_scan",
    )(xbc, xbc, xbc, dt, init, conv_w, conv_b, dt_bias, a_log, d_skip)
    return y, fin


def _pad_states(s):
    b = s.shape[0]
    st = jnp.swapaxes(s, -1, -2)
    out = jnp.zeros((b, SSM_HEADS, LANES, LANES), f32)
    hpg = SSM_HEADS // SSM_GROUPS
    for h in range(SSM_HEADS):
        g, side = h // hpg, h % (LANES // SSM_HEAD_DIM)
        out = out.at[:, h, g * SSM_STATE:(g + 1) * SSM_STATE,
                     side * SSM_HEAD_DIM:(side + 1) * SSM_HEAD_DIM].set(st[:, h])
    return out


def _unpad_states(fin):
    hpg = SSM_HEADS // SSM_GROUPS
    outs = []
    for h in range(SSM_HEADS):
        g, side = h // hpg, h % (LANES // SSM_HEAD_DIM)
        q = fin[:, h, g * SSM_STATE:(g + 1) * SSM_STATE, side * SSM_HEAD_DIM:(side + 1) * SSM_HEAD_DIM]
        outs.append(jnp.swapaxes(q, -1, -2))
    return jnp.stack(outs, axis=1)


def _out_proj_kernel(attn_ref, conv_ref, yf_ref, yb_ref, z_ref, x_ref, mod_ref, sg_ref, w_ref, g2_ref,
                     rw_ref, rb_ref, xo_ref, h2_ref, lg_ref):
    gated = (yf_ref[...] + yb_ref[...]) * _silu(z_ref[...])
    ssm = gated * lax.rsqrt(jnp.mean(gated * gated, axis=-1, keepdims=True) + EPS) * sg_ref[...]
    o1 = ATT_WIDTH
    o2 = ATT_WIDTH + CONV_CH
    mix = (jnp.dot(attn_ref[...].astype(bf16), w_ref[0:o1, :], preferred_element_type=f32)
           + jnp.dot(conv_ref[...].astype(bf16), w_ref[o1:o2, :], preferred_element_type=f32)
           + jnp.dot(ssm.astype(bf16), w_ref[o2:, :], preferred_element_type=f32))
    x = x_ref[...] + mod_ref[2:3, :] * mix
    xo_ref[...] = x
    y = x * lax.rsqrt(jnp.mean(x * x, axis=-1, keepdims=True) + EPS) * g2_ref[...]
    h2 = y * (1.0 + mod_ref[4:5, :]) + mod_ref[3:4, :]
    h2_ref[...] = h2.astype(bf16)
    lg_ref[...] = jnp.dot(h2, rw_ref[...], precision=HIGHEST, preferred_element_type=f32) + rb_ref[...]


def _out_proj(geom, layer, attn, conv, y_f, y_b, z, x, mod, ssm_norm_g, w_out, norm2_g, router_w, router_b):
    n = geom.n_tok
    tile = lambda w: pl.BlockSpec((TOK_TILE, w), lambda i: (i, 0))
    const = lambda a: pl.BlockSpec(a.shape, lambda i: (0,) * a.ndim)
    return pl.pallas_call(
        _out_proj_kernel,
        grid=(geom.n_tiles,),
        in_specs=[
            tile(ATT_WIDTH), tile(CONV_CH), tile(SSM_WIDTH), tile(SSM_WIDTH), tile(SSM_WIDTH), tile(D_MODEL),
            pl.BlockSpec((None, None, 6, D_MODEL), lambda i: (layer, geom.cond_row(i), 0, 0)),
            const(ssm_norm_g), const(w_out), const(norm2_g), const(router_w), const(router_b),
        ],
        out_specs=[tile(D_MODEL), tile(D_MODEL), tile(LANES)],
        out_shape=[jax.ShapeDtypeStruct((n, D_MODEL), f32), jax.ShapeDtypeStruct((n, D_MODEL), bf16),
                   jax.ShapeDtypeStruct((n, LANES), f32)],
        compiler_params=_cparams("arbitrary"),
        name="out_proj",
    )(attn, conv, y_f, y_b, z, x, mod, ssm_norm_g, w_out, norm2_g, router_w, router_b)


def _moe_kernel(be_ref, nused_ref, x_ref, wg_ref, bg_ref, wu_ref, bu_ref, wd_ref, bd_ref, o_ref,
                wg_s, wu_s, wd_s):
    i = pl.program_id(0)
    prev_e = be_ref[jnp.maximum(i - 1, 0)]
    new_expert = (i == 0) | (be_ref[i] != prev_e)

    @pl.when(new_expert)
    def _():
        wg_s[...] = wg_ref[0].astype(bf16)
        wu_s[...] = wu_ref[0].astype(bf16)
        wd_s[...] = wd_ref[0].astype(bf16)

    @pl.when(i < nused_ref[0])
    def _():
        x = x_ref[...]
        g = jnp.dot(x, wg_s[...], preferred_element_type=f32) + bg_ref[0]
        u = jnp.dot(x, wu_s[...], preferred_element_type=f32) + bu_ref[0]
        g = jnp.minimum(g, SWIGLU_LIMIT)
        u = jnp.clip(u, -SWIGLU_LIMIT, SWIGLU_LIMIT)
        act = g * jax.nn.sigmoid(SWIGLU_ALPHA * g) * (u + 1.0)
        o_ref[...] = jnp.dot(act.astype(bf16), wd_s[...], preferred_element_type=f32) + bd_ref[0]

    @pl.when(i >= nused_ref[0])
    def _():
        o_ref[...] = jnp.zeros_like(o_ref)


def _moe(layer, xs, block_e, n_used, w_gate, b_gate, w_up, b_up, w_down, b_down):
    n_rows, d = xs.shape
    n_blocks = n_rows // MOE_TILE
    d_ff = w_gate.shape[-1]
    w_spec = lambda r, c: pl.BlockSpec((None, 1, r, c), lambda i, be, nu: (layer, be[i], 0, 0))
    b_spec = lambda c: pl.BlockSpec((None, 1, 1, c), lambda i, be, nu: (layer, be[i], 0, 0))
    row_spec = pl.BlockSpec((MOE_TILE, d), lambda i, be, nu: (i, 0))
    depth, n_e = b_gate.shape[:2]
    return pl.pallas_call(
        _moe_kernel,
        grid_spec=pltpu.PrefetchScalarGridSpec(
            num_scalar_prefetch=2,
            grid=(n_blocks,),
            in_specs=[row_spec, w_spec(d, d_ff), b_spec(d_ff), w_spec(d, d_ff), b_spec(d_ff),
                      w_spec(d_ff, d), b_spec(d)],
            out_specs=row_spec,
            scratch_shapes=[pltpu.VMEM((d, d_ff), bf16), pltpu.VMEM((d, d_ff), bf16), pltpu.VMEM((d_ff, d), bf16)],
        ),
        out_shape=jax.ShapeDtypeStruct((n_rows, d), f32),
        compiler_params=_cparams("arbitrary"),
        name="moe_experts",
    )(block_e, n_used, xs, w_gate, b_gate.reshape(depth, n_e, 1, d_ff), w_up, b_up.reshape(depth, n_e, 1, d_ff),
      w_down, b_down.reshape(depth, n_e, 1, d))


def _route(logits, n_blocks):
    n = logits.shape[0]
    top_val, top_idx = lax.top_k(logits, TOP_K)
    gate_w = jax.nn.softmax(top_val, axis=-1)
    flat_e = top_idx.reshape(-1)
    onehot = (flat_e[:, None] == jnp.arange(N_EXPERTS, dtype=flat_e.dtype)[None, :]).astype(jnp.int32)
    csum = jnp.cumsum(onehot, axis=0)
    counts = csum[-1]
    rank = jnp.take_along_axis(csum, flat_e[:, None], axis=1)[:, 0] - 1
    nb = (counts + MOE_TILE - 1) // MOE_TILE
    blk_end = jnp.cumsum(nb)
    blk_start = blk_end - nb
    dest = blk_start[flat_e] * MOE_TILE + rank
    block_e = jnp.minimum(jnp.searchsorted(blk_end, jnp.arange(n_blocks, dtype=blk_end.dtype), side='right'),
                          N_EXPERTS - 1).astype(jnp.int32)
    n_used = blk_end[-1:].astype(jnp.int32)
    row_tok = jnp.zeros((n_blocks * MOE_TILE,), jnp.int32).at[dest].set(jnp.arange(n * TOP_K, dtype=jnp.int32) // TOP_K)
    return gate_w, dest.reshape(n, TOP_K), block_e, n_used, row_tok


def _combine_kernel(final, x_ref, rows_ref, gw_ref, mod_ref, *rest):
    ffn = rows_ref[0] * gw_ref[:, 0:1]
    for k in range(1, TOP_K):
        ffn = ffn + rows_ref[k] * gw_ref[:, k:k + 1]
    x = x_ref[...] + mod_ref[5:6, :] * ffn
    if final:
        fg_ref, xo_ref, y_ref = rest
        xo_ref[...] = x
        y_ref[...] = x * lax.rsqrt(jnp.mean(x * x, axis=-1, keepdims=True) + EPS) * fg_ref[...]
    else:
        rest[0][...] = x


def _combine(geom, layer, x, rows, gate_w, mod, final_g):
    n = geom.n_tok
    final = final_g is not None
    tile = pl.BlockSpec((TOK_TILE, D_MODEL), lambda i: (i, 0))
    in_specs = [
        tile,
        pl.BlockSpec((TOP_K, TOK_TILE, D_MODEL), lambda i: (0, i, 0)),
        pl.BlockSpec((TOK_TILE, TOP_K), lambda i: (i, 0)),
        pl.BlockSpec((None, None, 6, D_MODEL), lambda i: (layer, geom.cond_row(i), 0, 0)),
    ]
    args = [x, rows, gate_w, mod]
    out_specs, out_shape = [tile], [jax.ShapeDtypeStruct((n, D_MODEL), f32)]
    if final:
        in_specs.append(pl.BlockSpec(final_g.shape, lambda i: (0, 0)))
        args.append(final_g)
        out_specs.append(tile)
        out_shape.append(jax.ShapeDtypeStruct((n, D_MODEL), f32))
    return pl.pallas_call(
        functools.partial(_combine_kernel, final),
        grid=(geom.n_tiles,),
        in_specs=in_specs,
        out_specs=out_specs,
        out_shape=out_shape,
        compiler_params=_cparams("arbitrary"),
        name="moe_combine",
    )(*args)


def _pad_lanes(a, width=LANES):
    return jnp.pad(a, [(0, 0)] * (a.ndim - 1) + [(0, width - a.shape[-1])])


def kernel(x_prompt, x_sample, c, cache_k, cache_v, state_ssm, c_ctx, norm1_g, norm2_g, w_ada, b_ada, w_in, w_out, attn_lambda, attn_subln_g, conv_dw_w, conv_dw_b, conv_ln_g, conv_ln_b, ssm_conv_w, ssm_conv_b, ssm_dt_bias, ssm_a_log, ssm_d, ssm_norm_g, router_w, router_b, w_gate, b_gate, w_up, b_up, w_down, b_down, final_g):
    n_ctx_seq, ctx_len, d = x_prompt.shape
    n_lat_seq, lat_len, _ = x_sample.shape
    past_len = cache_k.shape[-2]
    geom = _Geom(n_ctx_seq, ctx_len, n_lat_seq, lat_len)
    n = geom.n_tok

    x = jnp.concatenate([x_prompt.reshape(geom.n_ctx, d), x_sample.reshape(geom.n_lat, d)], axis=0)
    cond = jnp.concatenate([c_ctx[None, :], c, jnp.zeros((COND_ROWS - 1 - n_lat_seq, d), f32)], axis=0)
    mod = _ada(cond, w_ada, b_ada).reshape(DEPTH, COND_ROWS, 6, d)
    tables = _rope_tables(geom)

    n_assign = n * TOP_K
    n_blocks = -(-n_assign // MOE_TILE) + N_EXPERTS

    new_k, new_v, new_s = [], [], []
    y_final = None
    for l in range(DEPTH):
        w_main = w_in[l, :, :N_MAIN].astype(bf16)
        w_dt = _pad_lanes(w_in[l, :, N_MAIN:]).astype(bf16)
        q, k, v, glu, z, xbc, dt = _in_proj(geom, l, x, mod, norm1_g[l][None, :], w_main, w_dt, tables)

        lam_init = 0.8 - 0.6 * math.exp(-0.3 * l)
        subln = jnp.tile(attn_subln_g[l], LANES // ATT_V_DIM)[None, :]
        k_cache = cache_k[:, l].transpose(0, 3, 1, 2, 4).reshape(n_lat_seq, past_len, QK_WIDTH)
        v_cache = cache_v[:, l].transpose(0, 2, 1, 3).reshape(n_lat_seq, past_len, ATT_WIDTH)
        k_all = jnp.concatenate([k[geom.n_ctx:].reshape(n_lat_seq, lat_len, QK_WIDTH), k_cache], axis=1).astype(bf16)
        v_all = jnp.concatenate([v[geom.n_ctx:].reshape(n_lat_seq, lat_len, ATT_WIDTH), v_cache], axis=1).astype(bf16)
        attn = _attention(lam_init, attn_lambda[l], subln, q, k, v, n_ctx_seq, ctx_len, ctx_len, 0, False, n)
        attn = _attention(lam_init, attn_lambda[l], subln, q, k_all, v_all, n_lat_seq, lat_len,
                          lat_len + past_len, geom.n_ctx, True, n, prev_out=attn)

        conv = _cconv(geom, glu, conv_dw_w[l], conv_dw_b[l][None, :], conv_ln_g[l][None, :], conv_ln_b[l][None, :])

        dt_bias = _pad_lanes(ssm_dt_bias[l].reshape(1, 2 * SSM_HEADS))
        a_log = _pad_lanes(ssm_a_log[l].reshape(1, 2 * SSM_HEADS))
        d_skip = jnp.repeat(ssm_d[l], SSM_HEAD_DIM)[None, :]
        ys, fins = [], []
        for direction in range(2):
            init = _pad_states(state_ssm[:, l, direction])
            y_d, fin = _ssd(geom, direction, xbc, dt, init, ssm_conv_w[l], ssm_conv_b[l][None, :], dt_bias, a_log, d_skip)
            ys.append(y_d)
            fins.append(_unpad_states(fin[:n_ctx_seq]))

        x, h2, logits = _out_proj(geom, l, attn, conv, ys[0], ys[1], z, x, mod, ssm_norm_g[l][None, :],
                                  w_out[l].astype(bf16), norm2_g[l][None, :],
                                  _pad_lanes(router_w[l]), _pad_lanes(router_b[l][None, :]))

        gate_w, dest, block_e, n_used, row_tok = _route(logits[:, :N_EXPERTS], n_blocks)
        xs = jnp.take(h2, row_tok, axis=0)
        eo = _moe(l, xs, block_e, n_used, w_gate, b_gate, w_up, b_up, w_down, b_down)
        rows = jnp.take(eo, dest.T, axis=0)
        if l == DEPTH - 1:
            x, y_final = _combine(geom, l, x, rows, gate_w, mod, final_g[None, :])
        else:
            x, = _combine(geom, l, x, rows, gate_w, mod, None)

        new_k.append(k[:geom.n_ctx].reshape(n_ctx_seq, ctx_len, ATT_HEADS, 2, ATT_QK_DIM).transpose(0, 2, 3, 1, 4))
        new_v.append(v[:geom.n_ctx].reshape(n_ctx_seq, ctx_len, ATT_HEADS, ATT_V_DIM).transpose(0, 2, 1, 3))
        new_s.append(jnp.stack(fins, axis=1))

    y_prompt = y_final[:geom.n_ctx].reshape(n_ctx_seq, ctx_len, d)
    y_sample = y_final[geom.n_ctx:].reshape(n_lat_seq, lat_len, d)
    return (y_prompt, y_sample, jnp.stack(new_k, axis=1), jnp.stack(new_v, axis=1), jnp.stack(new_s, axis=1))
```

```python
import functools
import math

import numpy as np
import jax
import jax.numpy as jnp
from jax import lax
from jax.experimental import pallas as pl
from jax.experimental.pallas import tpu as pltpu

f32 = jnp.float32
bf16 = jnp.bfloat16
HIGHEST = lax.Precision.HIGHEST

D_MODEL = 1024
DEPTH = 2
GRID_W = 64
ATT_HEADS = 4
ATT_QK_DIM = 32
ATT_V_DIM = 64
ATT_WIDTH = 256
QK_WIDTH = 256
ROPE_AXIS_DIM = 16
ROPE_BASE = 10000.0
CONV_CH = 256
CONV_K = 31
SSM_WIDTH = 512
SSM_HEAD_DIM = 64
SSM_HEADS = 8
SSM_GROUPS = 2
SSM_STATE = 64
SSM_CONV_K = 5
SSM_XBC = 768
N_MAIN = 2560
N_EXPERTS = 32
TOP_K = 4
SWIGLU_LIMIT = 7.0
SWIGLU_ALPHA = 1.702
EPS = 1e-6

LANES = 128
SUBLANES = 8
TOK_TILE = 256
SSD_CHUNK = 128
CONV_HALO = 16
SSM_HALO = 8
MOE_TILE = 512
COND_ROWS = 16
VMEM_LIMIT = 56 * 1024 * 1024


def _cparams(*sem):
    return pltpu.CompilerParams(dimension_semantics=sem, vmem_limit_bytes=VMEM_LIMIT)


def _silu(x):
    return x * jax.nn.sigmoid(x)


def _ada_kernel(cond_ref, w_ref, b_ref, o_ref):
    s = _silu(cond_ref[...]).astype(bf16)
    o_ref[0] = jnp.dot(s, w_ref[0].astype(bf16), preferred_element_type=f32) + b_ref[0]


def _ada(cond, w_ada, b_ada):
    depth, d, n6 = w_ada.shape
    tn = 1536
    return pl.pallas_call(
        _ada_kernel,
        grid=(depth, n6 // tn),
        in_specs=[
            pl.BlockSpec((COND_ROWS, d), lambda l, j: (0, 0)),
            pl.BlockSpec((1, d, tn), lambda l, j: (l, 0, j)),
            pl.BlockSpec((1, 1, tn), lambda l, j: (l, 0, j)),
        ],
        out_specs=pl.BlockSpec((1, COND_ROWS, tn), lambda l, j: (l, 0, j)),
        out_shape=jax.ShapeDtypeStruct((depth, COND_ROWS, n6), f32),
        compiler_params=_cparams("arbitrary", "arbitrary"),
        name="ada",
    )(cond, w_ada, b_ada.reshape(depth, 1, n6))


class _Geom:
    def __init__(self, n_ctx_seq, ctx_len, n_lat_seq, lat_len):
        self.n_ctx_seq, self.ctx_len, self.n_lat_seq, self.lat_len = n_ctx_seq, ctx_len, n_lat_seq, lat_len
        self.n_ctx = n_ctx_seq * ctx_len
        self.n_lat = n_lat_seq * lat_len
        self.n_tok = self.n_ctx + self.n_lat
        assert ctx_len % TOK_TILE == 0 and lat_len % TOK_TILE == 0
        self.ctx_tiles = self.n_ctx // TOK_TILE
        self.tiles_per_ctx = ctx_len // TOK_TILE
        self.tiles_per_lat = lat_len // TOK_TILE
        self.n_tiles = self.n_tok // TOK_TILE

    def cond_row(self, i):
        return jnp.where(i < self.ctx_tiles, 0, 1 + (i - self.ctx_tiles) // self.tiles_per_lat)

    def seq_pos(self, i):
        j_ctx = i % self.tiles_per_ctx
        j_lat = (i - self.ctx_tiles) % self.tiles_per_lat
        is_ctx = i < self.ctx_tiles
        first = jnp.where(is_ctx, j_ctx == 0, j_lat == 0)
        last = jnp.where(is_ctx, j_ctx == self.tiles_per_ctx - 1, j_lat == self.tiles_per_lat - 1)
        return first, last

    def seq_index(self, i):
        return jnp.where(i < self.ctx_tiles, i // self.tiles_per_ctx,
                         self.n_ctx_seq + (i - self.ctx_tiles) // self.tiles_per_lat)


def _rope_tables(geom):
    t = np.arange(geom.lat_len)
    row = (t // GRID_W).astype(np.float64)
    col = (t % GRID_W).astype(np.float64)
    inv_freq = 1.0 / (ROPE_BASE ** (np.arange(0, ROPE_AXIS_DIM, 2, dtype=np.float64) / ROPE_AXIS_DIM))
    ang_r = row[:, None] * inv_freq[None, :]
    ang_c = col[:, None] * inv_freq[None, :]
    ang = np.concatenate([ang_r, ang_r, ang_c, ang_c], axis=-1)
    cos, sin = np.cos(ang), np.sin(ang)
    half = ROPE_AXIS_DIM // 2
    lane = np.arange(ATT_QK_DIM)
    first_half = (lane % ROPE_AXIS_DIM) < half
    sin_a = np.where(first_half[None, :], -sin, 0.0)
    sin_b = np.where(first_half[None, :], 0.0, sin)
    rep = LANES // ATT_QK_DIM

    def full(tab, ident):
        tab = np.tile(tab, (1, rep))
        return jnp.asarray(np.concatenate([np.full((TOK_TILE, LANES), ident), tab], axis=0), f32)

    return full(cos, 1.0), full(sin_a, 0.0), full(sin_b, 0.0)


def _in_proj_kernel(x_ref, mod_ref, g_ref, w_ref, wdt_ref, cos_ref, sa_ref, sb_ref,
                    q_ref, k_ref, v_ref, glu_ref, z_ref, xbc_ref, dt_ref):
    x = x_ref[...]
    y = x * lax.rsqrt(jnp.mean(x * x, axis=-1, keepdims=True) + EPS) * g_ref[...]
    h = (y * (1.0 + mod_ref[1:2, :]) + mod_ref[0:1, :]).astype(bf16)
    p = jnp.dot(h, w_ref[...], preferred_element_type=f32)
    dt_ref[...] = jnp.dot(h, wdt_ref[...], preferred_element_type=f32)
    cos, sa, sb = cos_ref[...], sa_ref[...], sb_ref[...]
    half = ROPE_AXIS_DIM // 2

    def rope(u):
        return u * cos + pltpu.roll(u, LANES - half, 1) * sa + pltpu.roll(u, half, 1) * sb

    scale = ATT_QK_DIM ** -0.5
    for j in range(QK_WIDTH // LANES):
        lo = j * LANES
        q_ref[:, lo:lo + LANES] = (rope(p[:, lo:lo + LANES]) * scale).astype(bf16)
        k_ref[:, lo:lo + LANES] = rope(p[:, QK_WIDTH + lo:QK_WIDTH + lo + LANES])
    o = 2 * QK_WIDTH
    v_ref[...] = p[:, o:o + ATT_WIDTH]
    o += ATT_WIDTH
    glu_ref[...] = p[:, o:o + 2 * CONV_CH]
    o += 2 * CONV_CH
    z_ref[...] = p[:, o:o + SSM_WIDTH]
    o += SSM_WIDTH
    xbc_ref[...] = p[:, o:o + SSM_XBC]


def _in_proj(geom, layer, x, mod, norm_g, w_main, w_dt, tables):
    n = geom.n_tok
    tile = lambda w: pl.BlockSpec((TOK_TILE, w), lambda i: (i, 0))
    const = lambda a: pl.BlockSpec(a.shape, lambda i: (0,) * a.ndim)
    tab_spec = pl.BlockSpec(
        (TOK_TILE, LANES),
        lambda i: (jnp.where(i < geom.ctx_tiles, 0, 1 + (i - geom.ctx_tiles) % geom.tiles_per_lat), 0))
    widths = (QK_WIDTH, QK_WIDTH, ATT_WIDTH, 2 * CONV_CH, SSM_WIDTH, SSM_XBC, LANES)
    dtypes = (bf16, f32, f32, f32, f32, f32, f32)
    return pl.pallas_call(
        _in_proj_kernel,
        grid=(geom.n_tiles,),
        in_specs=[
            tile(D_MODEL),
            pl.BlockSpec((None, None, 6, D_MODEL), lambda i: (layer, geom.cond_row(i), 0, 0)),
            const(norm_g), const(w_main), const(w_dt), tab_spec, tab_spec, tab_spec,
        ],
        out_specs=[tile(w) for w in widths],
        out_shape=[jax.ShapeDtypeStruct((n, w), dt) for w, dt in zip(widths, dtypes)],
        compiler_params=_cparams("arbitrary"),
        name="in_proj",
    )(x, mod, norm_g, w_main, w_dt, *tables)


def _attn_kernel(lam_init, al_ref, g_ref, q_ref, k_ref, v_ref, *rest):
    o_ref = rest[-1]
    al = al_ref[...]
    lam = (jnp.exp(jnp.sum(al[0:1] * al[1:2], axis=-1, keepdims=True))
           - jnp.exp(jnp.sum(al[2:3] * al[3:4], axis=-1, keepdims=True)) + lam_init)
    lane = lax.broadcasted_iota(jnp.int32, (1, LANES), 1)
    heads_per_slab = LANES // ATT_V_DIM
    for j in range(ATT_WIDTH // LANES):
        sl = slice(j * LANES, (j + 1) * LANES)
        qs = q_ref[:, sl]
        ks = k_ref[:, sl].astype(bf16)
        vs = v_ref[:, sl].astype(bf16)
        a_heads = []
        for hl in range(heads_per_slab):
            comps = []
            for c in range(2):
                lo = hl * ATT_V_DIM + c * ATT_QK_DIM
                qm = jnp.where((lane >= lo) & (lane < lo + ATT_QK_DIM), qs, jnp.zeros_like(qs))
                s = lax.dot_general(qm, ks, (((1,), (1,)), ((), ())), preferred_element_type=f32)
                e = jnp.exp(s - jnp.max(s, axis=-1, keepdims=True))
                l = jnp.sum(e, axis=-1, keepdims=True)
                comps.append(jnp.dot(e.astype(bf16), vs, preferred_element_type=f32) / l)
            a_heads.append(comps[0] - lam * comps[1])
        a = jnp.where(lane < ATT_V_DIM, a_heads[0], a_heads[1])
        a2 = a * a
        ms0 = jnp.sum(jnp.where(lane < ATT_V_DIM, a2, 0.0), axis=-1, keepdims=True)
        ms1 = jnp.sum(jnp.where(lane < ATT_V_DIM, 0.0, a2), axis=-1, keepdims=True)
        ms = jnp.where(lane < ATT_V_DIM, ms0, ms1) * (1.0 / ATT_V_DIM)
        o_ref[:, sl] = a * lax.rsqrt(ms + EPS) * g_ref[...] * (1.0 - lam_init)


def _attention(lam_init, attn_lambda, subln_g, q, k, v, n_seq, q_len, k_len, q_row0, kv_is_3d, n_out, prev_out=None):
    tq = TOK_TILE
    qb = q_len // tq
    q0 = q_row0 // tq
    q_spec = pl.BlockSpec((tq, ATT_WIDTH), lambda b, i: (q0 + b * qb + i, 0))
    if kv_is_3d:
        kv_spec = pl.BlockSpec((None, k_len, ATT_WIDTH), lambda b, i: (b, 0, 0))
    else:
        kv_spec = pl.BlockSpec((k_len, ATT_WIDTH), lambda b, i: (q_row0 // k_len + b, 0))
    in_specs = [
        pl.BlockSpec(attn_lambda.shape, lambda b, i: (0, 0)),
        pl.BlockSpec(subln_g.shape, lambda b, i: (0, 0)),
        q_spec, kv_spec, kv_spec,
    ]
    args = [attn_lambda, subln_g, q, k, v]
    aliases = {}
    if prev_out is not None:
        in_specs.append(pl.BlockSpec(memory_space=pl.ANY))
        args.append(prev_out)
        aliases = {5: 0}
    return pl.pallas_call(
        functools.partial(_attn_kernel, lam_init),
        grid=(n_seq, qb),
        in_specs=in_specs,
        out_specs=q_spec,
        out_shape=jax.ShapeDtypeStruct((n_out, ATT_WIDTH), f32),
        input_output_aliases=aliases,
        compiler_params=_cparams("arbitrary", "arbitrary"),
        name="attention",
    )(*args)


def _halo_specs(geom, width, halo):
    per = TOK_TILE // halo
    n_halo = geom.n_tok // halo
    prev = pl.BlockSpec((halo, width), lambda i: (jnp.maximum(i * per - 1, 0), 0))
    nxt = pl.BlockSpec((halo, width), lambda i: (jnp.minimum((i + 1) * per, n_halo - 1), 0))
    return prev, nxt


def _cconv_kernel(geom, cur_ref, prev_ref, next_ref, w_ref, b_ref, lg_ref, lb_ref, o_ref, pad_ref):
    first, last = geom.seq_pos(pl.program_id(0))

    def glu(ref):
        return ref[:, :CONV_CH] * jax.nn.sigmoid(ref[:, CONV_CH:])

    pad_ref[0:CONV_HALO, :] = jnp.where(first, 0.0, glu(prev_ref))
    pad_ref[CONV_HALO:CONV_HALO + TOK_TILE, :] = glu(cur_ref)
    pad_ref[CONV_HALO + TOK_TILE:, :] = jnp.where(last, 0.0, glu(next_ref))
    rows = 64
    base = CONV_HALO - CONV_K // 2
    for r0 in range(0, TOK_TILE, rows):
        acc = jnp.broadcast_to(b_ref[...], (rows, CONV_CH))
        for k in range(CONV_K):
            acc = acc + w_ref[k:k + 1, :] * pad_ref[r0 + base + k:r0 + base + k + rows, :]
        xc = acc - jnp.mean(acc, axis=-1, keepdims=True)
        var = jnp.mean(xc * xc, axis=-1, keepdims=True)
        o_ref[r0:r0 + rows, :] = _silu(xc * lax.rsqrt(var + EPS) * lg_ref[...] + lb_ref[...])


def _cconv(geom, glu, w, b, ln_g, ln_b):
    prev, nxt = _halo_specs(geom, 2 * CONV_CH, CONV_HALO)
    const = lambda a: pl.BlockSpec(a.shape, lambda i: (0,) * a.ndim)
    return pl.pallas_call(
        functools.partial(_cconv_kernel, geom),
        grid=(geom.n_tiles,),
        in_specs=[pl.BlockSpec((TOK_TILE, 2 * CONV_CH), lambda i: (i, 0)), prev, nxt,
                  const(w), const(b), const(ln_g), const(ln_b)],
        out_specs=pl.BlockSpec((TOK_TILE, CONV_CH), lambda i: (i, 0)),
        out_shape=jax.ShapeDtypeStruct((geom.n_tok, CONV_CH), f32),
        scratch_shapes=[pltpu.VMEM((TOK_TILE + 2 * CONV_HALO, CONV_CH), f32)],
        compiler_params=_cparams("arbitrary"),
        name="conformer_conv",
    )(glu, glu, glu, w, b, ln_g, ln_b)


def _softplus(x):
    return jnp.maximum(x, 0.0) + jnp.log1p(jnp.exp(-jnp.abs(x)))


def _ssd_kernel(geom, direction, cur_ref, prev_ref, next_ref, dt_ref, init_ref, cw_ref, cb_ref, dtb_ref,
                alog_ref, dskip_ref, y_ref, fin_ref, pad_ref, xc_ref, st_ref):
    fwd = direction == 0
    i = pl.program_id(0)
    t = i if fwd else geom.n_tiles - 1 - i
    first, last = geom.seq_pos(t)
    enter, leave = (first, last) if fwd else (last, first)
    is_lat = t >= geom.ctx_tiles

    @pl.when(enter)
    def _():
        st_ref[...] = jnp.where(is_lat, init_ref[...], 0.0)

    pad_ref[0:SSM_HALO, :] = jnp.where(first, 0.0, prev_ref[...])
    pad_ref[SSM_HALO:SSM_HALO + TOK_TILE, :] = cur_ref[...]
    pad_ref[SSM_HALO + TOK_TILE:, :] = jnp.where(last, 0.0, next_ref[...])
    base = SSM_HALO - SSM_CONV_K // 2
    acc = jnp.broadcast_to(cb_ref[...], (TOK_TILE, SSM_XBC))
    for k in range(SSM_CONV_K):
        acc = acc + cw_ref[k:k + 1, :] * pad_ref[base + k:base + k + TOK_TILE, :]
    xc_ref[...] = _silu(acc)

    L = SSD_CHUNK
    a_row = -jnp.exp(alog_ref[...])
    ri = lax.broadcasted_iota(jnp.int32, (L, L), 0)
    ci = lax.broadcasted_iota(jnp.int32, (L, L), 1)
    keep = (ci <= ri) if fwd else (ci >= ri)
    tri_col = keep.astype(f32)
    tri_row = ((ri <= ci) if fwd else (ri >= ci)).astype(f32)
    lane = lax.broadcasted_iota(jnp.int32, (1, LANES), 1)
    er = lax.broadcasted_iota(jnp.int32, (LANES, SSM_WIDTH), 0)
    ec = lax.broadcasted_iota(jnp.int32, (LANES, SSM_WIDTH), 1)
    expand = (er == direction * SSM_HEADS + ec // SSM_HEAD_DIM).astype(f32)
    heads_per_group = SSM_HEADS // SSM_GROUPS
    heads_per_slab = LANES // SSM_HEAD_DIM
    tot_at = L - 1 if fwd else 0

    chunks = range(TOK_TILE // L)
    for cidx in (chunks if fwd else reversed(chunks)):
        r0 = cidx * L
        xs = xc_ref[r0:r0 + L, 0:SSM_WIDTH]
        bm = xc_ref[r0:r0 + L, SSM_WIDTH:SSM_WIDTH + LANES]
        cm = xc_ref[r0:r0 + L, SSM_WIDTH + LANES:SSM_XBC]
        dt = _softplus(dt_ref[r0:r0 + L, :] + dtb_ref[...])
        da = dt * a_row
        acum_col = jnp.dot(tri_col, da, precision=HIGHEST, preferred_element_type=f32)
        acum_row = jnp.dot(da.T, tri_row, precision=HIGHEST, preferred_element_type=f32)
        tot_row = acum_col[tot_at:tot_at + 1, :]
        tot_col = acum_row[:, tot_at:tot_at + 1]
        decay_out = jnp.exp(tot_col - acum_row)
        decay_in = jnp.exp(acum_col)
        chunk_decay = jnp.exp(tot_row)
        xdt = (xs * jnp.dot(dt, expand, precision=HIGHEST, preferred_element_type=f32)).astype(bf16)
        bt = bm.T
        bb = bm.astype(bf16)
        ys = []
        for g in range(SSM_GROUPS):
            in_group = (lane >= g * SSM_STATE) & (lane < (g + 1) * SSM_STATE)
            cg = jnp.where(in_group, cm, 0.0)
            gmat = lax.dot_general(cg.astype(bf16), bb, (((1,), (1,)), ((), ())), preferred_element_type=f32)
            for hh in range(heads_per_group):
                h = g * heads_per_group + hh
                hl = direction * SSM_HEADS + h
                slab = xdt[:, (h // heads_per_slab) * LANES:(h // heads_per_slab + 1) * LANES]
                seg = acum_col[:, hl:hl + 1] - acum_row[hl:hl + 1, :]
                m = (gmat * jnp.exp(jnp.where(keep, seg, -jnp.inf))).astype(bf16)
                y_h = jnp.dot(m, slab, preferred_element_type=f32)
                c_in = (cg * decay_in[:, hl:hl + 1]).astype(bf16)
                state = st_ref[h]
                y_h = y_h + jnp.dot(c_in, state.astype(bf16), preferred_element_type=f32)
                b_out = (bt * decay_out[hl:hl + 1, :]).astype(bf16)
                st_ref[h] = chunk_decay[:, hl:hl + 1] * state + jnp.dot(b_out, slab, preferred_element_type=f32)
                ys.append(y_h)
        for j in range(SSM_WIDTH // LANES):
            y_slab = jnp.where(lane < SSM_HEAD_DIM, ys[heads_per_slab * j], ys[heads_per_slab * j + 1])
            if fwd:
                y_slab = y_slab + dskip_ref[:, j * LANES:(j + 1) * LANES] * xs[:, j * LANES:(j + 1) * LANES]
            y_ref[r0:r0 + L, j * LANES:(j + 1) * LANES] = y_slab

    @pl.when(leave)
    def _():
        fin_ref[...] = st_ref[...]


def _ssd(geom, direction, xbc, dt, init, conv_w, conv_b, dt_bias, a_log, d_skip):
    nt = geom.n_tiles
    tidx = (lambda i: i) if direction == 0 else (lambda i: nt - 1 - i)
    per = TOK_TILE // SSM_HALO
    n_halo = geom.n_tok // SSM_HALO
    n_seq = geom.n_ctx_seq + geom.n_lat_seq
    const = lambda a: pl.BlockSpec(a.shape, lambda i: (0,) * a.ndim)
    st_block = (None, SSM_HEADS, LANES, LANES)
    y, fin = pl.pallas_call(
        functools.partial(_ssd_kernel, geom, direction),
        grid=(nt,),
        in_specs=[
            pl.BlockSpec((TOK_TILE, SSM_XBC), lambda i: (tidx(i), 0)),
            pl.BlockSpec((SSM_HALO, SSM_XBC), lambda i: (jnp.maximum(tidx(i) * per - 1, 0), 0)),
            pl.BlockSpec((SSM_HALO, SSM_XBC), lambda i: (jnp.minimum((tidx(i) + 1) * per, n_halo - 1), 0)),
            pl.BlockSpec((TOK_TILE, LANES), lambda i: (tidx(i), 0)),
            pl.BlockSpec(st_block, lambda i: (jnp.maximum(geom.seq_index(tidx(i)) - geom.n_ctx_seq, 0), 0, 0, 0)),
            const(conv_w), const(conv_b), const(dt_bias), const(a_log), const(d_skip),
        ],
        out_specs=[
            pl.BlockSpec((TOK_TILE, SSM_WIDTH), lambda i: (tidx(i), 0)),
            pl.BlockSpec(st_block, lambda i: (geom.seq_index(tidx(i)), 0, 0, 0)),
        ],
        out_shape=[
            jax.ShapeDtypeStruct((geom.n_tok, SSM_WIDTH), f32),
            jax.ShapeDtypeStruct((n_seq, SSM_HEADS, LANES, LANES), f32),
        ],
        scratch_shapes=[
            pltpu.VMEM((TOK_TILE + 2 * SSM_HALO, SSM_XBC), f32),
            pltpu.VMEM((TOK_TILE, SSM_XBC), f32),
            pltpu.VMEM((SSM_HEADS, LANES, LANES), f32),
        ],
        compiler_params=_cparams("arbitrary"),
        name="ssd_scan",
    )(xbc, xbc, xbc, dt, init, conv_w, conv_b, dt_bias, a_log, d_skip)
    return y, fin


def _pad_states(s):
    b = s.shape[0]
    st = jnp.swapaxes(s, -1, -2)
    out = jnp.zeros((b, SSM_HEADS, LANES, LANES), f32)
    hpg = SSM_HEADS // SSM_GROUPS
    for h in range(SSM_HEADS):
        g, side = h // hpg, h % (LANES // SSM_HEAD_DIM)
        out = out.at[:, h, g * SSM_STATE:(g + 1) * SSM_STATE,
                     side * SSM_HEAD_DIM:(side + 1) * SSM_HEAD_DIM].set(st[:, h])
    return out


def _unpad_states(fin):
    hpg = SSM_HEADS // SSM_GROUPS
    outs = []
    for h in range(SSM_HEADS):
        g, side = h // hpg, h % (LANES // SSM_HEAD_DIM)
        q = fin[:, h, g * SSM_STATE:(g + 1) * SSM_STATE, side * SSM_HEAD_DIM:(side + 1) * SSM_HEAD_DIM]
        outs.append(jnp.swapaxes(q, -1, -2))
    return jnp.stack(outs, axis=1)


def _out_proj_kernel(attn_ref, conv_ref, yf_ref, yb_ref, z_ref, x_ref, mod_ref, sg_ref, w_ref, g2_ref,
                     rw_ref, rb_ref, xo_ref, h2_ref, ti_ref, gw_ref):
    gated = (yf_ref[...] + yb_ref[...]) * _silu(z_ref[...])
    ssm = gated * lax.rsqrt(jnp.mean(gated * gated, axis=-1, keepdims=True) + EPS) * sg_ref[...]
    o1 = ATT_WIDTH
    o2 = ATT_WIDTH + CONV_CH
    mix = (jnp.dot(attn_ref[...].astype(bf16), w_ref[0:o1, :], preferred_element_type=f32)
           + jnp.dot(conv_ref[...].astype(bf16), w_ref[o1:o2, :], preferred_element_type=f32)
           + jnp.dot(ssm.astype(bf16), w_ref[o2:, :], preferred_element_type=f32))
    x = x_ref[...] + mod_ref[2:3, :] * mix
    xo_ref[...] = x
    y = x * lax.rsqrt(jnp.mean(x * x, axis=-1, keepdims=True) + EPS) * g2_ref[...]
    h2 = y * (1.0 + mod_ref[4:5, :]) + mod_ref[3:4, :]
    h2_ref[...] = h2.astype(bf16)
    logits = jnp.dot(h2, rw_ref[...], precision=HIGHEST, preferred_element_type=f32) + rb_ref[...]
    lane = lax.broadcasted_iota(jnp.int32, (1, LANES), 1)
    lane_f = lane.astype(f32)
    lg = jnp.where(lane < N_EXPERTS, logits, -jnp.inf)
    idx_out = jnp.zeros(logits.shape, jnp.int32)
    gate_out = jnp.zeros(logits.shape, f32)
    top0 = None
    denom = None
    for k in range(TOP_K):
        m = jnp.max(lg, axis=-1, keepdims=True)
        idx = jnp.min(jnp.where(lg == m, lane_f, float(LANES)), axis=-1, keepdims=True).astype(jnp.int32)
        lg = jnp.where(lane == idx, -jnp.inf, lg)
        if k == 0:
            top0 = m
        e = jnp.exp(m - top0)
        denom = e if k == 0 else denom + e
        idx_out = jnp.where(lane == k, idx, idx_out)
        gate_out = jnp.where(lane == k, e, gate_out)
    ti_ref[...] = idx_out
    gw_ref[...] = gate_out / denom


def _out_proj(geom, layer, attn, conv, y_f, y_b, z, x, mod, ssm_norm_g, w_out, norm2_g, router_w, router_b):
    n = geom.n_tok
    tile = lambda w: pl.BlockSpec((TOK_TILE, w), lambda i: (i, 0))
    const = lambda a: pl.BlockSpec(a.shape, lambda i: (0,) * a.ndim)
    return pl.pallas_call(
        _out_proj_kernel,
        grid=(geom.n_tiles,),
        in_specs=[
            tile(ATT_WIDTH), tile(CONV_CH), tile(SSM_WIDTH), tile(SSM_WIDTH), tile(SSM_WIDTH), tile(D_MODEL),
            pl.BlockSpec((None, None, 6, D_MODEL), lambda i: (layer, geom.cond_row(i), 0, 0)),
            const(ssm_norm_g), const(w_out), const(norm2_g), const(router_w), const(router_b),
        ],
        out_specs=[tile(D_MODEL), tile(D_MODEL), tile(LANES), tile(LANES)],
        out_shape=[jax.ShapeDtypeStruct((n, D_MODEL), f32), jax.ShapeDtypeStruct((n, D_MODEL), bf16),
                   jax.ShapeDtypeStruct((n, LANES), jnp.int32), jax.ShapeDtypeStruct((n, LANES), f32)],
        compiler_params=_cparams("arbitrary"),
        name="out_proj",
    )(attn, conv, y_f, y_b, z, x, mod, ssm_norm_g, w_out, norm2_g, router_w, router_b)


def _moe_kernel(be_ref, nused_ref, x_ref, wg_ref, bg_ref, wu_ref, bu_ref, wd_ref, bd_ref, o_ref,
                wg_s, wu_s, wd_s):
    i = pl.program_id(0)
    prev_e = be_ref[jnp.maximum(i - 1, 0)]
    new_expert = (i == 0) | (be_ref[i] != prev_e)

    @pl.when(new_expert)
    def _():
        wg_s[...] = wg_ref[0].astype(bf16)
        wu_s[...] = wu_ref[0].astype(bf16)
        wd_s[...] = wd_ref[0].astype(bf16)

    @pl.when(i < nused_ref[0])
    def _():
        x = x_ref[...]
        g = jnp.dot(x, wg_s[...], preferred_element_type=f32) + bg_ref[0]
        u = jnp.dot(x, wu_s[...], preferred_element_type=f32) + bu_ref[0]
        g = jnp.minimum(g, SWIGLU_LIMIT)
        u = jnp.clip(u, -SWIGLU_LIMIT, SWIGLU_LIMIT)
        act = g * jax.nn.sigmoid(SWIGLU_ALPHA * g) * (u + 1.0)
        o_ref[...] = jnp.dot(act.astype(bf16), wd_s[...], preferred_element_type=f32) + bd_ref[0]

    @pl.when(i >= nused_ref[0])
    def _():
        o_ref[...] = jnp.zeros_like(o_ref)


def _moe(layer, xs, block_e, n_used, w_gate, b_gate, w_up, b_up, w_down, b_down):
    n_rows, d = xs.shape
    n_blocks = n_rows // MOE_TILE
    d_ff = w_gate.shape[-1]
    w_spec = lambda r, c: pl.BlockSpec((None, 1, r, c), lambda i, be, nu: (layer, be[i], 0, 0))
    b_spec = lambda c: pl.BlockSpec((None, 1, 1, c), lambda i, be, nu: (layer, be[i], 0, 0))
    row_spec = pl.BlockSpec((MOE_TILE, d), lambda i, be, nu: (i, 0))
    depth, n_e = b_gate.shape[:2]
    return pl.pallas_call(
        _moe_kernel,
        grid_spec=pltpu.PrefetchScalarGridSpec(
            num_scalar_prefetch=2,
            grid=(n_blocks,),
            in_specs=[row_spec, w_spec(d, d_ff), b_spec(d_ff), w_spec(d, d_ff), b_spec(d_ff),
                      w_spec(d_ff, d), b_spec(d)],
            out_specs=row_spec,
            scratch_shapes=[pltpu.VMEM((d, d_ff), bf16), pltpu.VMEM((d, d_ff), bf16), pltpu.VMEM((d_ff, d), bf16)],
        ),
        out_shape=jax.ShapeDtypeStruct((n_rows, d), f32),
        compiler_params=_cparams("arbitrary"),
        name="moe_experts",
    )(block_e, n_used, xs, w_gate, b_gate.reshape(depth, n_e, 1, d_ff), w_up, b_up.reshape(depth, n_e, 1, d_ff),
      w_down, b_down.reshape(depth, n_e, 1, d))


def _route(top_idx, n_blocks):
    n = top_idx.shape[0]
    n_assign = n * TOP_K
    i32 = jnp.int32
    flat_e = top_idx.reshape(-1)
    iota = jnp.arange(n_assign, dtype=i32)
    sorted_e, order = lax.sort((flat_e, iota), num_keys=1, is_stable=True)
    counts = jnp.sum((flat_e[:, None] == jnp.arange(N_EXPERTS, dtype=i32)[None, :]).astype(i32), axis=0)
    grp_start = jnp.cumsum(counts) - counts
    nb = (counts + MOE_TILE - 1) // MOE_TILE
    blk_end = jnp.cumsum(nb)
    blk_start = blk_end - nb
    dest_sorted = blk_start[sorted_e] * MOE_TILE + (iota - grp_start[sorted_e])
    _, dest = lax.sort((order, dest_sorted), num_keys=1)
    block_e = jnp.minimum(jnp.searchsorted(blk_end, jnp.arange(n_blocks, dtype=i32), side='right'),
                          N_EXPERTS - 1).astype(i32)
    n_used = blk_end[-1:].astype(i32)
    row = jnp.arange(n_blocks * MOE_TILE, dtype=i32)
    row_e = block_e[row // MOE_TILE]
    in_grp = row - blk_start[row_e] * MOE_TILE
    src = order[jnp.clip(grp_start[row_e] + in_grp, 0, n_assign - 1)]
    row_tok = jnp.where(in_grp < counts[row_e], src // TOP_K, 0)
    return dest.reshape(n, TOP_K), block_e, n_used, row_tok


def _combine_kernel(final, x_ref, rows_ref, gw_ref, mod_ref, *rest):
    ffn = rows_ref[0] * gw_ref[:, 0:1]
    for k in range(1, TOP_K):
        ffn = ffn + rows_ref[k] * gw_ref[:, k:k + 1]
    x = x_ref[...] + mod_ref[5:6, :] * ffn
    if final:
        fg_ref, xo_ref, y_ref = rest
        xo_ref[...] = x
        y_ref[...] = x * lax.rsqrt(jnp.mean(x * x, axis=-1, keepdims=True) + EPS) * fg_ref[...]
    else:
        rest[0][...] = x


def _combine(geom, layer, x, rows, gate_w, mod, final_g):
    n = geom.n_tok
    final = final_g is not None
    tile = pl.BlockSpec((TOK_TILE, D_MODEL), lambda i: (i, 0))
    in_specs = [
        tile,
        pl.BlockSpec((TOP_K, TOK_TILE, D_MODEL), lambda i: (0, i, 0)),
        pl.BlockSpec((TOK_TILE, LANES), lambda i: (i, 0)),
        pl.BlockSpec((None, None, 6, D_MODEL), lambda i: (layer, geom.cond_row(i), 0, 0)),
    ]
    args = [x, rows, gate_w, mod]
    out_specs, out_shape = [tile], [jax.ShapeDtypeStruct((n, D_MODEL), f32)]
    if final:
        in_specs.append(pl.BlockSpec(final_g.shape, lambda i: (0, 0)))
        args.append(final_g)
        out_specs.append(tile)
        out_shape.append(jax.ShapeDtypeStruct((n, D_MODEL), f32))
    return pl.pallas_call(
        functools.partial(_combine_kernel, final),
        grid=(geom.n_tiles,),
        in_specs=in_specs,
        out_specs=out_specs,
        out_shape=out_shape,
        compiler_params=_cparams("arbitrary"),
        name="moe_combine",
    )(*args)


def _pad_lanes(a, width=LANES):
    return jnp.pad(a, [(0, 0)] * (a.ndim - 1) + [(0, width - a.shape[-1])])


def kernel(x_prompt, x_sample, c, cache_k, cache_v, state_ssm, c_ctx, norm1_g, norm2_g, w_ada, b_ada, w_in, w_out, attn_lambda, attn_subln_g, conv_dw_w, conv_dw_b, conv_ln_g, conv_ln_b, ssm_conv_w, ssm_conv_b, ssm_dt_bias, ssm_a_log, ssm_d, ssm_norm_g, router_w, router_b, w_gate, b_gate, w_up, b_up, w_down, b_down, final_g):
    n_ctx_seq, ctx_len, d = x_prompt.shape
    n_lat_seq, lat_len, _ = x_sample.shape
    past_len = cache_k.shape[-2]
    geom = _Geom(n_ctx_seq, ctx_len, n_lat_seq, lat_len)
    n = geom.n_tok

    x = jnp.concatenate([x_prompt.reshape(geom.n_ctx, d), x_sample.reshape(geom.n_lat, d)], axis=0)
    cond = jnp.concatenate([c_ctx[None, :], c, jnp.zeros((COND_ROWS - 1 - n_lat_seq, d), f32)], axis=0)
    mod = _ada(cond, w_ada, b_ada).reshape(DEPTH, COND_ROWS, 6, d)
    tables = _rope_tables(geom)

    n_assign = n * TOP_K
    n_blocks = -(-n_assign // MOE_TILE) + N_EXPERTS

    new_k, new_v, new_s = [], [], []
    y_final = None
    for l in range(DEPTH):
        w_main = w_in[l, :, :N_MAIN].astype(bf16)
        w_dt = _pad_lanes(w_in[l, :, N_MAIN:]).astype(bf16)
        q, k, v, glu, z, xbc, dt = _in_proj(geom, l, x, mod, norm1_g[l][None, :], w_main, w_dt, tables)

        lam_init = 0.8 - 0.6 * math.exp(-0.3 * l)
        subln = jnp.tile(attn_subln_g[l], LANES // ATT_V_DIM)[None, :]
        k_cache = cache_k[:, l].transpose(0, 3, 1, 2, 4).reshape(n_lat_seq, past_len, QK_WIDTH)
        v_cache = cache_v[:, l].transpose(0, 2, 1, 3).reshape(n_lat_seq, past_len, ATT_WIDTH)
        k_all = jnp.concatenate([k[geom.n_ctx:].reshape(n_lat_seq, lat_len, QK_WIDTH), k_cache], axis=1).astype(bf16)
        v_all = jnp.concatenate([v[geom.n_ctx:].reshape(n_lat_seq, lat_len, ATT_WIDTH), v_cache], axis=1).astype(bf16)
        attn = _attention(lam_init, attn_lambda[l], subln, q, k, v, n_ctx_seq, ctx_len, ctx_len, 0, False, n)
        attn = _attention(lam_init, attn_lambda[l], subln, q, k_all, v_all, n_lat_seq, lat_len,
                          lat_len + past_len, geom.n_ctx, True, n, prev_out=attn)

        conv = _cconv(geom, glu, conv_dw_w[l], conv_dw_b[l][None, :], conv_ln_g[l][None, :], conv_ln_b[l][None, :])

        dt_bias = _pad_lanes(ssm_dt_bias[l].reshape(1, 2 * SSM_HEADS))
        a_log = _pad_lanes(ssm_a_log[l].reshape(1, 2 * SSM_HEADS))
        d_skip = jnp.repeat(ssm_d[l], SSM_HEAD_DIM)[None, :]
        ys, fins = [], []
        for direction in range(2):
            init = _pad_states(state_ssm[:, l, direction])
            y_d, fin = _ssd(geom, direction, xbc, dt, init, ssm_conv_w[l], ssm_conv_b[l][None, :], dt_bias, a_log, d_skip)
            ys.append(y_d)
            fins.append(_unpad_states(fin[:n_ctx_seq]))

        x, h2, top_idx, gate_w = _out_proj(geom, l, attn, conv, ys[0], ys[1], z, x, mod, ssm_norm_g[l][None, :],
                                  w_out[l].astype(bf16), norm2_g[l][None, :],
                                  _pad_lanes(router_w[l]), _pad_lanes(router_b[l][None, :]))

        dest, block_e, n_used, row_tok = _route(top_idx[:, :TOP_K], n_blocks)
        xs = jnp.take(h2, row_tok, axis=0)
        eo = _moe(l, xs, block_e, n_used, w_gate, b_gate, w_up, b_up, w_down, b_down)
        rows = jnp.take(eo, dest.T.reshape(-1), axis=0).reshape(TOP_K, n, d)
        if l == DEPTH - 1:
            x, y_final = _combine(geom, l, x, rows, gate_w, mod, final_g[None, :])
        else:
            x, = _combine(geom, l, x, rows, gate_w, mod, None)

        new_k.append(k[:geom.n_ctx].reshape(n_ctx_seq, ctx_len, ATT_HEADS, 2, ATT_QK_DIM).transpose(0, 2, 3, 1, 4))
        new_v.append(v[:geom.n_ctx].reshape(n_ctx_seq, ctx_len, ATT_HEADS, ATT_V_DIM).transpose(0, 2, 1, 3))
        new_s.append(jnp.stack(fins, axis=1))

    y_prompt = y_final[:geom.n_ctx].reshape(n_ctx_seq, ctx_len, d)
    y_sample = y_final[geom.n_ctx:].reshape(n_lat_seq, lat_len, d)
    return (y_prompt, y_sample, jnp.stack(new_k, axis=1), jnp.stack(new_v, axis=1), jnp.stack(new_s, axis=1))
```

```python
import functools
import math

import numpy as np
import jax
import jax.numpy as jnp
from jax import lax
from jax.experimental import pallas as pl
from jax.experimental.pallas import tpu as pltpu

f32 = jnp.float32
bf16 = jnp.bfloat16
HIGHEST = lax.Precision.HIGHEST

D_MODEL = 1024
DEPTH = 2
GRID_W = 64
ATT_HEADS = 4
ATT_QK_DIM = 32
ATT_V_DIM = 64
ATT_WIDTH = 256
QK_WIDTH = 256
ROPE_AXIS_DIM = 16
ROPE_BASE = 10000.0
CONV_CH = 256
CONV_K = 31
SSM_WIDTH = 512
SSM_HEAD_DIM = 64
SSM_HEADS = 8
SSM_GROUPS = 2
SSM_STATE = 64
SSM_CONV_K = 5
SSM_XBC = 768
N_MAIN = 2560
N_EXPERTS = 32
TOP_K = 4
SWIGLU_LIMIT = 7.0
SWIGLU_ALPHA = 1.702
EPS = 1e-6

LANES = 128
SUBLANES = 8
TOK_TILE = 256
SSD_CHUNK = 128
CONV_HALO = 16
SSM_HALO = 8
MOE_TILE = 512
COND_ROWS = 16
VMEM_LIMIT = 56 * 1024 * 1024


def _cparams(*sem):
    return pltpu.CompilerParams(dimension_semantics=sem, vmem_limit_bytes=VMEM_LIMIT)


def _silu(x):
    return x * jax.nn.sigmoid(x)


def _ada_kernel(cond_ref, w_ref, b_ref, o_ref):
    s = _silu(cond_ref[...]).astype(bf16)
    o_ref[0] = jnp.dot(s, w_ref[0].astype(bf16), preferred_element_type=f32) + b_ref[0]


def _ada(cond, w_ada, b_ada):
    depth, d, n6 = w_ada.shape
    tn = 1536
    return pl.pallas_call(
        _ada_kernel,
        grid=(depth, n6 // tn),
        in_specs=[
            pl.BlockSpec((COND_ROWS, d), lambda l, j: (0, 0)),
            pl.BlockSpec((1, d, tn), lambda l, j: (l, 0, j)),
            pl.BlockSpec((1, 1, tn), lambda l, j: (l, 0, j)),
        ],
        out_specs=pl.BlockSpec((1, COND_ROWS, tn), lambda l, j: (l, 0, j)),
        out_shape=jax.ShapeDtypeStruct((depth, COND_ROWS, n6), f32),
        compiler_params=_cparams("arbitrary", "arbitrary"),
        name="ada",
    )(cond, w_ada, b_ada.reshape(depth, 1, n6))


class _Geom:
    def __init__(self, n_ctx_seq, ctx_len, n_lat_seq, lat_len):
        self.n_ctx_seq, self.ctx_len, self.n_lat_seq, self.lat_len = n_ctx_seq, ctx_len, n_lat_seq, lat_len
        self.n_ctx = n_ctx_seq * ctx_len
        self.n_lat = n_lat_seq * lat_len
        self.n_tok = self.n_ctx + self.n_lat
        assert ctx_len % TOK_TILE == 0 and lat_len % TOK_TILE == 0
        self.ctx_tiles = self.n_ctx // TOK_TILE
        self.tiles_per_ctx = ctx_len // TOK_TILE
        self.tiles_per_lat = lat_len // TOK_TILE
        self.n_tiles = self.n_tok // TOK_TILE

    def cond_row(self, i):
        return jnp.where(i < self.ctx_tiles, 0, 1 + (i - self.ctx_tiles) // self.tiles_per_lat)

    def seq_pos(self, i):
        j_ctx = i % self.tiles_per_ctx
        j_lat = (i - self.ctx_tiles) % self.tiles_per_lat
        is_ctx = i < self.ctx_tiles
        first = jnp.where(is_ctx, j_ctx == 0, j_lat == 0)
        last = jnp.where(is_ctx, j_ctx == self.tiles_per_ctx - 1, j_lat == self.tiles_per_lat - 1)
        return first, last

    def seq_index(self, i):
        return jnp.where(i < self.ctx_tiles, i // self.tiles_per_ctx,
                         self.n_ctx_seq + (i - self.ctx_tiles) // self.tiles_per_lat)


def _rope_tables(geom):
    t = np.arange(geom.lat_len)
    row = (t // GRID_W).astype(np.float64)
    col = (t % GRID_W).astype(np.float64)
    inv_freq = 1.0 / (ROPE_BASE ** (np.arange(0, ROPE_AXIS_DIM, 2, dtype=np.float64) / ROPE_AXIS_DIM))
    ang_r = row[:, None] * inv_freq[None, :]
    ang_c = col[:, None] * inv_freq[None, :]
    ang = np.concatenate([ang_r, ang_r, ang_c, ang_c], axis=-1)
    cos, sin = np.cos(ang), np.sin(ang)
    half = ROPE_AXIS_DIM // 2
    lane = np.arange(ATT_QK_DIM)
    first_half = (lane % ROPE_AXIS_DIM) < half
    sin_a = np.where(first_half[None, :], -sin, 0.0)
    sin_b = np.where(first_half[None, :], 0.0, sin)
    rep = LANES // ATT_QK_DIM

    def full(tab, ident):
        tab = np.tile(tab, (1, rep))
        return jnp.asarray(np.concatenate([np.full((TOK_TILE, LANES), ident), tab], axis=0), f32)

    return full(cos, 1.0), full(sin_a, 0.0), full(sin_b, 0.0)


def _in_proj_kernel(x_ref, mod_ref, g_ref, w_ref, wdt_ref, cos_ref, sa_ref, sb_ref,
                    q_ref, k_ref, v_ref, glu_ref, z_ref, xbc_ref, dt_ref):
    x = x_ref[...]
    y = x * lax.rsqrt(jnp.mean(x * x, axis=-1, keepdims=True) + EPS) * g_ref[...]
    h = (y * (1.0 + mod_ref[1:2, :]) + mod_ref[0:1, :]).astype(bf16)
    p = jnp.dot(h, w_ref[...], preferred_element_type=f32)
    dt_ref[...] = jnp.dot(h, wdt_ref[...], preferred_element_type=f32)
    cos, sa, sb = cos_ref[...], sa_ref[...], sb_ref[...]
    half = ROPE_AXIS_DIM // 2

    def rope(u):
        return u * cos + pltpu.roll(u, LANES - half, 1) * sa + pltpu.roll(u, half, 1) * sb

    scale = ATT_QK_DIM ** -0.5
    for j in range(QK_WIDTH // LANES):
        lo = j * LANES
        q_ref[:, lo:lo + LANES] = (rope(p[:, lo:lo + LANES]) * scale).astype(bf16)
        k_ref[:, lo:lo + LANES] = rope(p[:, QK_WIDTH + lo:QK_WIDTH + lo + LANES])
    o = 2 * QK_WIDTH
    v_ref[...] = p[:, o:o + ATT_WIDTH]
    o += ATT_WIDTH
    glu_ref[...] = p[:, o:o + 2 * CONV_CH]
    o += 2 * CONV_CH
    z_ref[...] = p[:, o:o + SSM_WIDTH]
    o += SSM_WIDTH
    xbc_ref[...] = p[:, o:o + SSM_XBC]


def _in_proj(geom, layer, x, mod, norm_g, w_main, w_dt, tables):
    n = geom.n_tok
    tile = lambda w: pl.BlockSpec((TOK_TILE, w), lambda i: (i, 0))
    const = lambda a: pl.BlockSpec(a.shape, lambda i: (0,) * a.ndim)
    tab_spec = pl.BlockSpec(
        (TOK_TILE, LANES),
        lambda i: (jnp.where(i < geom.ctx_tiles, 0, 1 + (i - geom.ctx_tiles) % geom.tiles_per_lat), 0))
    widths = (QK_WIDTH, QK_WIDTH, ATT_WIDTH, 2 * CONV_CH, SSM_WIDTH, SSM_XBC, LANES)
    dtypes = (bf16, f32, f32, f32, f32, f32, f32)
    return pl.pallas_call(
        _in_proj_kernel,
        grid=(geom.n_tiles,),
        in_specs=[
            tile(D_MODEL),
            pl.BlockSpec((None, None, 6, D_MODEL), lambda i: (layer, geom.cond_row(i), 0, 0)),
            const(norm_g), const(w_main), const(w_dt), tab_spec, tab_spec, tab_spec,
        ],
        out_specs=[tile(w) for w in widths],
        out_shape=[jax.ShapeDtypeStruct((n, w), dt) for w, dt in zip(widths, dtypes)],
        compiler_params=_cparams("arbitrary"),
        name="in_proj",
    )(x, mod, norm_g, w_main, w_dt, *tables)


def _attn_kernel(lam_init, al_ref, g_ref, q_ref, k_ref, v_ref, o_ref):
    al = al_ref[...]
    lam = (jnp.exp(jnp.sum(al[0:1] * al[1:2], axis=-1, keepdims=True))
           - jnp.exp(jnp.sum(al[2:3] * al[3:4], axis=-1, keepdims=True)) + lam_init)
    lane = lax.broadcasted_iota(jnp.int32, (1, LANES), 1)
    heads_per_slab = LANES // ATT_V_DIM
    for j in range(ATT_WIDTH // LANES):
        sl = slice(j * LANES, (j + 1) * LANES)
        qs = q_ref[:, sl]
        ks = k_ref[:, sl].astype(bf16)
        vs = v_ref[:, sl].astype(bf16)
        a_heads = []
        for hl in range(heads_per_slab):
            comps = []
            for c in range(2):
                lo = hl * ATT_V_DIM + c * ATT_QK_DIM
                qm = jnp.where((lane >= lo) & (lane < lo + ATT_QK_DIM), qs, jnp.zeros_like(qs))
                s = lax.dot_general(qm, ks, (((1,), (1,)), ((), ())), preferred_element_type=f32)
                e = jnp.exp(s - jnp.max(s, axis=-1, keepdims=True))
                l = jnp.sum(e, axis=-1, keepdims=True)
                comps.append(jnp.dot(e.astype(bf16), vs, preferred_element_type=f32) / l)
            a_heads.append(comps[0] - lam * comps[1])
        a = jnp.where(lane < ATT_V_DIM, a_heads[0], a_heads[1])
        a2 = a * a
        ms0 = jnp.sum(jnp.where(lane < ATT_V_DIM, a2, 0.0), axis=-1, keepdims=True)
        ms1 = jnp.sum(jnp.where(lane < ATT_V_DIM, 0.0, a2), axis=-1, keepdims=True)
        ms = jnp.where(lane < ATT_V_DIM, ms0, ms1) * (1.0 / ATT_V_DIM)
        o_ref[:, sl] = a * lax.rsqrt(ms + EPS) * g_ref[...] * (1.0 - lam_init)


def _attention(lam_init, attn_lambda, subln_g, q, k, v, n_seq, q_len, k_len, q_row0, kv_is_3d):
    tq = TOK_TILE
    qb = q_len // tq
    q0 = q_row0 // tq
    if kv_is_3d:
        kv_spec = pl.BlockSpec((None, k_len, ATT_WIDTH), lambda b, i: (b, 0, 0))
    else:
        kv_spec = pl.BlockSpec((k_len, ATT_WIDTH), lambda b, i: (q_row0 // k_len + b, 0))
    return pl.pallas_call(
        functools.partial(_attn_kernel, lam_init),
        grid=(n_seq, qb),
        in_specs=[
            pl.BlockSpec(attn_lambda.shape, lambda b, i: (0, 0)),
            pl.BlockSpec(subln_g.shape, lambda b, i: (0, 0)),
            pl.BlockSpec((tq, ATT_WIDTH), lambda b, i: (q0 + b * qb + i, 0)),
            kv_spec, kv_spec,
        ],
        out_specs=pl.BlockSpec((tq, ATT_WIDTH), lambda b, i: (b * qb + i, 0)),
        out_shape=jax.ShapeDtypeStruct((n_seq * q_len, ATT_WIDTH), f32),
        compiler_params=_cparams("arbitrary", "arbitrary"),
        name="attention",
    )(attn_lambda, subln_g, q, k, v)


def _halo_specs(geom, width, halo):
    per = TOK_TILE // halo
    n_halo = geom.n_tok // halo
    prev = pl.BlockSpec((halo, width), lambda i: (jnp.maximum(i * per - 1, 0), 0))
    nxt = pl.BlockSpec((halo, width), lambda i: (jnp.minimum((i + 1) * per, n_halo - 1), 0))
    return prev, nxt


def _cconv_kernel(geom, cur_ref, prev_ref, next_ref, w_ref, b_ref, lg_ref, lb_ref, o_ref, pad_ref):
    first, last = geom.seq_pos(pl.program_id(0))

    def glu(ref):
        return ref[:, :CONV_CH] * jax.nn.sigmoid(ref[:, CONV_CH:])

    pad_ref[0:CONV_HALO, :] = jnp.where(first, 0.0, glu(prev_ref))
    pad_ref[CONV_HALO:CONV_HALO + TOK_TILE, :] = glu(cur_ref)
    pad_ref[CONV_HALO + TOK_TILE:, :] = jnp.where(last, 0.0, glu(next_ref))
    rows = 64
    base = CONV_HALO - CONV_K // 2
    for r0 in range(0, TOK_TILE, rows):
        acc = jnp.broadcast_to(b_ref[...], (rows, CONV_CH))
        for k in range(CONV_K):
            acc = acc + w_ref[k:k + 1, :] * pad_ref[r0 + base + k:r0 + base + k + rows, :]
        xc = acc - jnp.mean(acc, axis=-1, keepdims=True)
        var = jnp.mean(xc * xc, axis=-1, keepdims=True)
        o_ref[r0:r0 + rows, :] = _silu(xc * lax.rsqrt(var + EPS) * lg_ref[...] + lb_ref[...])


def _cconv(geom, glu, w, b, ln_g, ln_b):
    prev, nxt = _halo_specs(geom, 2 * CONV_CH, CONV_HALO)
    const = lambda a: pl.BlockSpec(a.shape, lambda i: (0,) * a.ndim)
    return pl.pallas_call(
        functools.partial(_cconv_kernel, geom),
        grid=(geom.n_tiles,),
        in_specs=[pl.BlockSpec((TOK_TILE, 2 * CONV_CH), lambda i: (i, 0)), prev, nxt,
                  const(w), const(b), const(ln_g), const(ln_b)],
        out_specs=pl.BlockSpec((TOK_TILE, CONV_CH), lambda i: (i, 0)),
        out_shape=jax.ShapeDtypeStruct((geom.n_tok, CONV_CH), f32),
        scratch_shapes=[pltpu.VMEM((TOK_TILE + 2 * CONV_HALO, CONV_CH), f32)],
        compiler_params=_cparams("arbitrary"),
        name="conformer_conv",
    )(glu, glu, glu, w, b, ln_g, ln_b)


def _softplus(x):
    return jnp.maximum(x, 0.0) + jnp.log1p(jnp.exp(-jnp.abs(x)))


def _ssd_kernel(geom, direction, cur_ref, prev_ref, next_ref, dt_ref, init_ref, cw_ref, cb_ref, dtb_ref,
                alog_ref, dskip_ref, y_ref, fin_ref, pad_ref, xc_ref, st_ref):
    fwd = direction == 0
    i = pl.program_id(0)
    t = i if fwd else geom.n_tiles - 1 - i
    first, last = geom.seq_pos(t)
    enter, leave = (first, last) if fwd else (last, first)
    is_lat = t >= geom.ctx_tiles

    @pl.when(enter)
    def _():
        st_ref[...] = jnp.where(is_lat, init_ref[...], 0.0)

    pad_ref[0:SSM_HALO, :] = jnp.where(first, 0.0, prev_ref[...])
    pad_ref[SSM_HALO:SSM_HALO + TOK_TILE, :] = cur_ref[...]
    pad_ref[SSM_HALO + TOK_TILE:, :] = jnp.where(last, 0.0, next_ref[...])
    base = SSM_HALO - SSM_CONV_K // 2
    acc = jnp.broadcast_to(cb_ref[...], (TOK_TILE, SSM_XBC))
    for k in range(SSM_CONV_K):
        acc = acc + cw_ref[k:k + 1, :] * pad_ref[base + k:base + k + TOK_TILE, :]
    xc_ref[...] = _silu(acc)

    L = SSD_CHUNK
    a_row = -jnp.exp(alog_ref[...])
    ri = lax.broadcasted_iota(jnp.int32, (L, L), 0)
    ci = lax.broadcasted_iota(jnp.int32, (L, L), 1)
    keep = (ci <= ri) if fwd else (ci >= ri)
    tri_col = keep.astype(f32)
    tri_row = ((ri <= ci) if fwd else (ri >= ci)).astype(f32)
    lane = lax.broadcasted_iota(jnp.int32, (1, LANES), 1)
    er = lax.broadcasted_iota(jnp.int32, (LANES, SSM_WIDTH), 0)
    ec = lax.broadcasted_iota(jnp.int32, (LANES, SSM_WIDTH), 1)
    expand = (er == direction * SSM_HEADS + ec // SSM_HEAD_DIM).astype(f32)
    heads_per_group = SSM_HEADS // SSM_GROUPS
    heads_per_slab = LANES // SSM_HEAD_DIM
    tot_at = L - 1 if fwd else 0

    chunks = range(TOK_TILE // L)
    for cidx in (chunks if fwd else reversed(chunks)):
        r0 = cidx * L
        xs = xc_ref[r0:r0 + L, 0:SSM_WIDTH]
        bm = xc_ref[r0:r0 + L, SSM_WIDTH:SSM_WIDTH + LANES]
        cm = xc_ref[r0:r0 + L, SSM_WIDTH + LANES:SSM_XBC]
        dt = _softplus(dt_ref[r0:r0 + L, :] + dtb_ref[...])
        da = dt * a_row
        acum_col = jnp.dot(tri_col, da, precision=HIGHEST, preferred_element_type=f32)
        acum_row = jnp.dot(da.T, tri_row, precision=HIGHEST, preferred_element_type=f32)
        tot_row = acum_col[tot_at:tot_at + 1, :]
        tot_col = acum_row[:, tot_at:tot_at + 1]
        decay_out = jnp.exp(tot_col - acum_row)
        decay_in = jnp.exp(acum_col)
        chunk_decay = jnp.exp(tot_row)
        xdt = (xs * jnp.dot(dt, expand, precision=HIGHEST, preferred_element_type=f32)).astype(bf16)
        bt = bm.T
        bb = bm.astype(bf16)
        ys = []
        for g in range(SSM_GROUPS):
            in_group = (lane >= g * SSM_STATE) & (lane < (g + 1) * SSM_STATE)
            cg = jnp.where(in_group, cm, 0.0)
            gmat = lax.dot_general(cg.astype(bf16), bb, (((1,), (1,)), ((), ())), preferred_element_type=f32)
            for hh in range(heads_per_group):
                h = g * heads_per_group + hh
                hl = direction * SSM_HEADS + h
                slab = xdt[:, (h // heads_per_slab) * LANES:(h // heads_per_slab + 1) * LANES]
                seg = acum_col[:, hl:hl + 1] - acum_row[hl:hl + 1, :]
                m = (gmat * jnp.exp(jnp.where(keep, seg, -jnp.inf))).astype(bf16)
                y_h = jnp.dot(m, slab, preferred_element_type=f32)
                c_in = (cg * decay_in[:, hl:hl + 1]).astype(bf16)
                state = st_ref[h]
                y_h = y_h + jnp.dot(c_in, state.astype(bf16), preferred_element_type=f32)
                b_out = (bt * decay_out[hl:hl + 1, :]).astype(bf16)
                st_ref[h] = chunk_decay[:, hl:hl + 1] * state + jnp.dot(b_out, slab, preferred_element_type=f32)
                ys.append(y_h)
        for j in range(SSM_WIDTH // LANES):
            y_slab = jnp.where(lane < SSM_HEAD_DIM, ys[heads_per_slab * j], ys[heads_per_slab * j + 1])
            if fwd:
                y_slab = y_slab + dskip_ref[:, j * LANES:(j + 1) * LANES] * xs[:, j * LANES:(j + 1) * LANES]
            y_ref[r0:r0 + L, j * LANES:(j + 1) * LANES] = y_slab

    @pl.when(leave)
    def _():
        fin_ref[...] = st_ref[...]


def _ssd(geom, direction, xbc, dt, init, conv_w, conv_b, dt_bias, a_log, d_skip):
    nt = geom.n_tiles
    tidx = (lambda i: i) if direction == 0 else (lambda i: nt - 1 - i)
    per = TOK_TILE // SSM_HALO
    n_halo = geom.n_tok // SSM_HALO
    n_seq = geom.n_ctx_seq + geom.n_lat_seq
    const = lambda a: pl.BlockSpec(a.shape, lambda i: (0,) * a.ndim)
    st_block = (None, SSM_HEADS, LANES, LANES)
    y, fin = pl.pallas_call(
        functools.partial(_ssd_kernel, geom, direction),
        grid=(nt,),
        in_specs=[
            pl.BlockSpec((TOK_TILE, SSM_XBC), lambda i: (tidx(i), 0)),
            pl.BlockSpec((SSM_HALO, SSM_XBC), lambda i: (jnp.maximum(tidx(i) * per - 1, 0), 0)),
            pl.BlockSpec((SSM_HALO, SSM_XBC), lambda i: (jnp.minimum((tidx(i) + 1) * per, n_halo - 1), 0)),
            pl.BlockSpec((TOK_TILE, LANES), lambda i: (tidx(i), 0)),
            pl.BlockSpec(st_block, lambda i: (jnp.maximum(geom.seq_index(tidx(i)) - geom.n_ctx_seq, 0), 0, 0, 0)),
            const(conv_w), const(conv_b), const(dt_bias), const(a_log), const(d_skip),
        ],
        out_specs=[
            pl.BlockSpec((TOK_TILE, SSM_WIDTH), lambda i: (tidx(i), 0)),
            pl.BlockSpec(st_block, lambda i: (geom.seq_index(tidx(i)), 0, 0, 0)),
        ],
        out_shape=[
            jax.ShapeDtypeStruct((geom.n_tok, SSM_WIDTH), f32),
            jax.ShapeDtypeStruct((n_seq, SSM_HEADS, LANES, LANES), f32),
        ],
        scratch_shapes=[
            pltpu.VMEM((TOK_TILE + 2 * SSM_HALO, SSM_XBC), f32),
            pltpu.VMEM((TOK_TILE, SSM_XBC), f32),
            pltpu.VMEM((SSM_HEADS, LANES, LANES), f32),
        ],
        compiler_params=_cparams("arbitrary"),
        name="ssd_scan",
    )(xbc, xbc, xbc, dt, init, conv_w, conv_b, dt_bias, a_log, d_skip)
    return y, fin


def _pad_states(s):
    return jnp.tile(jnp.swapaxes(s, -1, -2), (1, 1, LANES // SSM_STATE, LANES // SSM_HEAD_DIM))


def _unpad_states(fin):
    hpg = SSM_HEADS // SSM_GROUPS
    outs = []
    for h in range(SSM_HEADS):
        g, side = h // hpg, h % (LANES // SSM_HEAD_DIM)
        q = fin[:, h, g * SSM_STATE:(g + 1) * SSM_STATE, side * SSM_HEAD_DIM:(side + 1) * SSM_HEAD_DIM]
        outs.append(jnp.swapaxes(q, -1, -2))
    return jnp.stack(outs, axis=1)


def _out_proj_kernel(geom, actx_ref, alat_ref, conv_ref, yf_ref, yb_ref, z_ref, x_ref, mod_ref, sg_ref, w_ref,
                     g2_ref, rw_ref, rb_ref, xo_ref, h2_ref, ti_ref, gw_ref):
    attn = jnp.where(pl.program_id(0) < geom.ctx_tiles, actx_ref[...], alat_ref[...])
    gated = (yf_ref[...] + yb_ref[...]) * _silu(z_ref[...])
    ssm = gated * lax.rsqrt(jnp.mean(gated * gated, axis=-1, keepdims=True) + EPS) * sg_ref[...]
    o1 = ATT_WIDTH
    o2 = ATT_WIDTH + CONV_CH
    mix = (jnp.dot(attn.astype(bf16), w_ref[0:o1, :], preferred_element_type=f32)
           + jnp.dot(conv_ref[...].astype(bf16), w_ref[o1:o2, :], preferred_element_type=f32)
           + jnp.dot(ssm.astype(bf16), w_ref[o2:, :], preferred_element_type=f32))
    x = x_ref[...] + mod_ref[2:3, :] * mix
    xo_ref[...] = x
    y = x * lax.rsqrt(jnp.mean(x * x, axis=-1, keepdims=True) + EPS) * g2_ref[...]
    h2 = y * (1.0 + mod_ref[4:5, :]) + mod_ref[3:4, :]
    h2_ref[...] = h2.astype(bf16)
    logits = jnp.dot(h2, rw_ref[...], precision=HIGHEST, preferred_element_type=f32) + rb_ref[...]
    lane = lax.broadcasted_iota(jnp.int32, (1, LANES), 1)
    lane_f = lane.astype(f32)
    lg = jnp.where(lane < N_EXPERTS, logits, -jnp.inf)
    idx_out = jnp.zeros(logits.shape, jnp.int32)
    gate_out = jnp.zeros(logits.shape, f32)
    top0 = None
    denom = None
    for k in range(TOP_K):
        m = jnp.max(lg, axis=-1, keepdims=True)
        idx = jnp.min(jnp.where(lg == m, lane_f, float(LANES)), axis=-1, keepdims=True).astype(jnp.int32)
        lg = jnp.where(lane == idx, -jnp.inf, lg)
        if k == 0:
            top0 = m
        e = jnp.exp(m - top0)
        denom = e if k == 0 else denom + e
        idx_out = jnp.where(lane == k, idx, idx_out)
        gate_out = jnp.where(lane == k, e, gate_out)
    ti_ref[...] = idx_out
    gw_ref[...] = gate_out / denom


def _out_proj(geom, layer, attn_ctx, attn_lat, conv, y_f, y_b, z, x, mod, ssm_norm_g, w_out, norm2_g, router_w,
              router_b):
    n = geom.n_tok
    tile = lambda w: pl.BlockSpec((TOK_TILE, w), lambda i: (i, 0))
    const = lambda a: pl.BlockSpec(a.shape, lambda i: (0,) * a.ndim)
    nct = geom.ctx_tiles
    return pl.pallas_call(
        functools.partial(_out_proj_kernel, geom),
        grid=(geom.n_tiles,),
        in_specs=[
            pl.BlockSpec((TOK_TILE, ATT_WIDTH), lambda i: (jnp.minimum(i, nct - 1), 0)),
            pl.BlockSpec((TOK_TILE, ATT_WIDTH), lambda i: (jnp.maximum(i - nct, 0), 0)),
            tile(CONV_CH), tile(SSM_WIDTH), tile(SSM_WIDTH), tile(SSM_WIDTH), tile(D_MODEL),
            pl.BlockSpec((None, None, 6, D_MODEL), lambda i: (layer, geom.cond_row(i), 0, 0)),
            const(ssm_norm_g), const(w_out), const(norm2_g), const(router_w), const(router_b),
        ],
        out_specs=[tile(D_MODEL), tile(D_MODEL), tile(LANES), tile(LANES)],
        out_shape=[jax.ShapeDtypeStruct((n, D_MODEL), f32), jax.ShapeDtypeStruct((n, D_MODEL), bf16),
                   jax.ShapeDtypeStruct((n, LANES), jnp.int32), jax.ShapeDtypeStruct((n, LANES), f32)],
        compiler_params=_cparams("arbitrary"),
        name="out_proj",
    )(attn_ctx, attn_lat, conv, y_f, y_b, z, x, mod, ssm_norm_g, w_out, norm2_g, router_w, router_b)


def _moe_kernel(be_ref, nused_ref, x_ref, wg_ref, bg_ref, wu_ref, bu_ref, wd_ref, bd_ref, o_ref,
                wg_s, wu_s, wd_s):
    i = pl.program_id(0)
    prev_e = be_ref[jnp.maximum(i - 1, 0)]
    new_expert = (i == 0) | (be_ref[i] != prev_e)

    @pl.when(new_expert)
    def _():
        wg_s[...] = wg_ref[0].astype(bf16)
        wu_s[...] = wu_ref[0].astype(bf16)
        wd_s[...] = wd_ref[0].astype(bf16)

    @pl.when(i < nused_ref[0])
    def _():
        x = x_ref[...]
        g = jnp.dot(x, wg_s[...], preferred_element_type=f32) + bg_ref[0]
        u = jnp.dot(x, wu_s[...], preferred_element_type=f32) + bu_ref[0]
        g = jnp.minimum(g, SWIGLU_LIMIT)
        u = jnp.clip(u, -SWIGLU_LIMIT, SWIGLU_LIMIT)
        act = g * jax.nn.sigmoid(SWIGLU_ALPHA * g) * (u + 1.0)
        o_ref[...] = jnp.dot(act.astype(bf16), wd_s[...], preferred_element_type=f32) + bd_ref[0]

    @pl.when(i >= nused_ref[0])
    def _():
        o_ref[...] = jnp.zeros_like(o_ref)


def _moe(layer, xs, block_e, n_used, w_gate, b_gate, w_up, b_up, w_down, b_down):
    n_rows, d = xs.shape
    n_blocks = n_rows // MOE_TILE
    d_ff = w_gate.shape[-1]
    w_spec = lambda r, c: pl.BlockSpec((None, 1, r, c), lambda i, be, nu: (layer, be[i], 0, 0))
    b_spec = lambda c: pl.BlockSpec((None, 1, 1, c), lambda i, be, nu: (layer, be[i], 0, 0))
    row_spec = pl.BlockSpec((MOE_TILE, d), lambda i, be, nu: (i, 0))
    depth, n_e = b_gate.shape[:2]
    return pl.pallas_call(
        _moe_kernel,
        grid_spec=pltpu.PrefetchScalarGridSpec(
            num_scalar_prefetch=2,
            grid=(n_blocks,),
            in_specs=[row_spec, w_spec(d, d_ff), b_spec(d_ff), w_spec(d, d_ff), b_spec(d_ff),
                      w_spec(d_ff, d), b_spec(d)],
            out_specs=row_spec,
            scratch_shapes=[pltpu.VMEM((d, d_ff), bf16), pltpu.VMEM((d, d_ff), bf16), pltpu.VMEM((d_ff, d), bf16)],
        ),
        out_shape=jax.ShapeDtypeStruct((n_rows, d), f32),
        compiler_params=_cparams("arbitrary"),
        name="moe_experts",
    )(block_e, n_used, xs, w_gate, b_gate.reshape(depth, n_e, 1, d_ff), w_up, b_up.reshape(depth, n_e, 1, d_ff),
      w_down, b_down.reshape(depth, n_e, 1, d))


def _route(top_idx, n_blocks):
    n = top_idx.shape[0]
    n_assign = n * TOP_K
    i32 = jnp.int32
    flat_e = top_idx.reshape(-1)
    iota = jnp.arange(n_assign, dtype=i32)
    sorted_e, order = lax.sort((flat_e, iota), num_keys=1, is_stable=True)
    counts = jnp.sum((flat_e[:, None] == jnp.arange(N_EXPERTS, dtype=i32)[None, :]).astype(i32), axis=0)
    grp_start = jnp.cumsum(counts) - counts
    nb = (counts + MOE_TILE - 1) // MOE_TILE
    blk_end = jnp.cumsum(nb)
    blk_start = blk_end - nb
    dest_sorted = blk_start[sorted_e] * MOE_TILE + (iota - grp_start[sorted_e])
    _, dest = lax.sort((order, dest_sorted), num_keys=1)
    block_e = jnp.minimum(jnp.sum((blk_end[None, :] <= jnp.arange(n_blocks, dtype=i32)[:, None]).astype(i32), axis=1),
                          N_EXPERTS - 1)
    n_used = blk_end[-1:].astype(i32)
    row = jnp.arange(n_blocks * MOE_TILE, dtype=i32)
    row_e = block_e[row // MOE_TILE]
    in_grp = row - blk_start[row_e] * MOE_TILE
    src = order[jnp.clip(grp_start[row_e] + in_grp, 0, n_assign - 1)]
    row_tok = jnp.where(in_grp < counts[row_e], src // TOP_K, 0)
    return dest.reshape(n, TOP_K), block_e, n_used, row_tok


def _combine_kernel(final, x_ref, rows_ref, gw_ref, mod_ref, *rest):
    ffn = rows_ref[0] * gw_ref[:, 0:1]
    for k in range(1, TOP_K):
        ffn = ffn + rows_ref[k] * gw_ref[:, k:k + 1]
    x = x_ref[...] + mod_ref[5:6, :] * ffn
    if final:
        fg_ref, y_ref = rest
        y_ref[...] = x * lax.rsqrt(jnp.mean(x * x, axis=-1, keepdims=True) + EPS) * fg_ref[...]
    else:
        rest[0][...] = x


def _combine(geom, layer, x, rows, gate_w, mod, final_g):
    n = geom.n_tok
    final = final_g is not None
    tile = pl.BlockSpec((TOK_TILE, D_MODEL), lambda i: (i, 0))
    in_specs = [
        tile,
        pl.BlockSpec((TOP_K, TOK_TILE, D_MODEL), lambda i: (0, i, 0)),
        pl.BlockSpec((TOK_TILE, LANES), lambda i: (i, 0)),
        pl.BlockSpec((None, None, 6, D_MODEL), lambda i: (layer, geom.cond_row(i), 0, 0)),
    ]
    args = [x, rows, gate_w, mod]
    if final:
        in_specs.append(pl.BlockSpec(final_g.shape, lambda i: (0, 0)))
        args.append(final_g)
    return pl.pallas_call(
        functools.partial(_combine_kernel, final),
        grid=(geom.n_tiles,),
        in_specs=in_specs,
        out_specs=tile,
        out_shape=jax.ShapeDtypeStruct((n, D_MODEL), f32),
        compiler_params=_cparams("arbitrary"),
        name="moe_combine",
    )(*args)


def _pad_lanes(a, width=LANES):
    return jnp.pad(a, [(0, 0)] * (a.ndim - 1) + [(0, width - a.shape[-1])])


def kernel(x_prompt, x_sample, c, cache_k, cache_v, state_ssm, c_ctx, norm1_g, norm2_g, w_ada, b_ada, w_in, w_out, attn_lambda, attn_subln_g, conv_dw_w, conv_dw_b, conv_ln_g, conv_ln_b, ssm_conv_w, ssm_conv_b, ssm_dt_bias, ssm_a_log, ssm_d, ssm_norm_g, router_w, router_b, w_gate, b_gate, w_up, b_up, w_down, b_down, final_g):
    n_ctx_seq, ctx_len, d = x_prompt.shape
    n_lat_seq, lat_len, _ = x_sample.shape
    past_len = cache_k.shape[-2]
    geom = _Geom(n_ctx_seq, ctx_len, n_lat_seq, lat_len)
    n = geom.n_tok

    x = jnp.concatenate([x_prompt.reshape(geom.n_ctx, d), x_sample.reshape(geom.n_lat, d)], axis=0)
    cond = jnp.concatenate([c_ctx[None, :], c, jnp.zeros((COND_ROWS - 1 - n_lat_seq, d), f32)], axis=0)
    mod = _ada(cond, w_ada, b_ada).reshape(DEPTH, COND_ROWS, 6, d)
    tables = _rope_tables(geom)

    n_assign = n * TOP_K
    n_blocks = -(-n_assign // MOE_TILE) + N_EXPERTS

    new_k, new_v, new_s = [], [], []
    y_final = None
    for l in range(DEPTH):
        w_main = w_in[l, :, :N_MAIN].astype(bf16)
        w_dt = _pad_lanes(w_in[l, :, N_MAIN:]).astype(bf16)
        q, k, v, glu, z, xbc, dt = _in_proj(geom, l, x, mod, norm1_g[l][None, :], w_main, w_dt, tables)

        lam_init = 0.8 - 0.6 * math.exp(-0.3 * l)
        subln = jnp.tile(attn_subln_g[l], LANES // ATT_V_DIM)[None, :]
        k_cache = cache_k[:, l].transpose(0, 3, 1, 2, 4).reshape(n_lat_seq, past_len, QK_WIDTH)
        v_cache = cache_v[:, l].transpose(0, 2, 1, 3).reshape(n_lat_seq, past_len, ATT_WIDTH)
        k_all = jnp.concatenate([k[geom.n_ctx:].reshape(n_lat_seq, lat_len, QK_WIDTH), k_cache], axis=1).astype(bf16)
        v_all = jnp.concatenate([v[geom.n_ctx:].reshape(n_lat_seq, lat_len, ATT_WIDTH), v_cache], axis=1).astype(bf16)
        attn_ctx = _attention(lam_init, attn_lambda[l], subln, q, k, v, n_ctx_seq, ctx_len, ctx_len, 0, False)
        attn_lat = _attention(lam_init, attn_lambda[l], subln, q, k_all, v_all, n_lat_seq, lat_len,
                              lat_len + past_len, geom.n_ctx, True)

        conv = _cconv(geom, glu, conv_dw_w[l], conv_dw_b[l][None, :], conv_ln_g[l][None, :], conv_ln_b[l][None, :])

        dt_bias = _pad_lanes(ssm_dt_bias[l].reshape(1, 2 * SSM_HEADS))
        a_log = _pad_lanes(ssm_a_log[l].reshape(1, 2 * SSM_HEADS))
        d_skip = jnp.repeat(ssm_d[l], SSM_HEAD_DIM)[None, :]
        ys, fins = [], []
        for direction in range(2):
            init = _pad_states(state_ssm[:, l, direction])
            y_d, fin = _ssd(geom, direction, xbc, dt, init, ssm_conv_w[l], ssm_conv_b[l][None, :], dt_bias, a_log, d_skip)
            ys.append(y_d)
            fins.append(_unpad_states(fin[:n_ctx_seq]))

        x, h2, top_idx, gate_w = _out_proj(geom, l, attn_ctx, attn_lat, conv, ys[0], ys[1], z, x, mod, ssm_norm_g[l][None, :],
                                  w_out[l].astype(bf16), norm2_g[l][None, :],
                                  _pad_lanes(router_w[l]), _pad_lanes(router_b[l][None, :]))

        dest, block_e, n_used, row_tok = _route(top_idx[:, :TOP_K], n_blocks)
        row_tok, dest_flat = lax.optimization_barrier((row_tok, dest.T.reshape(-1)))
        xs = jnp.take(h2, row_tok, axis=0)
        eo = _moe(l, xs, block_e, n_used, w_gate, b_gate, w_up, b_up, w_down, b_down)
        rows = jnp.take(eo, dest_flat, axis=0).reshape(TOP_K, n, d)
        if l == DEPTH - 1:
            y_final = _combine(geom, l, x, rows, gate_w, mod, final_g[None, :])
        else:
            x = _combine(geom, l, x, rows, gate_w, mod, None)

        new_k.append(k[:geom.n_ctx].reshape(n_ctx_seq, ctx_len, ATT_HEADS, 2, ATT_QK_DIM).transpose(0, 2, 3, 1, 4))
        new_v.append(v[:geom.n_ctx].reshape(n_ctx_seq, ctx_len, ATT_HEADS, ATT_V_DIM).transpose(0, 2, 1, 3))
        new_s.append(jnp.stack(fins, axis=1))

    y_prompt = y_final[:geom.n_ctx].reshape(n_ctx_seq, ctx_len, d)
    y_sample = y_final[geom.n_ctx:].reshape(n_lat_seq, lat_len, d)
    return (y_prompt, y_sample, jnp.stack(new_k, axis=1), jnp.stack(new_v, axis=1), jnp.stack(new_s, axis=1))
```

```python
import functools
import math

import numpy as np
import jax
import jax.numpy as jnp
from jax import lax
from jax.experimental import pallas as pl
from jax.experimental.pallas import tpu as pltpu

f32 = jnp.float32
bf16 = jnp.bfloat16
HIGHEST = lax.Precision.HIGHEST

D_MODEL = 1024
DEPTH = 2
GRID_W = 64
ATT_HEADS = 4
ATT_QK_DIM = 32
ATT_V_DIM = 64
ATT_WIDTH = 256
QK_WIDTH = 256
ROPE_AXIS_DIM = 16
ROPE_BASE = 10000.0
CONV_CH = 256
CONV_K = 31
SSM_WIDTH = 512
SSM_HEAD_DIM = 64
SSM_HEADS = 8
SSM_GROUPS = 2
SSM_STATE = 64
SSM_CONV_K = 5
SSM_XBC = 768
N_MAIN = 2560
N_EXPERTS = 32
TOP_K = 4
SWIGLU_LIMIT = 7.0
SWIGLU_ALPHA = 1.702
EPS = 1e-6

LANES = 128
SUBLANES = 8
TOK_TILE = 256
SSD_CHUNK = 128
CONV_HALO = 16
SSM_HALO = 8
MOE_TILE = 512
COND_ROWS = 16
VMEM_LIMIT = 56 * 1024 * 1024


def _cparams(*sem):
    return pltpu.CompilerParams(dimension_semantics=sem, vmem_limit_bytes=VMEM_LIMIT)


def _silu(x):
    return x * jax.nn.sigmoid(x)


def _ada_kernel(cond_ref, w_ref, b_ref, o_ref):
    s = _silu(cond_ref[...]).astype(bf16)
    o_ref[0] = jnp.dot(s, w_ref[0].astype(bf16), preferred_element_type=f32) + b_ref[0]


def _ada(cond, w_ada, b_ada):
    depth, d, n6 = w_ada.shape
    tn = 1536
    return pl.pallas_call(
        _ada_kernel,
        grid=(depth, n6 // tn),
        in_specs=[
            pl.BlockSpec((COND_ROWS, d), lambda l, j: (0, 0)),
            pl.BlockSpec((1, d, tn), lambda l, j: (l, 0, j)),
            pl.BlockSpec((1, 1, tn), lambda l, j: (l, 0, j)),
        ],
        out_specs=pl.BlockSpec((1, COND_ROWS, tn), lambda l, j: (l, 0, j)),
        out_shape=jax.ShapeDtypeStruct((depth, COND_ROWS, n6), f32),
        compiler_params=_cparams("arbitrary", "arbitrary"),
        name="ada",
    )(cond, w_ada, b_ada.reshape(depth, 1, n6))


class _Geom:
    def __init__(self, n_ctx_seq, ctx_len, n_lat_seq, lat_len):
        self.n_ctx_seq, self.ctx_len, self.n_lat_seq, self.lat_len = n_ctx_seq, ctx_len, n_lat_seq, lat_len
        self.n_ctx = n_ctx_seq * ctx_len
        self.n_lat = n_lat_seq * lat_len
        self.n_tok = self.n_ctx + self.n_lat
        assert ctx_len % TOK_TILE == 0 and lat_len % TOK_TILE == 0
        self.ctx_tiles = self.n_ctx // TOK_TILE
        self.tiles_per_ctx = ctx_len // TOK_TILE
        self.tiles_per_lat = lat_len // TOK_TILE
        self.n_tiles = self.n_tok // TOK_TILE

    def cond_row(self, i):
        return jnp.where(i < self.ctx_tiles, 0, 1 + (i - self.ctx_tiles) // self.tiles_per_lat)

    def seq_pos(self, i):
        j_ctx = i % self.tiles_per_ctx
        j_lat = (i - self.ctx_tiles) % self.tiles_per_lat
        is_ctx = i < self.ctx_tiles
        first = jnp.where(is_ctx, j_ctx == 0, j_lat == 0)
        last = jnp.where(is_ctx, j_ctx == self.tiles_per_ctx - 1, j_lat == self.tiles_per_lat - 1)
        return first, last

    def seq_index(self, i):
        return jnp.where(i < self.ctx_tiles, i // self.tiles_per_ctx,
                         self.n_ctx_seq + (i - self.ctx_tiles) // self.tiles_per_lat)


def _rope_tables(geom):
    t = np.arange(geom.lat_len)
    row = (t // GRID_W).astype(np.float64)
    col = (t % GRID_W).astype(np.float64)
    inv_freq = 1.0 / (ROPE_BASE ** (np.arange(0, ROPE_AXIS_DIM, 2, dtype=np.float64) / ROPE_AXIS_DIM))
    ang_r = row[:, None] * inv_freq[None, :]
    ang_c = col[:, None] * inv_freq[None, :]
    ang = np.concatenate([ang_r, ang_r, ang_c, ang_c], axis=-1)
    cos, sin = np.cos(ang), np.sin(ang)
    half = ROPE_AXIS_DIM // 2
    lane = np.arange(ATT_QK_DIM)
    first_half = (lane % ROPE_AXIS_DIM) < half
    sin_a = np.where(first_half[None, :], -sin, 0.0)
    sin_b = np.where(first_half[None, :], 0.0, sin)
    rep = LANES // ATT_QK_DIM

    def full(tab, ident):
        tab = np.tile(tab, (1, rep))
        return jnp.asarray(np.concatenate([np.full((TOK_TILE, LANES), ident), tab], axis=0), f32)

    return full(cos, 1.0), full(sin_a, 0.0), full(sin_b, 0.0)


def _in_proj_kernel(x_ref, mod_ref, g_ref, w_ref, wdt_ref, cos_ref, sa_ref, sb_ref,
                    q_ref, k_ref, v_ref, glu_ref, z_ref, xbc_ref, dt_ref):
    x = x_ref[...]
    y = x * lax.rsqrt(jnp.mean(x * x, axis=-1, keepdims=True) + EPS) * g_ref[...]
    h = (y * (1.0 + mod_ref[1:2, :]) + mod_ref[0:1, :]).astype(bf16)
    p = jnp.dot(h, w_ref[...], preferred_element_type=f32)
    dt_ref[...] = jnp.dot(h, wdt_ref[...], preferred_element_type=f32)
    cos, sa, sb = cos_ref[...], sa_ref[...], sb_ref[...]
    half = ROPE_AXIS_DIM // 2

    def rope(u):
        return u * cos + pltpu.roll(u, LANES - half, 1) * sa + pltpu.roll(u, half, 1) * sb

    scale = ATT_QK_DIM ** -0.5
    for j in range(QK_WIDTH // LANES):
        lo = j * LANES
        q_ref[:, lo:lo + LANES] = (rope(p[:, lo:lo + LANES]) * scale).astype(bf16)
        k_ref[:, lo:lo + LANES] = rope(p[:, QK_WIDTH + lo:QK_WIDTH + lo + LANES])
    o = 2 * QK_WIDTH
    v_ref[...] = p[:, o:o + ATT_WIDTH]
    o += ATT_WIDTH
    glu_ref[...] = p[:, o:o + 2 * CONV_CH]
    o += 2 * CONV_CH
    z_ref[...] = p[:, o:o + SSM_WIDTH]
    o += SSM_WIDTH
    xbc_ref[...] = p[:, o:o + SSM_XBC]


def _in_proj(geom, layer, x, mod, norm_g, w_main, w_dt, tables):
    n = geom.n_tok
    tile = lambda w: pl.BlockSpec((TOK_TILE, w), lambda i: (i, 0))
    const = lambda a: pl.BlockSpec(a.shape, lambda i: (0,) * a.ndim)
    tab_spec = pl.BlockSpec(
        (TOK_TILE, LANES),
        lambda i: (jnp.where(i < geom.ctx_tiles, 0, 1 + (i - geom.ctx_tiles) % geom.tiles_per_lat), 0))
    widths = (QK_WIDTH, QK_WIDTH, ATT_WIDTH, 2 * CONV_CH, SSM_WIDTH, SSM_XBC, LANES)
    dtypes = (bf16, f32, f32, f32, f32, f32, f32)
    return pl.pallas_call(
        _in_proj_kernel,
        grid=(geom.n_tiles,),
        in_specs=[
            tile(D_MODEL),
            pl.BlockSpec((None, None, 6, D_MODEL), lambda i: (layer, geom.cond_row(i), 0, 0)),
            const(norm_g), const(w_main), const(w_dt), tab_spec, tab_spec, tab_spec,
        ],
        out_specs=[tile(w) for w in widths],
        out_shape=[jax.ShapeDtypeStruct((n, w), dt) for w, dt in zip(widths, dtypes)],
        compiler_params=_cparams("arbitrary"),
        name="in_proj",
    )(x, mod, norm_g, w_main, w_dt, *tables)


def _attn_kernel(lam_init, al_ref, g_ref, q_ref, k_ref, v_ref, o_ref):
    al = al_ref[...]
    lam = (jnp.exp(jnp.sum(al[0:1] * al[1:2], axis=-1, keepdims=True))
           - jnp.exp(jnp.sum(al[2:3] * al[3:4], axis=-1, keepdims=True)) + lam_init)
    lane = lax.broadcasted_iota(jnp.int32, (1, LANES), 1)
    heads_per_slab = LANES // ATT_V_DIM
    for j in range(ATT_WIDTH // LANES):
        sl = slice(j * LANES, (j + 1) * LANES)
        qs = q_ref[:, sl]
        ks = k_ref[:, sl].astype(bf16)
        vs = v_ref[:, sl].astype(bf16)
        a_heads = []
        for hl in range(heads_per_slab):
            comps = []
            for c in range(2):
                lo = hl * ATT_V_DIM + c * ATT_QK_DIM
                qm = jnp.where((lane >= lo) & (lane < lo + ATT_QK_DIM), qs, jnp.zeros_like(qs))
                s = lax.dot_general(qm, ks, (((1,), (1,)), ((), ())), preferred_element_type=f32)
                e = jnp.exp(s - jnp.max(s, axis=-1, keepdims=True))
                l = jnp.sum(e, axis=-1, keepdims=True)
                comps.append(jnp.dot(e.astype(bf16), vs, preferred_element_type=f32) / l)
            a_heads.append(comps[0] - lam * comps[1])
        a = jnp.where(lane < ATT_V_DIM, a_heads[0], a_heads[1])
        a2 = a * a
        ms0 = jnp.sum(jnp.where(lane < ATT_V_DIM, a2, 0.0), axis=-1, keepdims=True)
        ms1 = jnp.sum(jnp.where(lane < ATT_V_DIM, 0.0, a2), axis=-1, keepdims=True)
        ms = jnp.where(lane < ATT_V_DIM, ms0, ms1) * (1.0 / ATT_V_DIM)
        o_ref[:, sl] = a * lax.rsqrt(ms + EPS) * g_ref[...] * (1.0 - lam_init)


def _attention(lam_init, attn_lambda, subln_g, q, k, v, n_seq, q_len, k_len, q_row0, kv_is_3d):
    tq = TOK_TILE
    qb = q_len // tq
    q0 = q_row0 // tq
    if kv_is_3d:
        kv_spec = pl.BlockSpec((None, k_len, ATT_WIDTH), lambda b, i: (b, 0, 0))
    else:
        kv_spec = pl.BlockSpec((k_len, ATT_WIDTH), lambda b, i: (q_row0 // k_len + b, 0))
    return pl.pallas_call(
        functools.partial(_attn_kernel, lam_init),
        grid=(n_seq, qb),
        in_specs=[
            pl.BlockSpec(attn_lambda.shape, lambda b, i: (0, 0)),
            pl.BlockSpec(subln_g.shape, lambda b, i: (0, 0)),
            pl.BlockSpec((tq, ATT_WIDTH), lambda b, i: (q0 + b * qb + i, 0)),
            kv_spec, kv_spec,
        ],
        out_specs=pl.BlockSpec((tq, ATT_WIDTH), lambda b, i: (b * qb + i, 0)),
        out_shape=jax.ShapeDtypeStruct((n_seq * q_len, ATT_WIDTH), f32),
        compiler_params=_cparams("arbitrary", "arbitrary"),
        name="attention",
    )(attn_lambda, subln_g, q, k, v)


def _halo_specs(geom, width, halo):
    per = TOK_TILE // halo
    n_halo = geom.n_tok // halo
    prev = pl.BlockSpec((halo, width), lambda i: (jnp.maximum(i * per - 1, 0), 0))
    nxt = pl.BlockSpec((halo, width), lambda i: (jnp.minimum((i + 1) * per, n_halo - 1), 0))
    return prev, nxt


def _cconv_kernel(geom, cur_ref, prev_ref, next_ref, w_ref, b_ref, lg_ref, lb_ref, o_ref, pad_ref):
    first, last = geom.seq_pos(pl.program_id(0))

    def glu(ref):
        return ref[:, :CONV_CH] * jax.nn.sigmoid(ref[:, CONV_CH:])

    pad_ref[0:CONV_HALO, :] = jnp.where(first, 0.0, glu(prev_ref))
    pad_ref[CONV_HALO:CONV_HALO + TOK_TILE, :] = glu(cur_ref)
    pad_ref[CONV_HALO + TOK_TILE:, :] = jnp.where(last, 0.0, glu(next_ref))
    rows = 64
    base = CONV_HALO - CONV_K // 2
    for r0 in range(0, TOK_TILE, rows):
        acc = jnp.broadcast_to(b_ref[...], (rows, CONV_CH))
        for k in range(CONV_K):
            acc = acc + w_ref[k:k + 1, :] * pad_ref[r0 + base + k:r0 + base + k + rows, :]
        xc = acc - jnp.mean(acc, axis=-1, keepdims=True)
        var = jnp.mean(xc * xc, axis=-1, keepdims=True)
        o_ref[r0:r0 + rows, :] = _silu(xc * lax.rsqrt(var + EPS) * lg_ref[...] + lb_ref[...])


def _cconv(geom, glu, w, b, ln_g, ln_b):
    prev, nxt = _halo_specs(geom, 2 * CONV_CH, CONV_HALO)
    const = lambda a: pl.BlockSpec(a.shape, lambda i: (0,) * a.ndim)
    return pl.pallas_call(
        functools.partial(_cconv_kernel, geom),
        grid=(geom.n_tiles,),
        in_specs=[pl.BlockSpec((TOK_TILE, 2 * CONV_CH), lambda i: (i, 0)), prev, nxt,
                  const(w), const(b), const(ln_g), const(ln_b)],
        out_specs=pl.BlockSpec((TOK_TILE, CONV_CH), lambda i: (i, 0)),
        out_shape=jax.ShapeDtypeStruct((geom.n_tok, CONV_CH), f32),
        scratch_shapes=[pltpu.VMEM((TOK_TILE + 2 * CONV_HALO, CONV_CH), f32)],
        compiler_params=_cparams("arbitrary"),
        name="conformer_conv",
    )(glu, glu, glu, w, b, ln_g, ln_b)


def _softplus(x):
    return jnp.maximum(x, 0.0) + jnp.log1p(jnp.exp(-jnp.abs(x)))


def _ssd_kernel(geom, direction, cur_ref, prev_ref, next_ref, dt_ref, init_ref, cw_ref, cb_ref, dtb_ref,
                alog_ref, dskip_ref, y_ref, fin_ref, pad_ref, xc_ref, st_ref):
    fwd = direction == 0
    i = pl.program_id(0)
    t = i if fwd else geom.n_tiles - 1 - i
    first, last = geom.seq_pos(t)
    enter, leave = (first, last) if fwd else (last, first)
    is_lat = t >= geom.ctx_tiles

    @pl.when(enter)
    def _():
        st_ref[...] = jnp.where(is_lat, init_ref[...], 0.0)

    pad_ref[0:SSM_HALO, :] = jnp.where(first, 0.0, prev_ref[...])
    pad_ref[SSM_HALO:SSM_HALO + TOK_TILE, :] = cur_ref[...]
    pad_ref[SSM_HALO + TOK_TILE:, :] = jnp.where(last, 0.0, next_ref[...])
    base = SSM_HALO - SSM_CONV_K // 2
    acc = jnp.broadcast_to(cb_ref[...], (TOK_TILE, SSM_XBC))
    for k in range(SSM_CONV_K):
        acc = acc + cw_ref[k:k + 1, :] * pad_ref[base + k:base + k + TOK_TILE, :]
    xc_ref[...] = _silu(acc)

    L = SSD_CHUNK
    a_row = -jnp.exp(alog_ref[...])
    ri = lax.broadcasted_iota(jnp.int32, (L, L), 0)
    ci = lax.broadcasted_iota(jnp.int32, (L, L), 1)
    keep = (ci <= ri) if fwd else (ci >= ri)
    tri_col = keep.astype(f32)
    tri_row = ((ri <= ci) if fwd else (ri >= ci)).astype(f32)
    lane = lax.broadcasted_iota(jnp.int32, (1, LANES), 1)
    er = lax.broadcasted_iota(jnp.int32, (LANES, SSM_WIDTH), 0)
    ec = lax.broadcasted_iota(jnp.int32, (LANES, SSM_WIDTH), 1)
    expand = (er == direction * SSM_HEADS + ec // SSM_HEAD_DIM).astype(f32)
    heads_per_group = SSM_HEADS // SSM_GROUPS
    heads_per_slab = LANES // SSM_HEAD_DIM
    tot_at = L - 1 if fwd else 0

    chunks = range(TOK_TILE // L)
    for cidx in (chunks if fwd else reversed(chunks)):
        r0 = cidx * L
        xs = xc_ref[r0:r0 + L, 0:SSM_WIDTH]
        bm = xc_ref[r0:r0 + L, SSM_WIDTH:SSM_WIDTH + LANES]
        cm = xc_ref[r0:r0 + L, SSM_WIDTH + LANES:SSM_XBC]
        dt = _softplus(dt_ref[r0:r0 + L, :] + dtb_ref[...])
        da = dt * a_row
        acum_col = jnp.dot(tri_col, da, precision=HIGHEST, preferred_element_type=f32)
        acum_row = jnp.dot(da.T, tri_row, precision=HIGHEST, preferred_element_type=f32)
        tot_row = acum_col[tot_at:tot_at + 1, :]
        tot_col = acum_row[:, tot_at:tot_at + 1]
        decay_out = jnp.exp(tot_col - acum_row)
        decay_in = jnp.exp(acum_col)
        chunk_decay = jnp.exp(tot_row)
        xdt = (xs * jnp.dot(dt, expand, precision=HIGHEST, preferred_element_type=f32)).astype(bf16)
        bt = bm.T
        bb = bm.astype(bf16)
        ys = []
        for g in range(SSM_GROUPS):
            in_group = (lane >= g * SSM_STATE) & (lane < (g + 1) * SSM_STATE)
            cg = jnp.where(in_group, cm, 0.0)
            gmat = lax.dot_general(cg.astype(bf16), bb, (((1,), (1,)), ((), ())), preferred_element_type=f32)
            for hh in range(heads_per_group):
                h = g * heads_per_group + hh
                hl = direction * SSM_HEADS + h
                slab = xdt[:, (h // heads_per_slab) * LANES:(h // heads_per_slab + 1) * LANES]
                seg = acum_col[:, hl:hl + 1] - acum_row[hl:hl + 1, :]
                m = (gmat * jnp.exp(jnp.where(keep, seg, -jnp.inf))).astype(bf16)
                y_h = jnp.dot(m, slab, preferred_element_type=f32)
                c_in = (cg * decay_in[:, hl:hl + 1]).astype(bf16)
                state = st_ref[h]
                y_h = y_h + jnp.dot(c_in, state.astype(bf16), preferred_element_type=f32)
                b_out = (bt * decay_out[hl:hl + 1, :]).astype(bf16)
                st_ref[h] = chunk_decay[:, hl:hl + 1] * state + jnp.dot(b_out, slab, preferred_element_type=f32)
                ys.append(y_h)
        for j in range(SSM_WIDTH // LANES):
            y_slab = jnp.where(lane < SSM_HEAD_DIM, ys[heads_per_slab * j], ys[heads_per_slab * j + 1])
            if fwd:
                y_slab = y_slab + dskip_ref[:, j * LANES:(j + 1) * LANES] * xs[:, j * LANES:(j + 1) * LANES]
            y_ref[r0:r0 + L, j * LANES:(j + 1) * LANES] = y_slab

    @pl.when(leave)
    def _():
        fin_ref[...] = st_ref[...]


def _ssd(geom, direction, xbc, dt, init, conv_w, conv_b, dt_bias, a_log, d_skip):
    nt = geom.n_tiles
    tidx = (lambda i: i) if direction == 0 else (lambda i: nt - 1 - i)
    per = TOK_TILE // SSM_HALO
    n_halo = geom.n_tok // SSM_HALO
    n_seq = geom.n_ctx_seq + geom.n_lat_seq
    const = lambda a: pl.BlockSpec(a.shape, lambda i: (0,) * a.ndim)
    st_block = (None, SSM_HEADS, LANES, LANES)
    y, fin = pl.pallas_call(
        functools.partial(_ssd_kernel, geom, direction),
        grid=(nt,),
        in_specs=[
            pl.BlockSpec((TOK_TILE, SSM_XBC), lambda i: (tidx(i), 0)),
            pl.BlockSpec((SSM_HALO, SSM_XBC), lambda i: (jnp.maximum(tidx(i) * per - 1, 0), 0)),
            pl.BlockSpec((SSM_HALO, SSM_XBC), lambda i: (jnp.minimum((tidx(i) + 1) * per, n_halo - 1), 0)),
            pl.BlockSpec((TOK_TILE, LANES), lambda i: (tidx(i), 0)),
            pl.BlockSpec(st_block, lambda i: (jnp.maximum(geom.seq_index(tidx(i)) - geom.n_ctx_seq, 0), 0, 0, 0)),
            const(conv_w), const(conv_b), const(dt_bias), const(a_log), const(d_skip),
        ],
        out_specs=[
            pl.BlockSpec((TOK_TILE, SSM_WIDTH), lambda i: (tidx(i), 0)),
            pl.BlockSpec(st_block, lambda i: (geom.seq_index(tidx(i)), 0, 0, 0)),
        ],
        out_shape=[
            jax.ShapeDtypeStruct((geom.n_tok, SSM_WIDTH), f32),
            jax.ShapeDtypeStruct((n_seq, SSM_HEADS, LANES, LANES), f32),
        ],
        scratch_shapes=[
            pltpu.VMEM((TOK_TILE + 2 * SSM_HALO, SSM_XBC), f32),
            pltpu.VMEM((TOK_TILE, SSM_XBC), f32),
            pltpu.VMEM((SSM_HEADS, LANES, LANES), f32),
        ],
        compiler_params=_cparams("arbitrary"),
        name="ssd_scan",
    )(xbc, xbc, xbc, dt, init, conv_w, conv_b, dt_bias, a_log, d_skip)
    return y, fin


def _pad_states(s):
    return jnp.tile(jnp.swapaxes(s, -1, -2), (1, 1, LANES // SSM_STATE, LANES // SSM_HEAD_DIM))


def _unpad_states(fin):
    hpg = SSM_HEADS // SSM_GROUPS
    outs = []
    for h in range(SSM_HEADS):
        g, side = h // hpg, h % (LANES // SSM_HEAD_DIM)
        q = fin[:, h, g * SSM_STATE:(g + 1) * SSM_STATE, side * SSM_HEAD_DIM:(side + 1) * SSM_HEAD_DIM]
        outs.append(jnp.swapaxes(q, -1, -2))
    return jnp.stack(outs, axis=1)


def _out_proj_kernel(geom, actx_ref, alat_ref, conv_ref, yf_ref, yb_ref, z_ref, x_ref, mod_ref, sg_ref, w_ref,
                     g2_ref, rw_ref, rb_ref, xo_ref, h2_ref, ti_ref, gw_ref):
    attn = jnp.where(pl.program_id(0) < geom.ctx_tiles, actx_ref[...], alat_ref[...])
    gated = (yf_ref[...] + yb_ref[...]) * _silu(z_ref[...])
    ssm = gated * lax.rsqrt(jnp.mean(gated * gated, axis=-1, keepdims=True) + EPS) * sg_ref[...]
    o1 = ATT_WIDTH
    o2 = ATT_WIDTH + CONV_CH
    mix = (jnp.dot(attn.astype(bf16), w_ref[0:o1, :], preferred_element_type=f32)
           + jnp.dot(conv_ref[...].astype(bf16), w_ref[o1:o2, :], preferred_element_type=f32)
           + jnp.dot(ssm.astype(bf16), w_ref[o2:, :], preferred_element_type=f32))
    x = x_ref[...] + mod_ref[2:3, :] * mix
    xo_ref[...] = x
    y = x * lax.rsqrt(jnp.mean(x * x, axis=-1, keepdims=True) + EPS) * g2_ref[...]
    h2 = y * (1.0 + mod_ref[4:5, :]) + mod_ref[3:4, :]
    h2_ref[...] = h2
    logits = jnp.dot(h2, rw_ref[...], precision=HIGHEST, preferred_element_type=f32) + rb_ref[...]
    lane = lax.broadcasted_iota(jnp.int32, (1, LANES), 1)
    lane_f = lane.astype(f32)
    lg = jnp.where(lane < N_EXPERTS, logits, -jnp.inf)
    idx_out = jnp.zeros(logits.shape, jnp.int32)
    gate_out = jnp.zeros(logits.shape, f32)
    top0 = None
    denom = None
    for k in range(TOP_K):
        m = jnp.max(lg, axis=-1, keepdims=True)
        idx = jnp.min(jnp.where(lg == m, lane_f, float(LANES)), axis=-1, keepdims=True).astype(jnp.int32)
        lg = jnp.where(lane == idx, -jnp.inf, lg)
        if k == 0:
            top0 = m
        e = jnp.exp(m - top0)
        denom = e if k == 0 else denom + e
        idx_out = jnp.where(lane == k, idx, idx_out)
        gate_out = jnp.where(lane == k, e, gate_out)
    ti_ref[...] = idx_out
    gw_ref[...] = gate_out / denom


def _out_proj(geom, layer, attn_ctx, attn_lat, conv, y_f, y_b, z, x, mod, ssm_norm_g, w_out, norm2_g, router_w,
              router_b):
    n = geom.n_tok
    tile = lambda w: pl.BlockSpec((TOK_TILE, w), lambda i: (i, 0))
    const = lambda a: pl.BlockSpec(a.shape, lambda i: (0,) * a.ndim)
    nct = geom.ctx_tiles
    return pl.pallas_call(
        functools.partial(_out_proj_kernel, geom),
        grid=(geom.n_tiles,),
        in_specs=[
            pl.BlockSpec((TOK_TILE, ATT_WIDTH), lambda i: (jnp.minimum(i, nct - 1), 0)),
            pl.BlockSpec((TOK_TILE, ATT_WIDTH), lambda i: (jnp.maximum(i - nct, 0), 0)),
            tile(CONV_CH), tile(SSM_WIDTH), tile(SSM_WIDTH), tile(SSM_WIDTH), tile(D_MODEL),
            pl.BlockSpec((None, None, 6, D_MODEL), lambda i: (layer, geom.cond_row(i), 0, 0)),
            const(ssm_norm_g), const(w_out), const(norm2_g), const(router_w), const(router_b),
        ],
        out_specs=[tile(D_MODEL), tile(D_MODEL), tile(LANES), tile(LANES)],
        out_shape=[jax.ShapeDtypeStruct((n, D_MODEL), f32), jax.ShapeDtypeStruct((n, D_MODEL), f32),
                   jax.ShapeDtypeStruct((n, LANES), jnp.int32), jax.ShapeDtypeStruct((n, LANES), f32)],
        compiler_params=_cparams("arbitrary"),
        name="out_proj",
    )(attn_ctx, attn_lat, conv, y_f, y_b, z, x, mod, ssm_norm_g, w_out, norm2_g, router_w, router_b)


def _moe_kernel(be_ref, nused_ref, x_ref, wg_ref, bg_ref, wu_ref, bu_ref, wd_ref, bd_ref, o_ref,
                wg_s, wu_s, wd_s):
    i = pl.program_id(0)
    prev_e = be_ref[jnp.maximum(i - 1, 0)]
    new_expert = (i == 0) | (be_ref[i] != prev_e)

    @pl.when(new_expert)
    def _():
        wg_s[...] = wg_ref[0].astype(bf16)
        wu_s[...] = wu_ref[0].astype(bf16)
        wd_s[...] = wd_ref[0].astype(bf16)

    @pl.when(i < nused_ref[0])
    def _():
        x = x_ref[...].astype(bf16)
        g = jnp.dot(x, wg_s[...], preferred_element_type=f32) + bg_ref[0]
        u = jnp.dot(x, wu_s[...], preferred_element_type=f32) + bu_ref[0]
        g = jnp.minimum(g, SWIGLU_LIMIT)
        u = jnp.clip(u, -SWIGLU_LIMIT, SWIGLU_LIMIT)
        act = g * jax.nn.sigmoid(SWIGLU_ALPHA * g) * (u + 1.0)
        o_ref[...] = jnp.dot(act.astype(bf16), wd_s[...], preferred_element_type=f32) + bd_ref[0]

    @pl.when(i >= nused_ref[0])
    def _():
        o_ref[...] = jnp.zeros_like(o_ref)


def _moe(layer, xs, block_e, n_used, w_gate, b_gate, w_up, b_up, w_down, b_down):
    n_rows, d = xs.shape
    n_blocks = n_rows // MOE_TILE
    d_ff = w_gate.shape[-1]
    w_spec = lambda r, c: pl.BlockSpec((None, 1, r, c), lambda i, be, nu: (layer, be[i], 0, 0))
    b_spec = lambda c: pl.BlockSpec((None, 1, 1, c), lambda i, be, nu: (layer, be[i], 0, 0))
    row_spec = pl.BlockSpec((MOE_TILE, d), lambda i, be, nu: (i, 0))
    depth, n_e = b_gate.shape[:2]
    return pl.pallas_call(
        _moe_kernel,
        grid_spec=pltpu.PrefetchScalarGridSpec(
            num_scalar_prefetch=2,
            grid=(n_blocks,),
            in_specs=[row_spec, w_spec(d, d_ff), b_spec(d_ff), w_spec(d, d_ff), b_spec(d_ff),
                      w_spec(d_ff, d), b_spec(d)],
            out_specs=row_spec,
            scratch_shapes=[pltpu.VMEM((d, d_ff), bf16), pltpu.VMEM((d, d_ff), bf16), pltpu.VMEM((d_ff, d), bf16)],
        ),
        out_shape=jax.ShapeDtypeStruct((n_rows, d), f32),
        compiler_params=_cparams("arbitrary"),
        name="moe_experts",
    )(block_e, n_used, xs, w_gate, b_gate.reshape(depth, n_e, 1, d_ff), w_up, b_up.reshape(depth, n_e, 1, d_ff),
      w_down, b_down.reshape(depth, n_e, 1, d))


def _route(top_idx, n_blocks):
    n = top_idx.shape[0]
    n_assign = n * TOP_K
    i32 = jnp.int32
    flat_e = top_idx.reshape(-1)
    iota = jnp.arange(n_assign, dtype=i32)
    sorted_e, order = lax.sort((flat_e, iota), num_keys=1, is_stable=True)
    counts = jnp.sum((flat_e[:, None] == jnp.arange(N_EXPERTS, dtype=i32)[None, :]).astype(i32), axis=0)
    grp_start = jnp.cumsum(counts) - counts
    nb = (counts + MOE_TILE - 1) // MOE_TILE
    blk_end = jnp.cumsum(nb)
    blk_start = blk_end - nb
    dest_sorted = blk_start[sorted_e] * MOE_TILE + (iota - grp_start[sorted_e])
    _, dest = lax.sort((order, dest_sorted), num_keys=1)
    block_e = jnp.minimum(jnp.sum((blk_end[None, :] <= jnp.arange(n_blocks, dtype=i32)[:, None]).astype(i32), axis=1),
                          N_EXPERTS - 1)
    n_used = blk_end[-1:].astype(i32)
    row = jnp.arange(n_blocks * MOE_TILE, dtype=i32)
    row_e = block_e[row // MOE_TILE]
    in_grp = row - blk_start[row_e] * MOE_TILE
    src = order[jnp.clip(grp_start[row_e] + in_grp, 0, n_assign - 1)]
    row_tok = jnp.where(in_grp < counts[row_e], src // TOP_K, 0)
    return dest.reshape(n, TOP_K), block_e, n_used, row_tok


GATHER_CHUNK = 2048
GATHER_UNROLL = 8


def _gather_kernel(idx_ref, src_ref, out_ref, sems):
    i = pl.program_id(0)
    n = pl.num_programs(0)
    base = i * GATHER_CHUNK

    def chunk_copy(step, slot):
        return pltpu.make_async_copy(src_ref.at[pl.ds(0, GATHER_CHUNK)],
                                     out_ref.at[pl.ds(step * GATHER_CHUNK, GATHER_CHUNK)], sems.at[slot])

    def issue(r, carry):
        pltpu.make_async_copy(src_ref.at[pl.ds(idx_ref[r], 1)], out_ref.at[pl.ds(base + r, 1)],
                              sems.at[i % 2]).start()
        return carry

    lax.fori_loop(0, GATHER_CHUNK, issue, 0, unroll=GATHER_UNROLL)

    @pl.when(i > 0)
    def _():
        chunk_copy(i - 1, (i + 1) % 2).wait()

    @pl.when(i == n - 1)
    def _():
        chunk_copy(i, i % 2).wait()


def _gather_rows(src, idx):
    n_out = idx.shape[0]
    assert n_out % GATHER_CHUNK == 0
    return pl.pallas_call(
        _gather_kernel,
        grid=(n_out // GATHER_CHUNK,),
        in_specs=[pl.BlockSpec((GATHER_CHUNK,), lambda i: (i,), memory_space=pltpu.SMEM),
                  pl.BlockSpec(memory_space=pl.ANY)],
        out_specs=pl.BlockSpec(memory_space=pl.ANY),
        out_shape=jax.ShapeDtypeStruct((n_out, src.shape[1]), src.dtype),
        scratch_shapes=[pltpu.SemaphoreType.DMA((2,))],
        compiler_params=pltpu.CompilerParams(dimension_semantics=("arbitrary",)),
        name="gather_rows",
    )(idx, src)


def _combine_kernel(final, x_ref, rows_ref, gw_ref, mod_ref, *rest):
    ffn = rows_ref[0] * gw_ref[:, 0:1]
    for k in range(1, TOP_K):
        ffn = ffn + rows_ref[k] * gw_ref[:, k:k + 1]
    x = x_ref[...] + mod_ref[5:6, :] * ffn
    if final:
        fg_ref, y_ref = rest
        y_ref[...] = x * lax.rsqrt(jnp.mean(x * x, axis=-1, keepdims=True) + EPS) * fg_ref[...]
    else:
        rest[0][...] = x


def _combine(geom, layer, x, rows, gate_w, mod, final_g):
    n = geom.n_tok
    final = final_g is not None
    tile = pl.BlockSpec((TOK_TILE, D_MODEL), lambda i: (i, 0))
    in_specs = [
        tile,
        pl.BlockSpec((TOP_K, TOK_TILE, D_MODEL), lambda i: (0, i, 0)),
        pl.BlockSpec((TOK_TILE, LANES), lambda i: (i, 0)),
        pl.BlockSpec((None, None, 6, D_MODEL), lambda i: (layer, geom.cond_row(i), 0, 0)),
    ]
    args = [x, rows, gate_w, mod]
    if final:
        in_specs.append(pl.BlockSpec(final_g.shape, lambda i: (0, 0)))
        args.append(final_g)
    return pl.pallas_call(
        functools.partial(_combine_kernel, final),
        grid=(geom.n_tiles,),
        in_specs=in_specs,
        out_specs=tile,
        out_shape=jax.ShapeDtypeStruct((n, D_MODEL), f32),
        compiler_params=_cparams("arbitrary"),
        name="moe_combine",
    )(*args)


def _pad_lanes(a, width=LANES):
    return jnp.pad(a, [(0, 0)] * (a.ndim - 1) + [(0, width - a.shape[-1])])


def kernel(x_prompt, x_sample, c, cache_k, cache_v, state_ssm, c_ctx, norm1_g, norm2_g, w_ada, b_ada, w_in, w_out, attn_lambda, attn_subln_g, conv_dw_w, conv_dw_b, conv_ln_g, conv_ln_b, ssm_conv_w, ssm_conv_b, ssm_dt_bias, ssm_a_log, ssm_d, ssm_norm_g, router_w, router_b, w_gate, b_gate, w_up, b_up, w_down, b_down, final_g):
    n_ctx_seq, ctx_len, d = x_prompt.shape
    n_lat_seq, lat_len, _ = x_sample.shape
    past_len = cache_k.shape[-2]
    geom = _Geom(n_ctx_seq, ctx_len, n_lat_seq, lat_len)
    n = geom.n_tok

    x = jnp.concatenate([x_prompt.reshape(geom.n_ctx, d), x_sample.reshape(geom.n_lat, d)], axis=0)
    cond = jnp.concatenate([c_ctx[None, :], c, jnp.zeros((COND_ROWS - 1 - n_lat_seq, d), f32)], axis=0)
    mod = _ada(cond, w_ada, b_ada).reshape(DEPTH, COND_ROWS, 6, d)
    tables = _rope_tables(geom)

    n_assign = n * TOP_K
    n_blocks = -(-n_assign // MOE_TILE) + N_EXPERTS

    new_k, new_v, new_s = [], [], []
    y_final = None
    for l in range(DEPTH):
        w_main = w_in[l, :, :N_MAIN].astype(bf16)
        w_dt = _pad_lanes(w_in[l, :, N_MAIN:]).astype(bf16)
        q, k, v, glu, z, xbc, dt = _in_proj(geom, l, x, mod, norm1_g[l][None, :], w_main, w_dt, tables)

        lam_init = 0.8 - 0.6 * math.exp(-0.3 * l)
        subln = jnp.tile(attn_subln_g[l], LANES // ATT_V_DIM)[None, :]
        k_cache = cache_k[:, l].transpose(0, 3, 1, 2, 4).reshape(n_lat_seq, past_len, QK_WIDTH)
        v_cache = cache_v[:, l].transpose(0, 2, 1, 3).reshape(n_lat_seq, past_len, ATT_WIDTH)
        k_all = jnp.concatenate([k[geom.n_ctx:].reshape(n_lat_seq, lat_len, QK_WIDTH), k_cache], axis=1).astype(bf16)
        v_all = jnp.concatenate([v[geom.n_ctx:].reshape(n_lat_seq, lat_len, ATT_WIDTH), v_cache], axis=1).astype(bf16)
        attn_ctx = _attention(lam_init, attn_lambda[l], subln, q, k, v, n_ctx_seq, ctx_len, ctx_len, 0, False)
        attn_lat = _attention(lam_init, attn_lambda[l], subln, q, k_all, v_all, n_lat_seq, lat_len,
                              lat_len + past_len, geom.n_ctx, True)

        conv = _cconv(geom, glu, conv_dw_w[l], conv_dw_b[l][None, :], conv_ln_g[l][None, :], conv_ln_b[l][None, :])

        dt_bias = _pad_lanes(ssm_dt_bias[l].reshape(1, 2 * SSM_HEADS))
        a_log = _pad_lanes(ssm_a_log[l].reshape(1, 2 * SSM_HEADS))
        d_skip = jnp.repeat(ssm_d[l], SSM_HEAD_DIM)[None, :]
        ys, fins = [], []
        for direction in range(2):
            init = _pad_states(state_ssm[:, l, direction])
            y_d, fin = _ssd(geom, direction, xbc, dt, init, ssm_conv_w[l], ssm_conv_b[l][None, :], dt_bias, a_log, d_skip)
            ys.append(y_d)
            fins.append(_unpad_states(fin[:n_ctx_seq]))

        x, h2, top_idx, gate_w = _out_proj(geom, l, attn_ctx, attn_lat, conv, ys[0], ys[1], z, x, mod, ssm_norm_g[l][None, :],
                                  w_out[l].astype(bf16), norm2_g[l][None, :],
                                  _pad_lanes(router_w[l]), _pad_lanes(router_b[l][None, :]))

        dest, block_e, n_used, row_tok = _route(top_idx[:, :TOP_K], n_blocks)
        xs = _gather_rows(h2, row_tok)
        eo = _moe(l, xs, block_e, n_used, w_gate, b_gate, w_up, b_up, w_down, b_down)
        rows = _gather_rows(eo, dest.T.reshape(-1)).reshape(TOP_K, n, d)
        if l == DEPTH - 1:
            y_final = _combine(geom, l, x, rows, gate_w, mod, final_g[None, :])
        else:
            x = _combine(geom, l, x, rows, gate_w, mod, None)

        new_k.append(k[:geom.n_ctx].reshape(n_ctx_seq, ctx_len, ATT_HEADS, 2, ATT_QK_DIM).transpose(0, 2, 3, 1, 4))
        new_v.append(v[:geom.n_ctx].reshape(n_ctx_seq, ctx_len, ATT_HEADS, ATT_V_DIM).transpose(0, 2, 1, 3))
        new_s.append(jnp.stack(fins, axis=1))

    y_prompt = y_final[:geom.n_ctx].reshape(n_ctx_seq, ctx_len, d)
    y_sample = y_final[geom.n_ctx:].reshape(n_lat_seq, lat_len, d)
    return (y_prompt, y_sample, jnp.stack(new_k, axis=1), jnp.stack(new_v, axis=1), jnp.stack(new_s, axis=1))
```

```python
import functools
import math

import numpy as np
import jax
import jax.numpy as jnp
from jax import lax
from jax.experimental import pallas as pl
from jax.experimental.pallas import tpu as pltpu

f32 = jnp.float32
bf16 = jnp.bfloat16
HIGHEST = lax.Precision.HIGHEST

D_MODEL = 1024
DEPTH = 2
GRID_W = 64
ATT_HEADS = 4
ATT_QK_DIM = 32
ATT_V_DIM = 64
ATT_WIDTH = 256
QK_WIDTH = 256
ROPE_AXIS_DIM = 16
ROPE_BASE = 10000.0
CONV_CH = 256
CONV_K = 31
SSM_WIDTH = 512
SSM_HEAD_DIM = 64
SSM_HEADS = 8
SSM_GROUPS = 2
SSM_STATE = 64
SSM_CONV_K = 5
SSM_XBC = 768
N_MAIN = 2560
N_EXPERTS = 32
TOP_K = 4
SWIGLU_LIMIT = 7.0
SWIGLU_ALPHA = 1.702
EPS = 1e-6

LANES = 128
SUBLANES = 8
TOK_TILE = 256
SSD_CHUNK = 128
CONV_HALO = 16
SSM_HALO = 8
MOE_TILE = 512
MOE_FF_CHUNK = 256
COND_ROWS = 16
VMEM_LIMIT = 56 * 1024 * 1024


def _cparams(*sem):
    return pltpu.CompilerParams(dimension_semantics=sem, vmem_limit_bytes=VMEM_LIMIT)


def _silu(x):
    return x * jax.nn.sigmoid(x)


def _ada_kernel(cond_ref, w_ref, b_ref, o_ref):
    s = _silu(cond_ref[...]).astype(bf16)
    o_ref[0] = jnp.dot(s, w_ref[0].astype(bf16), preferred_element_type=f32) + b_ref[0]


def _ada(cond, w_ada, b_ada):
    depth, d, n6 = w_ada.shape
    tn = 1536
    return pl.pallas_call(
        _ada_kernel,
        grid=(depth, n6 // tn),
        in_specs=[
            pl.BlockSpec((COND_ROWS, d), lambda l, j: (0, 0)),
            pl.BlockSpec((1, d, tn), lambda l, j: (l, 0, j)),
            pl.BlockSpec((1, 1, tn), lambda l, j: (l, 0, j)),
        ],
        out_specs=pl.BlockSpec((1, COND_ROWS, tn), lambda l, j: (l, 0, j)),
        out_shape=jax.ShapeDtypeStruct((depth, COND_ROWS, n6), f32),
        compiler_params=_cparams("arbitrary", "arbitrary"),
        name="ada",
    )(cond, w_ada, b_ada.reshape(depth, 1, n6))


class _Geom:
    def __init__(self, n_ctx_seq, ctx_len, n_lat_seq, lat_len):
        self.n_ctx_seq, self.ctx_len, self.n_lat_seq, self.lat_len = n_ctx_seq, ctx_len, n_lat_seq, lat_len
        self.n_ctx = n_ctx_seq * ctx_len
        self.n_lat = n_lat_seq * lat_len
        self.n_tok = self.n_ctx + self.n_lat
        assert ctx_len % TOK_TILE == 0 and lat_len % TOK_TILE == 0
        self.ctx_tiles = self.n_ctx // TOK_TILE
        self.tiles_per_ctx = ctx_len // TOK_TILE
        self.tiles_per_lat = lat_len // TOK_TILE
        self.n_tiles = self.n_tok // TOK_TILE

    def cond_row(self, i):
        return jnp.where(i < self.ctx_tiles, 0, 1 + (i - self.ctx_tiles) // self.tiles_per_lat)

    def seq_pos(self, i):
        j_ctx = i % self.tiles_per_ctx
        j_lat = (i - self.ctx_tiles) % self.tiles_per_lat
        is_ctx = i < self.ctx_tiles
        first = jnp.where(is_ctx, j_ctx == 0, j_lat == 0)
        last = jnp.where(is_ctx, j_ctx == self.tiles_per_ctx - 1, j_lat == self.tiles_per_lat - 1)
        return first, last

    def seq_index(self, i):
        return jnp.where(i < self.ctx_tiles, i // self.tiles_per_ctx,
                         self.n_ctx_seq + (i - self.ctx_tiles) // self.tiles_per_lat)


def _rope_tables(geom):
    t = np.arange(geom.lat_len)
    row = (t // GRID_W).astype(np.float64)
    col = (t % GRID_W).astype(np.float64)
    inv_freq = 1.0 / (ROPE_BASE ** (np.arange(0, ROPE_AXIS_DIM, 2, dtype=np.float64) / ROPE_AXIS_DIM))
    ang_r = row[:, None] * inv_freq[None, :]
    ang_c = col[:, None] * inv_freq[None, :]
    ang = np.concatenate([ang_r, ang_r, ang_c, ang_c], axis=-1)
    cos, sin = np.cos(ang), np.sin(ang)
    half = ROPE_AXIS_DIM // 2
    lane = np.arange(ATT_QK_DIM)
    first_half = (lane % ROPE_AXIS_DIM) < half
    sin_a = np.where(first_half[None, :], -sin, 0.0)
    sin_b = np.where(first_half[None, :], 0.0, sin)
    rep = LANES // ATT_QK_DIM

    def full(tab, ident):
        tab = np.tile(tab, (1, rep))
        return jnp.asarray(np.concatenate([np.full((TOK_TILE, LANES), ident), tab], axis=0), f32)

    return full(cos, 1.0), full(sin_a, 0.0), full(sin_b, 0.0)


def _in_proj_kernel(x_ref, mod_ref, g_ref, w_ref, wdt_ref, cos_ref, sa_ref, sb_ref,
                    q_ref, k_ref, v_ref, glu_ref, z_ref, xbc_ref, dt_ref):
    x = x_ref[...]
    y = x * lax.rsqrt(jnp.mean(x * x, axis=-1, keepdims=True) + EPS) * g_ref[...]
    h = (y * (1.0 + mod_ref[1:2, :]) + mod_ref[0:1, :]).astype(bf16)
    p = jnp.dot(h, w_ref[...], preferred_element_type=f32)
    dt_ref[...] = jnp.dot(h, wdt_ref[...], preferred_element_type=f32)
    cos, sa, sb = cos_ref[...], sa_ref[...], sb_ref[...]
    half = ROPE_AXIS_DIM // 2

    def rope(u):
        return u * cos + pltpu.roll(u, LANES - half, 1) * sa + pltpu.roll(u, half, 1) * sb

    scale = ATT_QK_DIM ** -0.5
    for j in range(QK_WIDTH // LANES):
        lo = j * LANES
        q_ref[:, lo:lo + LANES] = (rope(p[:, lo:lo + LANES]) * scale).astype(bf16)
        k_ref[:, lo:lo + LANES] = rope(p[:, QK_WIDTH + lo:QK_WIDTH + lo + LANES])
    o = 2 * QK_WIDTH
    v_ref[...] = p[:, o:o + ATT_WIDTH]
    o += ATT_WIDTH
    glu_ref[...] = p[:, o:o + 2 * CONV_CH]
    o += 2 * CONV_CH
    z_ref[...] = p[:, o:o + SSM_WIDTH]
    o += SSM_WIDTH
    xbc_ref[...] = p[:, o:o + SSM_XBC]


def _in_proj(geom, layer, x, mod, norm_g, w_main, w_dt, tables):
    n = geom.n_tok
    tile = lambda w: pl.BlockSpec((TOK_TILE, w), lambda i: (i, 0))
    const = lambda a: pl.BlockSpec(a.shape, lambda i: (0,) * a.ndim)
    tab_spec = pl.BlockSpec(
        (TOK_TILE, LANES),
        lambda i: (jnp.where(i < geom.ctx_tiles, 0, 1 + (i - geom.ctx_tiles) % geom.tiles_per_lat), 0))
    widths = (QK_WIDTH, QK_WIDTH, ATT_WIDTH, 2 * CONV_CH, SSM_WIDTH, SSM_XBC, LANES)
    dtypes = (bf16, f32, f32, f32, f32, f32, f32)
    return pl.pallas_call(
        _in_proj_kernel,
        grid=(geom.n_tiles,),
        in_specs=[
            tile(D_MODEL),
            pl.BlockSpec((None, None, 6, D_MODEL), lambda i: (layer, geom.cond_row(i), 0, 0)),
            const(norm_g), const(w_main), const(w_dt), tab_spec, tab_spec, tab_spec,
        ],
        out_specs=[tile(w) for w in widths],
        out_shape=[jax.ShapeDtypeStruct((n, w), dt) for w, dt in zip(widths, dtypes)],
        compiler_params=_cparams("arbitrary"),
        name="in_proj",
    )(x, mod, norm_g, w_main, w_dt, *tables)


def _attn_kernel(lam_init, al_ref, g_ref, q_ref, k_ref, v_ref, o_ref):
    al = al_ref[...]
    lam = (jnp.exp(jnp.sum(al[0:1] * al[1:2], axis=-1, keepdims=True))
           - jnp.exp(jnp.sum(al[2:3] * al[3:4], axis=-1, keepdims=True)) + lam_init)
    lane = lax.broadcasted_iota(jnp.int32, (1, LANES), 1)
    heads_per_slab = LANES // ATT_V_DIM
    for j in range(ATT_WIDTH // LANES):
        sl = slice(j * LANES, (j + 1) * LANES)
        qs = q_ref[:, sl]
        ks = k_ref[:, sl].astype(bf16)
        vs = v_ref[:, sl].astype(bf16)
        a_heads = []
        for hl in range(heads_per_slab):
            comps = []
            for c in range(2):
                lo = hl * ATT_V_DIM + c * ATT_QK_DIM
                qm = jnp.where((lane >= lo) & (lane < lo + ATT_QK_DIM), qs, jnp.zeros_like(qs))
                s = lax.dot_general(qm, ks, (((1,), (1,)), ((), ())), preferred_element_type=f32)
                e = jnp.exp(s - jnp.max(s, axis=-1, keepdims=True))
                l = jnp.sum(e, axis=-1, keepdims=True)
                comps.append(jnp.dot(e.astype(bf16), vs, preferred_element_type=f32) / l)
            a_heads.append(comps[0] - lam * comps[1])
        a = jnp.where(lane < ATT_V_DIM, a_heads[0], a_heads[1])
        a2 = a * a
        ms0 = jnp.sum(jnp.where(lane < ATT_V_DIM, a2, 0.0), axis=-1, keepdims=True)
        ms1 = jnp.sum(jnp.where(lane < ATT_V_DIM, 0.0, a2), axis=-1, keepdims=True)
        ms = jnp.where(lane < ATT_V_DIM, ms0, ms1) * (1.0 / ATT_V_DIM)
        o_ref[:, sl] = a * lax.rsqrt(ms + EPS) * g_ref[...] * (1.0 - lam_init)


def _attention(lam_init, attn_lambda, subln_g, q, k, v, n_seq, q_len, k_len, q_row0, kv_is_3d):
    tq = TOK_TILE
    qb = q_len // tq
    q0 = q_row0 // tq
    if kv_is_3d:
        kv_spec = pl.BlockSpec((None, k_len, ATT_WIDTH), lambda b, i: (b, 0, 0))
    else:
        kv_spec = pl.BlockSpec((k_len, ATT_WIDTH), lambda b, i: (q_row0 // k_len + b, 0))
    return pl.pallas_call(
        functools.partial(_attn_kernel, lam_init),
        grid=(n_seq, qb),
        in_specs=[
            pl.BlockSpec(attn_lambda.shape, lambda b, i: (0, 0)),
            pl.BlockSpec(subln_g.shape, lambda b, i: (0, 0)),
            pl.BlockSpec((tq, ATT_WIDTH), lambda b, i: (q0 + b * qb + i, 0)),
            kv_spec, kv_spec,
        ],
        out_specs=pl.BlockSpec((tq, ATT_WIDTH), lambda b, i: (b * qb + i, 0)),
        out_shape=jax.ShapeDtypeStruct((n_seq * q_len, ATT_WIDTH), f32),
        compiler_params=_cparams("arbitrary", "arbitrary"),
        name="attention",
    )(attn_lambda, subln_g, q, k, v)


def _halo_specs(geom, width, halo):
    per = TOK_TILE // halo
    n_halo = geom.n_tok // halo
    prev = pl.BlockSpec((halo, width), lambda i: (jnp.maximum(i * per - 1, 0), 0))
    nxt = pl.BlockSpec((halo, width), lambda i: (jnp.minimum((i + 1) * per, n_halo - 1), 0))
    return prev, nxt


def _cconv_kernel(geom, cur_ref, prev_ref, next_ref, w_ref, b_ref, lg_ref, lb_ref, o_ref, pad_ref):
    first, last = geom.seq_pos(pl.program_id(0))

    def glu(ref):
        return ref[:, :CONV_CH] * jax.nn.sigmoid(ref[:, CONV_CH:])

    pad_ref[0:CONV_HALO, :] = jnp.where(first, 0.0, glu(prev_ref))
    pad_ref[CONV_HALO:CONV_HALO + TOK_TILE, :] = glu(cur_ref)
    pad_ref[CONV_HALO + TOK_TILE:, :] = jnp.where(last, 0.0, glu(next_ref))
    rows = 64
    base = CONV_HALO - CONV_K // 2
    for r0 in range(0, TOK_TILE, rows):
        acc = jnp.broadcast_to(b_ref[...], (rows, CONV_CH))
        for k in range(CONV_K):
            acc = acc + w_ref[k:k + 1, :] * pad_ref[r0 + base + k:r0 + base + k + rows, :]
        xc = acc - jnp.mean(acc, axis=-1, keepdims=True)
        var = jnp.mean(xc * xc, axis=-1, keepdims=True)
        o_ref[r0:r0 + rows, :] = _silu(xc * lax.rsqrt(var + EPS) * lg_ref[...] + lb_ref[...])


def _cconv(geom, glu, w, b, ln_g, ln_b):
    prev, nxt = _halo_specs(geom, 2 * CONV_CH, CONV_HALO)
    const = lambda a: pl.BlockSpec(a.shape, lambda i: (0,) * a.ndim)
    return pl.pallas_call(
        functools.partial(_cconv_kernel, geom),
        grid=(geom.n_tiles,),
        in_specs=[pl.BlockSpec((TOK_TILE, 2 * CONV_CH), lambda i: (i, 0)), prev, nxt,
                  const(w), const(b), const(ln_g), const(ln_b)],
        out_specs=pl.BlockSpec((TOK_TILE, CONV_CH), lambda i: (i, 0)),
        out_shape=jax.ShapeDtypeStruct((geom.n_tok, CONV_CH), f32),
        scratch_shapes=[pltpu.VMEM((TOK_TILE + 2 * CONV_HALO, CONV_CH), f32)],
        compiler_params=_cparams("arbitrary"),
        name="conformer_conv",
    )(glu, glu, glu, w, b, ln_g, ln_b)


def _softplus(x):
    return jnp.maximum(x, 0.0) + jnp.log1p(jnp.exp(-jnp.abs(x)))


def _ssd_kernel(geom, direction, cur_ref, prev_ref, next_ref, dt_ref, init_ref, cw_ref, cb_ref, dtb_ref,
                alog_ref, dskip_ref, y_ref, fin_ref, pad_ref, xc_ref, st_ref):
    fwd = direction == 0
    i = pl.program_id(0)
    t = i if fwd else geom.n_tiles - 1 - i
    first, last = geom.seq_pos(t)
    enter, leave = (first, last) if fwd else (last, first)
    is_lat = t >= geom.ctx_tiles

    @pl.when(enter)
    def _():
        st_ref[...] = jnp.where(is_lat, init_ref[...], 0.0)

    pad_ref[0:SSM_HALO, :] = jnp.where(first, 0.0, prev_ref[...])
    pad_ref[SSM_HALO:SSM_HALO + TOK_TILE, :] = cur_ref[...]
    pad_ref[SSM_HALO + TOK_TILE:, :] = jnp.where(last, 0.0, next_ref[...])
    base = SSM_HALO - SSM_CONV_K // 2
    acc = jnp.broadcast_to(cb_ref[...], (TOK_TILE, SSM_XBC))
    for k in range(SSM_CONV_K):
        acc = acc + cw_ref[k:k + 1, :] * pad_ref[base + k:base + k + TOK_TILE, :]
    xc_ref[...] = _silu(acc)

    L = SSD_CHUNK
    a_row = -jnp.exp(alog_ref[...])
    ri = lax.broadcasted_iota(jnp.int32, (L, L), 0)
    ci = lax.broadcasted_iota(jnp.int32, (L, L), 1)
    keep = (ci <= ri) if fwd else (ci >= ri)
    tri_col = keep.astype(f32)
    tri_row = ((ri <= ci) if fwd else (ri >= ci)).astype(f32)
    lane = lax.broadcasted_iota(jnp.int32, (1, LANES), 1)
    er = lax.broadcasted_iota(jnp.int32, (LANES, SSM_WIDTH), 0)
    ec = lax.broadcasted_iota(jnp.int32, (LANES, SSM_WIDTH), 1)
    expand = (er == direction * SSM_HEADS + ec // SSM_HEAD_DIM).astype(f32)
    heads_per_group = SSM_HEADS // SSM_GROUPS
    heads_per_slab = LANES // SSM_HEAD_DIM
    tot_at = L - 1 if fwd else 0

    chunks = range(TOK_TILE // L)
    for cidx in (chunks if fwd else reversed(chunks)):
        r0 = cidx * L
        xs = xc_ref[r0:r0 + L, 0:SSM_WIDTH]
        bm = xc_ref[r0:r0 + L, SSM_WIDTH:SSM_WIDTH + LANES]
        cm = xc_ref[r0:r0 + L, SSM_WIDTH + LANES:SSM_XBC]
        dt = _softplus(dt_ref[r0:r0 + L, :] + dtb_ref[...])
        da = dt * a_row
        acum_col = jnp.dot(tri_col, da, precision=HIGHEST, preferred_element_type=f32)
        acum_row = jnp.dot(da.T, tri_row, precision=HIGHEST, preferred_element_type=f32)
        tot_row = acum_col[tot_at:tot_at + 1, :]
        tot_col = acum_row[:, tot_at:tot_at + 1]
        decay_out = jnp.exp(tot_col - acum_row)
        decay_in = jnp.exp(acum_col)
        chunk_decay = jnp.exp(tot_row)
        xdt = (xs * jnp.dot(dt, expand, precision=HIGHEST, preferred_element_type=f32)).astype(bf16)
        bt = bm.T
        bb = bm.astype(bf16)
        ys = []
        for g in range(SSM_GROUPS):
            in_group = (lane >= g * SSM_STATE) & (lane < (g + 1) * SSM_STATE)
            cg = jnp.where(in_group, cm, 0.0)
            gmat = lax.dot_general(cg.astype(bf16), bb, (((1,), (1,)), ((), ())), preferred_element_type=f32)
            for hh in range(heads_per_group):
                h = g * heads_per_group + hh
                hl = direction * SSM_HEADS + h
                slab = xdt[:, (h // heads_per_slab) * LANES:(h // heads_per_slab + 1) * LANES]
                seg = acum_col[:, hl:hl + 1] - acum_row[hl:hl + 1, :]
                m = (gmat * jnp.exp(jnp.where(keep, seg, -jnp.inf))).astype(bf16)
                y_h = jnp.dot(m, slab, preferred_element_type=f32)
                c_in = (cg * decay_in[:, hl:hl + 1]).astype(bf16)
                state = st_ref[h]
                y_h = y_h + jnp.dot(c_in, state.astype(bf16), preferred_element_type=f32)
                b_out = (bt * decay_out[hl:hl + 1, :]).astype(bf16)
                st_ref[h] = chunk_decay[:, hl:hl + 1] * state + jnp.dot(b_out, slab, preferred_element_type=f32)
                ys.append(y_h)
        for j in range(SSM_WIDTH // LANES):
            y_slab = jnp.where(lane < SSM_HEAD_DIM, ys[heads_per_slab * j], ys[heads_per_slab * j + 1])
            if fwd:
                y_slab = y_slab + dskip_ref[:, j * LANES:(j + 1) * LANES] * xs[:, j * LANES:(j + 1) * LANES]
            y_ref[r0:r0 + L, j * LANES:(j + 1) * LANES] = y_slab

    @pl.when(leave)
    def _():
        fin_ref[...] = st_ref[...]


def _ssd(geom, direction, xbc, dt, init, conv_w, conv_b, dt_bias, a_log, d_skip):
    nt = geom.n_tiles
    tidx = (lambda i: i) if direction == 0 else (lambda i: nt - 1 - i)
    per = TOK_TILE // SSM_HALO
    n_halo = geom.n_tok // SSM_HALO
    n_seq = geom.n_ctx_seq + geom.n_lat_seq
    const = lambda a: pl.BlockSpec(a.shape, lambda i: (0,) * a.ndim)
    st_block = (None, SSM_HEADS, LANES, LANES)
    y, fin = pl.pallas_call(
        functools.partial(_ssd_kernel, geom, direction),
        grid=(nt,),
        in_specs=[
            pl.BlockSpec((TOK_TILE, SSM_XBC), lambda i: (tidx(i), 0)),
            pl.BlockSpec((SSM_HALO, SSM_XBC), lambda i: (jnp.maximum(tidx(i) * per - 1, 0), 0)),
            pl.BlockSpec((SSM_HALO, SSM_XBC), lambda i: (jnp.minimum((tidx(i) + 1) * per, n_halo - 1), 0)),
            pl.BlockSpec((TOK_TILE, LANES), lambda i: (tidx(i), 0)),
            pl.BlockSpec(st_block, lambda i: (jnp.maximum(geom.seq_index(tidx(i)) - geom.n_ctx_seq, 0), 0, 0, 0)),
            const(conv_w), const(conv_b), const(dt_bias), const(a_log), const(d_skip),
        ],
        out_specs=[
            pl.BlockSpec((TOK_TILE, SSM_WIDTH), lambda i: (tidx(i), 0)),
            pl.BlockSpec(st_block, lambda i: (geom.seq_index(tidx(i)), 0, 0, 0)),
        ],
        out_shape=[
            jax.ShapeDtypeStruct((geom.n_tok, SSM_WIDTH), f32),
            jax.ShapeDtypeStruct((n_seq, SSM_HEADS, LANES, LANES), f32),
        ],
        scratch_shapes=[
            pltpu.VMEM((TOK_TILE + 2 * SSM_HALO, SSM_XBC), f32),
            pltpu.VMEM((TOK_TILE, SSM_XBC), f32),
            pltpu.VMEM((SSM_HEADS, LANES, LANES), f32),
        ],
        compiler_params=_cparams("arbitrary"),
        name="ssd_scan",
    )(xbc, xbc, xbc, dt, init, conv_w, conv_b, dt_bias, a_log, d_skip)
    return y, fin


def _pad_states(s):
    return jnp.tile(jnp.swapaxes(s, -1, -2), (1, 1, LANES // SSM_STATE, LANES // SSM_HEAD_DIM))


def _unpad_states(fin):
    hpg = SSM_HEADS // SSM_GROUPS
    outs = []
    for h in range(SSM_HEADS):
        g, side = h // hpg, h % (LANES // SSM_HEAD_DIM)
        q = fin[:, h, g * SSM_STATE:(g + 1) * SSM_STATE, side * SSM_HEAD_DIM:(side + 1) * SSM_HEAD_DIM]
        outs.append(jnp.swapaxes(q, -1, -2))
    return jnp.stack(outs, axis=1)


def _out_proj_kernel(geom, actx_ref, alat_ref, conv_ref, yf_ref, yb_ref, z_ref, x_ref, mod_ref, sg_ref, w_ref,
                     g2_ref, rw_ref, rb_ref, xo_ref, h2_ref, ti_ref, gw_ref):
    attn = jnp.where(pl.program_id(0) < geom.ctx_tiles, actx_ref[...], alat_ref[...])
    gated = (yf_ref[...] + yb_ref[...]) * _silu(z_ref[...])
    ssm = gated * lax.rsqrt(jnp.mean(gated * gated, axis=-1, keepdims=True) + EPS) * sg_ref[...]
    o1 = ATT_WIDTH
    o2 = ATT_WIDTH + CONV_CH
    mix = (jnp.dot(attn.astype(bf16), w_ref[0:o1, :], preferred_element_type=f32)
           + jnp.dot(conv_ref[...].astype(bf16), w_ref[o1:o2, :], preferred_element_type=f32)
           + jnp.dot(ssm.astype(bf16), w_ref[o2:, :], preferred_element_type=f32))
    x = x_ref[...] + mod_ref[2:3, :] * mix
    xo_ref[...] = x
    y = x * lax.rsqrt(jnp.mean(x * x, axis=-1, keepdims=True) + EPS) * g2_ref[...]
    h2 = y * (1.0 + mod_ref[4:5, :]) + mod_ref[3:4, :]
    h2_ref[...] = h2
    logits = jnp.dot(h2, rw_ref[...], precision=HIGHEST, preferred_element_type=f32) + rb_ref[...]
    lane = lax.broadcasted_iota(jnp.int32, (1, LANES), 1)
    lane_f = lane.astype(f32)
    lg = jnp.where(lane < N_EXPERTS, logits, -jnp.inf)
    idx_out = jnp.zeros(logits.shape, jnp.int32)
    gate_out = jnp.zeros(logits.shape, f32)
    top0 = None
    denom = None
    for k in range(TOP_K):
        m = jnp.max(lg, axis=-1, keepdims=True)
        idx = jnp.min(jnp.where(lg == m, lane_f, float(LANES)), axis=-1, keepdims=True).astype(jnp.int32)
        lg = jnp.where(lane == idx, -jnp.inf, lg)
        if k == 0:
            top0 = m
        e = jnp.exp(m - top0)
        denom = e if k == 0 else denom + e
        idx_out = jnp.where(lane == k, idx, idx_out)
        gate_out = jnp.where(lane == k, e, gate_out)
    ti_ref[...] = idx_out
    gw_ref[...] = gate_out / denom


def _out_proj(geom, layer, attn_ctx, attn_lat, conv, y_f, y_b, z, x, mod, ssm_norm_g, w_out, norm2_g, router_w,
              router_b):
    n = geom.n_tok
    tile = lambda w: pl.BlockSpec((TOK_TILE, w), lambda i: (i, 0))
    const = lambda a: pl.BlockSpec(a.shape, lambda i: (0,) * a.ndim)
    nct = geom.ctx_tiles
    return pl.pallas_call(
        functools.partial(_out_proj_kernel, geom),
        grid=(geom.n_tiles,),
        in_specs=[
            pl.BlockSpec((TOK_TILE, ATT_WIDTH), lambda i: (jnp.minimum(i, nct - 1), 0)),
            pl.BlockSpec((TOK_TILE, ATT_WIDTH), lambda i: (jnp.maximum(i - nct, 0), 0)),
            tile(CONV_CH), tile(SSM_WIDTH), tile(SSM_WIDTH), tile(SSM_WIDTH), tile(D_MODEL),
            pl.BlockSpec((None, None, 6, D_MODEL), lambda i: (layer, geom.cond_row(i), 0, 0)),
            const(ssm_norm_g), const(w_out), const(norm2_g), const(router_w), const(router_b),
        ],
        out_specs=[tile(D_MODEL), tile(D_MODEL), tile(LANES), tile(LANES)],
        out_shape=[jax.ShapeDtypeStruct((n, D_MODEL), f32), jax.ShapeDtypeStruct((n, D_MODEL), f32),
                   jax.ShapeDtypeStruct((n, LANES), jnp.int32), jax.ShapeDtypeStruct((n, LANES), f32)],
        compiler_params=_cparams("arbitrary"),
        name="out_proj",
    )(attn_ctx, attn_lat, conv, y_f, y_b, z, x, mod, ssm_norm_g, w_out, norm2_g, router_w, router_b)


def _moe_kernel(be_ref, nused_ref, rs_ref, x_ref, wg_ref, bg_ref, wu_ref, bu_ref, wd_ref, bd_ref, o_ref,
                wg_s, wu_s, wd_s):
    i = pl.program_id(0)
    prev_e = be_ref[jnp.maximum(i - 1, 0)]
    new_expert = (i == 0) | (be_ref[i] != prev_e)

    @pl.when(new_expert)
    def _():
        wg_s[...] = wg_ref[0].astype(bf16)
        wu_s[...] = wu_ref[0].astype(bf16)
        wd_s[...] = wd_ref[0].astype(bf16)

    @pl.when(i < nused_ref[0])
    def _():
        x = x_ref[...].astype(bf16)
        acc = None
        for c0 in range(0, wg_s.shape[1], MOE_FF_CHUNK):
            sl = slice(c0, c0 + MOE_FF_CHUNK)
            g = jnp.dot(x, wg_s[:, sl], preferred_element_type=f32) + bg_ref[0, :, sl]
            u = jnp.dot(x, wu_s[:, sl], preferred_element_type=f32) + bu_ref[0, :, sl]
            g = jnp.minimum(g, SWIGLU_LIMIT)
            u = jnp.clip(u, -SWIGLU_LIMIT, SWIGLU_LIMIT)
            act = g * jax.nn.sigmoid(SWIGLU_ALPHA * g) * (u + 1.0)
            part = jnp.dot(act.astype(bf16), wd_s[sl, :], preferred_element_type=f32)
            acc = part if acc is None else acc + part
        o_ref[...] = acc + bd_ref[0]

    @pl.when(i >= nused_ref[0])
    def _():
        o_ref[...] = jnp.zeros_like(o_ref)


def _moe(layer, xs, block_e, n_used, row_start, w_gate, b_gate, w_up, b_up, w_down, b_down):
    d = xs.shape[1]
    n_blocks = block_e.shape[0]
    d_ff = w_gate.shape[-1]
    w_spec = lambda r, c: pl.BlockSpec((None, 1, r, c), lambda i, be, nu, rs: (layer, be[i], 0, 0))
    b_spec = lambda c: pl.BlockSpec((None, 1, 1, c), lambda i, be, nu, rs: (layer, be[i], 0, 0))
    depth, n_e = b_gate.shape[:2]
    return pl.pallas_call(
        _moe_kernel,
        grid_spec=pltpu.PrefetchScalarGridSpec(
            num_scalar_prefetch=3,
            grid=(n_blocks,),
            in_specs=[pl.BlockSpec((pl.Element(MOE_TILE), pl.Element(d)), lambda i, be, nu, rs: (rs[i] * SUBLANES, 0)),
                      w_spec(d, d_ff), b_spec(d_ff), w_spec(d, d_ff), b_spec(d_ff), w_spec(d_ff, d), b_spec(d)],
            out_specs=pl.BlockSpec((MOE_TILE, d), lambda i, be, nu, rs: (i, 0)),
            scratch_shapes=[pltpu.VMEM((d, d_ff), bf16), pltpu.VMEM((d, d_ff), bf16), pltpu.VMEM((d_ff, d), bf16)],
        ),
        out_shape=jax.ShapeDtypeStruct((n_blocks * MOE_TILE, d), f32),
        compiler_params=_cparams("arbitrary"),
        name="moe_experts",
    )(block_e, n_used, row_start, xs, w_gate, b_gate.reshape(depth, n_e, 1, d_ff), w_up,
      b_up.reshape(depth, n_e, 1, d_ff), w_down, b_down.reshape(depth, n_e, 1, d))


def _expert_table(expert_ids, table):
    onehot = expert_ids[:, None] == jnp.arange(N_EXPERTS, dtype=jnp.int32)[None, :]
    return jnp.sum(jnp.where(onehot, table[None, :], 0), axis=1)


def _route(top_idx, n_blocks):
    n = top_idx.shape[0]
    n_assign = n * TOP_K
    i32 = jnp.int32
    flat_e = top_idx.reshape(-1)
    experts = jnp.arange(N_EXPERTS, dtype=i32)
    counts = jnp.sum((flat_e[:, None] == experts[None, :]).astype(i32), axis=0)
    fill = SUBLANES - 1
    grp_size = (counts + fill) // SUBLANES * SUBLANES
    j = jnp.arange(fill, dtype=i32)
    filler_key = jnp.where(j[None, :] < (grp_size - counts)[:, None], 2 * experts[:, None] + 1, 2 * N_EXPERTS)
    keys = jnp.concatenate([2 * flat_e, filler_key.reshape(-1)])
    n_sorted = n_assign + N_EXPERTS * fill
    ids = jnp.arange(n_sorted, dtype=i32)
    sorted_key, order = lax.sort((keys, ids), num_keys=1, is_stable=True)
    grp_start = jnp.cumsum(grp_size) - grp_size
    nb = (counts + MOE_TILE - 1) // MOE_TILE
    blk_end = jnp.cumsum(nb)
    blk_start = blk_end - nb
    pad_shift = blk_start * MOE_TILE - grp_start
    dest_sorted = ids + _expert_table(jnp.minimum(sorted_key // 2, N_EXPERTS - 1), pad_shift)
    _, dest = lax.sort((order, dest_sorted), num_keys=1)
    dest = dest[:n_assign]
    blk = jnp.arange(n_blocks, dtype=i32)
    block_e = jnp.minimum(jnp.sum((blk_end[None, :] <= blk[:, None]).astype(i32), axis=1), N_EXPERTS - 1)
    n_used = blk_end[-1:].astype(i32)
    row_start = jnp.where(blk < n_used[0], blk * MOE_TILE - _expert_table(block_e, pad_shift), 0)
    src_tok = jnp.concatenate([jnp.where(order < n_assign, order // TOP_K, 0), jnp.zeros((MOE_TILE,), i32)])
    return src_tok, dest.reshape(n, TOP_K), block_e, n_used, row_start // SUBLANES


def _combine_kernel(final, x_ref, rows_ref, gw_ref, mod_ref, *rest):
    ffn = rows_ref[0] * gw_ref[:, 0:1]
    for k in range(1, TOP_K):
        ffn = ffn + rows_ref[k] * gw_ref[:, k:k + 1]
    x = x_ref[...] + mod_ref[5:6, :] * ffn
    if final:
        fg_ref, y_ref = rest
        y_ref[...] = x * lax.rsqrt(jnp.mean(x * x, axis=-1, keepdims=True) + EPS) * fg_ref[...]
    else:
        rest[0][...] = x


def _combine(geom, layer, x, rows, gate_w, mod, final_g):
    n = geom.n_tok
    final = final_g is not None
    tile = pl.BlockSpec((TOK_TILE, D_MODEL), lambda i: (i, 0))
    in_specs = [
        tile,
        pl.BlockSpec((TOP_K, TOK_TILE, D_MODEL), lambda i: (0, i, 0)),
        pl.BlockSpec((TOK_TILE, LANES), lambda i: (i, 0)),
        pl.BlockSpec((None, None, 6, D_MODEL), lambda i: (layer, geom.cond_row(i), 0, 0)),
    ]
    args = [x, rows, gate_w, mod]
    if final:
        in_specs.append(pl.BlockSpec(final_g.shape, lambda i: (0, 0)))
        args.append(final_g)
    return pl.pallas_call(
        functools.partial(_combine_kernel, final),
        grid=(geom.n_tiles,),
        in_specs=in_specs,
        out_specs=tile,
        out_shape=jax.ShapeDtypeStruct((n, D_MODEL), f32),
        compiler_params=_cparams("arbitrary"),
        name="moe_combine",
    )(*args)


def _pad_lanes(a, width=LANES):
    return jnp.pad(a, [(0, 0)] * (a.ndim - 1) + [(0, width - a.shape[-1])])


def kernel(x_prompt, x_sample, c, cache_k, cache_v, state_ssm, c_ctx, norm1_g, norm2_g, w_ada, b_ada, w_in, w_out, attn_lambda, attn_subln_g, conv_dw_w, conv_dw_b, conv_ln_g, conv_ln_b, ssm_conv_w, ssm_conv_b, ssm_dt_bias, ssm_a_log, ssm_d, ssm_norm_g, router_w, router_b, w_gate, b_gate, w_up, b_up, w_down, b_down, final_g):
    n_ctx_seq, ctx_len, d = x_prompt.shape
    n_lat_seq, lat_len, _ = x_sample.shape
    past_len = cache_k.shape[-2]
    geom = _Geom(n_ctx_seq, ctx_len, n_lat_seq, lat_len)
    n = geom.n_tok

    x = jnp.concatenate([x_prompt.reshape(geom.n_ctx, d), x_sample.reshape(geom.n_lat, d)], axis=0)
    cond = jnp.concatenate([c_ctx[None, :], c, jnp.zeros((COND_ROWS - 1 - n_lat_seq, d), f32)], axis=0)
    mod = _ada(cond, w_ada, b_ada).reshape(DEPTH, COND_ROWS, 6, d)
    tables = _rope_tables(geom)

    n_assign = n * TOP_K
    n_blocks = -(-n_assign // MOE_TILE) + N_EXPERTS

    new_k, new_v, new_s = [], [], []
    y_final = None
    for l in range(DEPTH):
        w_main = w_in[l, :, :N_MAIN].astype(bf16)
        w_dt = _pad_lanes(w_in[l, :, N_MAIN:]).astype(bf16)
        q, k, v, glu, z, xbc, dt = _in_proj(geom, l, x, mod, norm1_g[l][None, :], w_main, w_dt, tables)

        lam_init = 0.8 - 0.6 * math.exp(-0.3 * l)
        subln = jnp.tile(attn_subln_g[l], LANES // ATT_V_DIM)[None, :]
        k_cache = cache_k[:, l].transpose(0, 3, 1, 2, 4).reshape(n_lat_seq, past_len, QK_WIDTH)
        v_cache = cache_v[:, l].transpose(0, 2, 1, 3).reshape(n_lat_seq, past_len, ATT_WIDTH)
        k_all = jnp.concatenate([k[geom.n_ctx:].reshape(n_lat_seq, lat_len, QK_WIDTH), k_cache], axis=1).astype(bf16)
        v_all = jnp.concatenate([v[geom.n_ctx:].reshape(n_lat_seq, lat_len, ATT_WIDTH), v_cache], axis=1).astype(bf16)
        attn_ctx = _attention(lam_init, attn_lambda[l], subln, q, k, v, n_ctx_seq, ctx_len, ctx_len, 0, False)
        attn_lat = _attention(lam_init, attn_lambda[l], subln, q, k_all, v_all, n_lat_seq, lat_len,
                              lat_len + past_len, geom.n_ctx, True)

        conv = _cconv(geom, glu, conv_dw_w[l], conv_dw_b[l][None, :], conv_ln_g[l][None, :], conv_ln_b[l][None, :])

        dt_bias = _pad_lanes(ssm_dt_bias[l].reshape(1, 2 * SSM_HEADS))
        a_log = _pad_lanes(ssm_a_log[l].reshape(1, 2 * SSM_HEADS))
        d_skip = jnp.repeat(ssm_d[l], SSM_HEAD_DIM)[None, :]
        ys, fins = [], []
        for direction in range(2):
            init = _pad_states(state_ssm[:, l, direction])
            y_d, fin = _ssd(geom, direction, xbc, dt, init, ssm_conv_w[l], ssm_conv_b[l][None, :], dt_bias, a_log, d_skip)
            ys.append(y_d)
            fins.append(_unpad_states(fin[:n_ctx_seq]))

        x, h2, top_idx, gate_w = _out_proj(geom, l, attn_ctx, attn_lat, conv, ys[0], ys[1], z, x, mod, ssm_norm_g[l][None, :],
                                  w_out[l].astype(bf16), norm2_g[l][None, :],
                                  _pad_lanes(router_w[l]), _pad_lanes(router_b[l][None, :]))

        src_tok, dest, block_e, n_used, row_start = _route(top_idx[:, :TOP_K], n_blocks)
        xs = jnp.take(h2, src_tok, axis=0, mode="clip")
        eo = _moe(l, xs, block_e, n_used, row_start, w_gate, b_gate, w_up, b_up, w_down, b_down)
        rows = jnp.take(eo, dest.T.reshape(-1), axis=0, mode="clip").reshape(TOP_K, n, d)
        if l == DEPTH - 1:
            y_final = _combine(geom, l, x, rows, gate_w, mod, final_g[None, :])
        else:
            x = _combine(geom, l, x, rows, gate_w, mod, None)

        new_k.append(k[:geom.n_ctx].reshape(n_ctx_seq, ctx_len, ATT_HEADS, 2, ATT_QK_DIM).transpose(0, 2, 3, 1, 4))
        new_v.append(v[:geom.n_ctx].reshape(n_ctx_seq, ctx_len, ATT_HEADS, ATT_V_DIM).transpose(0, 2, 1, 3))
        new_s.append(jnp.stack(fins, axis=1))

    y_prompt = y_final[:geom.n_ctx].reshape(n_ctx_seq, ctx_len, d)
    y_sample = y_final[geom.n_ctx:].reshape(n_lat_seq, lat_len, d)
    return (y_prompt, y_sample, jnp.stack(new_k, axis=1), jnp.stack(new_v, axis=1), jnp.stack(new_s, axis=1))
```

```python
import functools
import math

import numpy as np
import jax
import jax.numpy as jnp
from jax import lax
from jax.experimental import pallas as pl
from jax.experimental.pallas import tpu as pltpu

f32 = jnp.float32
bf16 = jnp.bfloat16
HIGHEST = lax.Precision.HIGHEST

D_MODEL = 1024
DEPTH = 2
GRID_W = 64
ATT_HEADS = 4
ATT_QK_DIM = 32
ATT_V_DIM = 64
ATT_WIDTH = 256
QK_WIDTH = 256
ROPE_AXIS_DIM = 16
ROPE_BASE = 10000.0
CONV_CH = 256
CONV_K = 31
SSM_WIDTH = 512
SSM_HEAD_DIM = 64
SSM_HEADS = 8
SSM_GROUPS = 2
SSM_STATE = 64
SSM_CONV_K = 5
SSM_XBC = 768
N_MAIN = 2560
N_EXPERTS = 32
TOP_K = 4
SWIGLU_LIMIT = 7.0
SWIGLU_ALPHA = 1.702
EPS = 1e-6

LANES = 128
SUBLANES = 8
TOK_TILE = 256
SSD_CHUNK = 128
CONV_HALO = 16
SSM_HALO = 8
MOE_TILE = 512
MOE_FF_CHUNK = 256
COND_ROWS = 16
VMEM_LIMIT = 56 * 1024 * 1024


def _cparams(*sem):
    return pltpu.CompilerParams(dimension_semantics=sem, vmem_limit_bytes=VMEM_LIMIT)


def _silu(x):
    return x * jax.nn.sigmoid(x)


def _ada_kernel(cond_ref, w_ref, b_ref, o_ref):
    s = _silu(cond_ref[...]).astype(bf16)
    o_ref[0] = jnp.dot(s, w_ref[0].astype(bf16), preferred_element_type=f32) + b_ref[0]


def _ada(cond, w_ada, b_ada):
    depth, d, n6 = w_ada.shape
    tn = 1536
    return pl.pallas_call(
        _ada_kernel,
        grid=(depth, n6 // tn),
        in_specs=[
            pl.BlockSpec((COND_ROWS, d), lambda l, j: (0, 0)),
            pl.BlockSpec((1, d, tn), lambda l, j: (l, 0, j)),
            pl.BlockSpec((1, 1, tn), lambda l, j: (l, 0, j)),
        ],
        out_specs=pl.BlockSpec((1, COND_ROWS, tn), lambda l, j: (l, 0, j)),
        out_shape=jax.ShapeDtypeStruct((depth, COND_ROWS, n6), f32),
        compiler_params=_cparams("arbitrary", "arbitrary"),
        name="ada",
    )(cond, w_ada, b_ada.reshape(depth, 1, n6))


class _Geom:
    def __init__(self, n_ctx_seq, ctx_len, n_lat_seq, lat_len):
        self.n_ctx_seq, self.ctx_len, self.n_lat_seq, self.lat_len = n_ctx_seq, ctx_len, n_lat_seq, lat_len
        self.n_ctx = n_ctx_seq * ctx_len
        self.n_lat = n_lat_seq * lat_len
        self.n_tok = self.n_ctx + self.n_lat
        assert ctx_len % TOK_TILE == 0 and lat_len % TOK_TILE == 0
        self.ctx_tiles = self.n_ctx // TOK_TILE
        self.tiles_per_ctx = ctx_len // TOK_TILE
        self.tiles_per_lat = lat_len // TOK_TILE
        self.n_tiles = self.n_tok // TOK_TILE

    def cond_row(self, i):
        return jnp.where(i < self.ctx_tiles, 0, 1 + (i - self.ctx_tiles) // self.tiles_per_lat)

    def seq_pos(self, i):
        j_ctx = i % self.tiles_per_ctx
        j_lat = (i - self.ctx_tiles) % self.tiles_per_lat
        is_ctx = i < self.ctx_tiles
        first = jnp.where(is_ctx, j_ctx == 0, j_lat == 0)
        last = jnp.where(is_ctx, j_ctx == self.tiles_per_ctx - 1, j_lat == self.tiles_per_lat - 1)
        return first, last

    def seq_index(self, i):
        return jnp.where(i < self.ctx_tiles, i // self.tiles_per_ctx,
                         self.n_ctx_seq + (i - self.ctx_tiles) // self.tiles_per_lat)


def _rope_tables(geom):
    t = np.arange(geom.lat_len)
    row = (t // GRID_W).astype(np.float64)
    col = (t % GRID_W).astype(np.float64)
    inv_freq = 1.0 / (ROPE_BASE ** (np.arange(0, ROPE_AXIS_DIM, 2, dtype=np.float64) / ROPE_AXIS_DIM))
    ang_r = row[:, None] * inv_freq[None, :]
    ang_c = col[:, None] * inv_freq[None, :]
    ang = np.concatenate([ang_r, ang_r, ang_c, ang_c], axis=-1)
    cos, sin = np.cos(ang), np.sin(ang)
    half = ROPE_AXIS_DIM // 2
    lane = np.arange(ATT_QK_DIM)
    first_half = (lane % ROPE_AXIS_DIM) < half
    sin_a = np.where(first_half[None, :], -sin, 0.0)
    sin_b = np.where(first_half[None, :], 0.0, sin)
    rep = LANES // ATT_QK_DIM

    def full(tab, ident):
        tab = np.tile(tab, (1, rep))
        return jnp.asarray(np.concatenate([np.full((TOK_TILE, LANES), ident), tab], axis=0), f32)

    return full(cos, 1.0), full(sin_a, 0.0), full(sin_b, 0.0)


def _in_proj_kernel(x_ref, mod_ref, g_ref, w_ref, wdt_ref, cos_ref, sa_ref, sb_ref,
                    q_ref, k_ref, v_ref, glu_ref, z_ref, xbc_ref, dt_ref):
    x = x_ref[...]
    y = x * lax.rsqrt(jnp.mean(x * x, axis=-1, keepdims=True) + EPS) * g_ref[...]
    h = (y * (1.0 + mod_ref[1:2, :]) + mod_ref[0:1, :]).astype(bf16)
    p = jnp.dot(h, w_ref[...], preferred_element_type=f32)
    dt_ref[...] = jnp.dot(h, wdt_ref[...], preferred_element_type=f32)
    cos, sa, sb = cos_ref[...], sa_ref[...], sb_ref[...]
    half = ROPE_AXIS_DIM // 2

    def rope(u):
        return u * cos + pltpu.roll(u, LANES - half, 1) * sa + pltpu.roll(u, half, 1) * sb

    scale = ATT_QK_DIM ** -0.5 * math.log2(math.e)
    for j in range(QK_WIDTH // LANES):
        lo = j * LANES
        q_ref[:, lo:lo + LANES] = (rope(p[:, lo:lo + LANES]) * scale).astype(bf16)
        k_ref[:, lo:lo + LANES] = rope(p[:, QK_WIDTH + lo:QK_WIDTH + lo + LANES])
    o = 2 * QK_WIDTH
    v_ref[...] = p[:, o:o + ATT_WIDTH]
    o += ATT_WIDTH
    glu_ref[...] = p[:, o:o + 2 * CONV_CH]
    o += 2 * CONV_CH
    z_ref[...] = p[:, o:o + SSM_WIDTH]
    o += SSM_WIDTH
    xbc_ref[...] = p[:, o:o + SSM_XBC]


def _in_proj(geom, layer, x, mod, norm_g, w_main, w_dt, tables):
    n = geom.n_tok
    tile = lambda w: pl.BlockSpec((TOK_TILE, w), lambda i: (i, 0))
    const = lambda a: pl.BlockSpec(a.shape, lambda i: (0,) * a.ndim)
    tab_spec = pl.BlockSpec(
        (TOK_TILE, LANES),
        lambda i: (jnp.where(i < geom.ctx_tiles, 0, 1 + (i - geom.ctx_tiles) % geom.tiles_per_lat), 0))
    widths = (QK_WIDTH, QK_WIDTH, ATT_WIDTH, 2 * CONV_CH, SSM_WIDTH, SSM_XBC, LANES)
    dtypes = (bf16, f32, f32, f32, f32, f32, f32)
    return pl.pallas_call(
        _in_proj_kernel,
        grid=(geom.n_tiles,),
        in_specs=[
            tile(D_MODEL),
            pl.BlockSpec((None, None, 6, D_MODEL), lambda i: (layer, geom.cond_row(i), 0, 0)),
            const(norm_g), const(w_main), const(w_dt), tab_spec, tab_spec, tab_spec,
        ],
        out_specs=[tile(w) for w in widths],
        out_shape=[jax.ShapeDtypeStruct((n, w), dt) for w, dt in zip(widths, dtypes)],
        compiler_params=_cparams("arbitrary"),
        name="in_proj",
    )(x, mod, norm_g, w_main, w_dt, *tables)


def _attn_kernel(lam_init, al_ref, g_ref, q_ref, k_ref, v_ref, o_ref):
    al = al_ref[...]
    lam = (jnp.exp(jnp.sum(al[0:1] * al[1:2], axis=-1, keepdims=True))
           - jnp.exp(jnp.sum(al[2:3] * al[3:4], axis=-1, keepdims=True)) + lam_init)
    lane = lax.broadcasted_iota(jnp.int32, (1, LANES), 1)
    heads_per_slab = LANES // ATT_V_DIM
    for j in range(ATT_WIDTH // LANES):
        sl = slice(j * LANES, (j + 1) * LANES)
        qs = q_ref[:, sl]
        ks = k_ref[:, sl].astype(bf16)
        vs = v_ref[:, sl].astype(bf16)
        a_heads = []
        for hl in range(heads_per_slab):
            own = (lane >= hl * ATT_V_DIM) & (lane < (hl + 1) * ATT_V_DIM)
            vh = jnp.where(own, vs, jnp.ones_like(vs))
            sum_lane = (1 - hl) * ATT_V_DIM
            comps = []
            for c in range(2):
                lo = hl * ATT_V_DIM + c * ATT_QK_DIM
                qm = jnp.where((lane >= lo) & (lane < lo + ATT_QK_DIM), qs, jnp.zeros_like(qs))
                s = lax.dot_general(qm, ks, (((1,), (1,)), ((), ())), preferred_element_type=f32)
                e = jnp.exp2(s - jnp.max(s, axis=-1, keepdims=True))
                pv = jnp.dot(e.astype(bf16), vh, preferred_element_type=f32)
                comps.append(pv / pv[:, sum_lane:sum_lane + 1])
            a_heads.append(comps[0] - lam * comps[1])
        a = jnp.where(lane < ATT_V_DIM, a_heads[0], a_heads[1])
        a2 = a * a
        ms0 = jnp.sum(jnp.where(lane < ATT_V_DIM, a2, 0.0), axis=-1, keepdims=True)
        ms1 = jnp.sum(jnp.where(lane < ATT_V_DIM, 0.0, a2), axis=-1, keepdims=True)
        ms = jnp.where(lane < ATT_V_DIM, ms0, ms1) * (1.0 / ATT_V_DIM)
        o_ref[:, sl] = a * lax.rsqrt(ms + EPS) * g_ref[...] * (1.0 - lam_init)


def _attention(lam_init, attn_lambda, subln_g, q, k, v, n_seq, q_len, k_len, q_row0, kv_is_3d):
    tq = TOK_TILE
    qb = q_len // tq
    q0 = q_row0 // tq
    if kv_is_3d:
        kv_spec = pl.BlockSpec((None, k_len, ATT_WIDTH), lambda b, i: (b, 0, 0))
    else:
        kv_spec = pl.BlockSpec((k_len, ATT_WIDTH), lambda b, i: (q_row0 // k_len + b, 0))
    return pl.pallas_call(
        functools.partial(_attn_kernel, lam_init),
        grid=(n_seq, qb),
        in_specs=[
            pl.BlockSpec(attn_lambda.shape, lambda b, i: (0, 0)),
            pl.BlockSpec(subln_g.shape, lambda b, i: (0, 0)),
            pl.BlockSpec((tq, ATT_WIDTH), lambda b, i: (q0 + b * qb + i, 0)),
            kv_spec, kv_spec,
        ],
        out_specs=pl.BlockSpec((tq, ATT_WIDTH), lambda b, i: (b * qb + i, 0)),
        out_shape=jax.ShapeDtypeStruct((n_seq * q_len, ATT_WIDTH), f32),
        compiler_params=_cparams("arbitrary", "arbitrary"),
        name="attention",
    )(attn_lambda, subln_g, q, k, v)


def _halo_specs(geom, width, halo):
    per = TOK_TILE // halo
    n_halo = geom.n_tok // halo
    prev = pl.BlockSpec((halo, width), lambda i: (jnp.maximum(i * per - 1, 0), 0))
    nxt = pl.BlockSpec((halo, width), lambda i: (jnp.minimum((i + 1) * per, n_halo - 1), 0))
    return prev, nxt


def _cconv_kernel(geom, cur_ref, prev_ref, next_ref, w_ref, b_ref, lg_ref, lb_ref, o_ref, pad_ref):
    first, last = geom.seq_pos(pl.program_id(0))

    def glu(ref):
        return ref[:, :CONV_CH] * jax.nn.sigmoid(ref[:, CONV_CH:])

    pad_ref[0:CONV_HALO, :] = jnp.where(first, 0.0, glu(prev_ref))
    pad_ref[CONV_HALO:CONV_HALO + TOK_TILE, :] = glu(cur_ref)
    pad_ref[CONV_HALO + TOK_TILE:, :] = jnp.where(last, 0.0, glu(next_ref))
    rows = 64
    base = CONV_HALO - CONV_K // 2
    for r0 in range(0, TOK_TILE, rows):
        acc = jnp.broadcast_to(b_ref[...], (rows, CONV_CH))
        for k in range(CONV_K):
            acc = acc + w_ref[k:k + 1, :] * pad_ref[r0 + base + k:r0 + base + k + rows, :]
        xc = acc - jnp.mean(acc, axis=-1, keepdims=True)
        var = jnp.mean(xc * xc, axis=-1, keepdims=True)
        o_ref[r0:r0 + rows, :] = _silu(xc * lax.rsqrt(var + EPS) * lg_ref[...] + lb_ref[...])


def _cconv(geom, glu, w, b, ln_g, ln_b):
    prev, nxt = _halo_specs(geom, 2 * CONV_CH, CONV_HALO)
    const = lambda a: pl.BlockSpec(a.shape, lambda i: (0,) * a.ndim)
    return pl.pallas_call(
        functools.partial(_cconv_kernel, geom),
        grid=(geom.n_tiles,),
        in_specs=[pl.BlockSpec((TOK_TILE, 2 * CONV_CH), lambda i: (i, 0)), prev, nxt,
                  const(w), const(b), const(ln_g), const(ln_b)],
        out_specs=pl.BlockSpec((TOK_TILE, CONV_CH), lambda i: (i, 0)),
        out_shape=jax.ShapeDtypeStruct((geom.n_tok, CONV_CH), f32),
        scratch_shapes=[pltpu.VMEM((TOK_TILE + 2 * CONV_HALO, CONV_CH), f32)],
        compiler_params=_cparams("arbitrary"),
        name="conformer_conv",
    )(glu, glu, glu, w, b, ln_g, ln_b)


def _softplus(x):
    return jnp.maximum(x, 0.0) + jnp.log1p(jnp.exp(-jnp.abs(x)))


def _ssd_kernel(geom, direction, cur_ref, prev_ref, next_ref, dt_ref, init_ref, cw_ref, cb_ref, dtb_ref,
                alog_ref, dskip_ref, y_ref, fin_ref, pad_ref, xc_ref, st_ref):
    fwd = direction == 0
    i = pl.program_id(0)
    t = i if fwd else geom.n_tiles - 1 - i
    first, last = geom.seq_pos(t)
    enter, leave = (first, last) if fwd else (last, first)
    is_lat = t >= geom.ctx_tiles

    @pl.when(enter)
    def _():
        st_ref[...] = jnp.where(is_lat, init_ref[...], 0.0)

    pad_ref[0:SSM_HALO, :] = jnp.where(first, 0.0, prev_ref[...])
    pad_ref[SSM_HALO:SSM_HALO + TOK_TILE, :] = cur_ref[...]
    pad_ref[SSM_HALO + TOK_TILE:, :] = jnp.where(last, 0.0, next_ref[...])
    base = SSM_HALO - SSM_CONV_K // 2
    acc = jnp.broadcast_to(cb_ref[...], (TOK_TILE, SSM_XBC))
    for k in range(SSM_CONV_K):
        acc = acc + cw_ref[k:k + 1, :] * pad_ref[base + k:base + k + TOK_TILE, :]
    xc_ref[...] = _silu(acc)

    L = SSD_CHUNK
    a_row = -jnp.exp(alog_ref[...])
    ri = lax.broadcasted_iota(jnp.int32, (L, L), 0)
    ci = lax.broadcasted_iota(jnp.int32, (L, L), 1)
    keep = (ci <= ri) if fwd else (ci >= ri)
    tri_col = keep.astype(f32)
    tri_row = ((ri <= ci) if fwd else (ri >= ci)).astype(f32)
    lane = lax.broadcasted_iota(jnp.int32, (1, LANES), 1)
    er = lax.broadcasted_iota(jnp.int32, (LANES, SSM_WIDTH), 0)
    ec = lax.broadcasted_iota(jnp.int32, (LANES, SSM_WIDTH), 1)
    expand = (er == direction * SSM_HEADS + ec // SSM_HEAD_DIM).astype(f32)
    heads_per_group = SSM_HEADS // SSM_GROUPS
    heads_per_slab = LANES // SSM_HEAD_DIM
    tot_at = L - 1 if fwd else 0

    chunks = range(TOK_TILE // L)
    for cidx in (chunks if fwd else reversed(chunks)):
        r0 = cidx * L
        xs = xc_ref[r0:r0 + L, 0:SSM_WIDTH]
        bm = xc_ref[r0:r0 + L, SSM_WIDTH:SSM_WIDTH + LANES]
        cm = xc_ref[r0:r0 + L, SSM_WIDTH + LANES:SSM_XBC]
        dt = _softplus(dt_ref[r0:r0 + L, :] + dtb_ref[...])
        da = dt * a_row
        acum_col = jnp.dot(tri_col, da, precision=HIGHEST, preferred_element_type=f32)
        acum_row = jnp.dot(da.T, tri_row, precision=HIGHEST, preferred_element_type=f32)
        tot_row = acum_col[tot_at:tot_at + 1, :]
        tot_col = acum_row[:, tot_at:tot_at + 1]
        decay_out = jnp.exp(tot_col - acum_row)
        decay_in = jnp.exp(acum_col)
        chunk_decay = jnp.exp(tot_row)
        xdt = (xs * jnp.dot(dt, expand, precision=HIGHEST, preferred_element_type=f32)).astype(bf16)
        bt = bm.T
        bb = bm.astype(bf16)
        ys = []
        for g in range(SSM_GROUPS):
            in_group = (lane >= g * SSM_STATE) & (lane < (g + 1) * SSM_STATE)
            cg = jnp.where(in_group, cm, 0.0)
            gmat = lax.dot_general(cg.astype(bf16), bb, (((1,), (1,)), ((), ())), preferred_element_type=f32)
            for hh in range(heads_per_group):
                h = g * heads_per_group + hh
                hl = direction * SSM_HEADS + h
                slab = xdt[:, (h // heads_per_slab) * LANES:(h // heads_per_slab + 1) * LANES]
                seg = acum_col[:, hl:hl + 1] - acum_row[hl:hl + 1, :]
                m = (gmat * jnp.exp(jnp.where(keep, seg, -jnp.inf))).astype(bf16)
                y_h = jnp.dot(m, slab, preferred_element_type=f32)
                c_in = (cg * decay_in[:, hl:hl + 1]).astype(bf16)
                state = st_ref[h]
                y_h = y_h + jnp.dot(c_in, state.astype(bf16), preferred_element_type=f32)
                b_out = (bt * decay_out[hl:hl + 1, :]).astype(bf16)
                st_ref[h] = chunk_decay[:, hl:hl + 1] * state + jnp.dot(b_out, slab, preferred_element_type=f32)
                ys.append(y_h)
        for j in range(SSM_WIDTH // LANES):
            y_slab = jnp.where(lane < SSM_HEAD_DIM, ys[heads_per_slab * j], ys[heads_per_slab * j + 1])
            if fwd:
                y_slab = y_slab + dskip_ref[:, j * LANES:(j + 1) * LANES] * xs[:, j * LANES:(j + 1) * LANES]
            y_ref[r0:r0 + L, j * LANES:(j + 1) * LANES] = y_slab

    @pl.when(leave)
    def _():
        fin_ref[...] = st_ref[...]


def _ssd(geom, direction, xbc, dt, init, conv_w, conv_b, dt_bias, a_log, d_skip):
    nt = geom.n_tiles
    tidx = (lambda i: i) if direction == 0 else (lambda i: nt - 1 - i)
    per = TOK_TILE // SSM_HALO
    n_halo = geom.n_tok // SSM_HALO
    n_seq = geom.n_ctx_seq + geom.n_lat_seq
    const = lambda a: pl.BlockSpec(a.shape, lambda i: (0,) * a.ndim)
    st_block = (None, SSM_HEADS, LANES, LANES)
    y, fin = pl.pallas_call(
        functools.partial(_ssd_kernel, geom, direction),
        grid=(nt,),
        in_specs=[
            pl.BlockSpec((TOK_TILE, SSM_XBC), lambda i: (tidx(i), 0)),
            pl.BlockSpec((SSM_HALO, SSM_XBC), lambda i: (jnp.maximum(tidx(i) * per - 1, 0), 0)),
            pl.BlockSpec((SSM_HALO, SSM_XBC), lambda i: (jnp.minimum((tidx(i) + 1) * per, n_halo - 1), 0)),
            pl.BlockSpec((TOK_TILE, LANES), lambda i: (tidx(i), 0)),
            pl.BlockSpec(st_block, lambda i: (jnp.maximum(geom.seq_index(tidx(i)) - geom.n_ctx_seq, 0), 0, 0, 0)),
            const(conv_w), const(conv_b), const(dt_bias), const(a_log), const(d_skip),
        ],
        out_specs=[
            pl.BlockSpec((TOK_TILE, SSM_WIDTH), lambda i: (tidx(i), 0)),
            pl.BlockSpec(st_block, lambda i: (geom.seq_index(tidx(i)), 0, 0, 0)),
        ],
        out_shape=[
            jax.ShapeDtypeStruct((geom.n_tok, SSM_WIDTH), f32),
            jax.ShapeDtypeStruct((n_seq, SSM_HEADS, LANES, LANES), f32),
        ],
        scratch_shapes=[
            pltpu.VMEM((TOK_TILE + 2 * SSM_HALO, SSM_XBC), f32),
            pltpu.VMEM((TOK_TILE, SSM_XBC), f32),
            pltpu.VMEM((SSM_HEADS, LANES, LANES), f32),
        ],
        compiler_params=_cparams("arbitrary"),
        name="ssd_scan",
    )(xbc, xbc, xbc, dt, init, conv_w, conv_b, dt_bias, a_log, d_skip)
    return y, fin


def _pad_states(s):
    return jnp.tile(jnp.swapaxes(s, -1, -2), (1, 1, LANES // SSM_STATE, LANES // SSM_HEAD_DIM))


def _unpad_states(fin):
    hpg = SSM_HEADS // SSM_GROUPS
    outs = []
    for h in range(SSM_HEADS):
        g, side = h // hpg, h % (LANES // SSM_HEAD_DIM)
        q = fin[:, h, g * SSM_STATE:(g + 1) * SSM_STATE, side * SSM_HEAD_DIM:(side + 1) * SSM_HEAD_DIM]
        outs.append(jnp.swapaxes(q, -1, -2))
    return jnp.stack(outs, axis=1)


def _out_proj_kernel(geom, actx_ref, alat_ref, conv_ref, yf_ref, yb_ref, z_ref, x_ref, mod_ref, sg_ref, w_ref,
                     g2_ref, rwh_ref, rwl_ref, rb_ref, xo_ref, h2_ref, ti_ref, gw_ref, cnt_ref):
    attn = jnp.where(pl.program_id(0) < geom.ctx_tiles, actx_ref[...], alat_ref[...])
    gated = (yf_ref[...] + yb_ref[...]) * _silu(z_ref[...])
    ssm = gated * lax.rsqrt(jnp.mean(gated * gated, axis=-1, keepdims=True) + EPS) * sg_ref[...]
    o1 = ATT_WIDTH
    o2 = ATT_WIDTH + CONV_CH
    mix = (jnp.dot(attn.astype(bf16), w_ref[0:o1, :], preferred_element_type=f32)
           + jnp.dot(conv_ref[...].astype(bf16), w_ref[o1:o2, :], preferred_element_type=f32)
           + jnp.dot(ssm.astype(bf16), w_ref[o2:, :], preferred_element_type=f32))
    x = x_ref[...] + mod_ref[2:3, :] * mix
    xo_ref[...] = x
    y = x * lax.rsqrt(jnp.mean(x * x, axis=-1, keepdims=True) + EPS) * g2_ref[...]
    h2 = y * (1.0 + mod_ref[4:5, :]) + mod_ref[3:4, :]
    h2_ref[...] = h2
    h_hi = h2.astype(bf16)
    h_lo = (h2 - h_hi.astype(f32)).astype(bf16)
    logits = (jnp.dot(h_hi, rwh_ref[...], preferred_element_type=f32)
              + jnp.dot(h_lo, rwh_ref[...], preferred_element_type=f32)
              + jnp.dot(h_hi, rwl_ref[...], preferred_element_type=f32)) + rb_ref[...]
    lane = lax.broadcasted_iota(jnp.int32, (1, LANES), 1)
    lane_f = lane.astype(f32)
    lg = jnp.where(lane < N_EXPERTS, logits, -jnp.inf)
    idx_out = jnp.zeros(logits.shape, jnp.int32)
    gate_out = jnp.zeros(logits.shape, f32)
    top0 = None
    denom = None
    picks = []
    for k in range(TOP_K):
        m = jnp.max(lg, axis=-1, keepdims=True)
        idx = jnp.min(jnp.where(lg == m, lane_f, float(LANES)), axis=-1, keepdims=True).astype(jnp.int32)
        picked = lane == idx
        picks.append(picked)
        lg = jnp.where(picked, -jnp.inf, lg)
        if k == 0:
            top0 = m
        e = jnp.exp(m - top0)
        denom = e if k == 0 else denom + e
        idx_out = jnp.where(lane == k, idx, idx_out)
        gate_out = jnp.where(lane == k, e, gate_out)
    gw_ref[...] = gate_out / denom

    @pl.when(pl.program_id(0) == 0)
    def _():
        cnt_ref[...] = jnp.zeros_like(cnt_ref)

    chosen = picks[0] | picks[1] | picks[2] | picks[3]
    sel = jnp.where(chosen, 1.0, 0.0)
    ri = lax.broadcasted_iota(jnp.int32, (TOK_TILE, TOK_TILE), 0)
    ci = lax.broadcasted_iota(jnp.int32, (TOK_TILE, TOK_TILE), 1)
    before = jnp.where(ci < ri, 1.0, 0.0).astype(bf16)
    prior = jnp.dot(before, sel.astype(bf16), preferred_element_type=f32) + cnt_ref[...]
    for k in range(TOP_K):
        rank = jnp.sum(jnp.where(picks[k], prior, 0.0), axis=-1, keepdims=True).astype(jnp.int32)
        idx_out = jnp.where(lane == TOP_K + k, rank, idx_out)
    ti_ref[...] = idx_out
    cnt_ref[...] = cnt_ref[...] + jnp.sum(sel, axis=0, keepdims=True)


def _out_proj(geom, layer, attn_ctx, attn_lat, conv, y_f, y_b, z, x, mod, ssm_norm_g, w_out, norm2_g, router_w,
              router_b):
    n = geom.n_tok
    tile = lambda w: pl.BlockSpec((TOK_TILE, w), lambda i: (i, 0))
    const = lambda a: pl.BlockSpec(a.shape, lambda i: (0,) * a.ndim)
    nct = geom.ctx_tiles
    rw_hi = router_w.astype(bf16)
    rw_lo = (router_w - rw_hi.astype(f32)).astype(bf16)
    return pl.pallas_call(
        functools.partial(_out_proj_kernel, geom),
        grid=(geom.n_tiles,),
        in_specs=[
            pl.BlockSpec((TOK_TILE, ATT_WIDTH), lambda i: (jnp.minimum(i, nct - 1), 0)),
            pl.BlockSpec((TOK_TILE, ATT_WIDTH), lambda i: (jnp.maximum(i - nct, 0), 0)),
            tile(CONV_CH), tile(SSM_WIDTH), tile(SSM_WIDTH), tile(SSM_WIDTH), tile(D_MODEL),
            pl.BlockSpec((None, None, 6, D_MODEL), lambda i: (layer, geom.cond_row(i), 0, 0)),
            const(ssm_norm_g), const(w_out), const(norm2_g), const(rw_hi), const(rw_lo), const(router_b),
        ],
        out_specs=[tile(D_MODEL), tile(D_MODEL), tile(LANES), tile(LANES), pl.BlockSpec((1, LANES), lambda i: (0, 0))],
        out_shape=[jax.ShapeDtypeStruct((n, D_MODEL), f32), jax.ShapeDtypeStruct((n, D_MODEL), f32),
                   jax.ShapeDtypeStruct((n, LANES), jnp.int32), jax.ShapeDtypeStruct((n, LANES), f32),
                   jax.ShapeDtypeStruct((1, LANES), f32)],
        compiler_params=_cparams("arbitrary"),
        name="out_proj",
    )(attn_ctx, attn_lat, conv, y_f, y_b, z, x, mod, ssm_norm_g, w_out, norm2_g, rw_hi, rw_lo, router_b)


def _moe_kernel(be_ref, nused_ref, rs_ref, x_ref, wg_ref, bg_ref, wu_ref, bu_ref, wd_ref, bd_ref, o_ref,
                wg_s, wu_s, wd_s):
    i = pl.program_id(0)
    prev_e = be_ref[jnp.maximum(i - 1, 0)]
    new_expert = (i == 0) | (be_ref[i] != prev_e)

    @pl.when(new_expert)
    def _():
        wg_s[...] = wg_ref[0].astype(bf16)
        wu_s[...] = wu_ref[0].astype(bf16)
        wd_s[...] = wd_ref[0].astype(bf16)

    @pl.when(i < nused_ref[0])
    def _():
        x = x_ref[...].astype(bf16)
        acc = None
        for c0 in range(0, wg_s.shape[1], MOE_FF_CHUNK):
            sl = slice(c0, c0 + MOE_FF_CHUNK)
            g = jnp.dot(x, wg_s[:, sl], preferred_element_type=f32) + bg_ref[0, :, sl]
            u = jnp.dot(x, wu_s[:, sl], preferred_element_type=f32) + bu_ref[0, :, sl]
            g = jnp.minimum(g, SWIGLU_LIMIT)
            u = jnp.clip(u, -SWIGLU_LIMIT, SWIGLU_LIMIT)
            act = g * jax.nn.sigmoid(SWIGLU_ALPHA * g) * (u + 1.0)
            part = jnp.dot(act.astype(bf16), wd_s[sl, :], preferred_element_type=f32)
            acc = part if acc is None else acc + part
        o_ref[...] = acc + bd_ref[0]

    @pl.when(i >= nused_ref[0])
    def _():
        o_ref[...] = jnp.zeros_like(o_ref)


def _moe(layer, xs, block_e, n_used, row_start, w_gate, b_gate, w_up, b_up, w_down, b_down):
    d = xs.shape[1]
    n_blocks = block_e.shape[0]
    d_ff = w_gate.shape[-1]
    w_spec = lambda r, c: pl.BlockSpec((None, 1, r, c), lambda i, be, nu, rs: (layer, be[i], 0, 0))
    b_spec = lambda c: pl.BlockSpec((None, 1, 1, c), lambda i, be, nu, rs: (layer, be[i], 0, 0))
    depth, n_e = b_gate.shape[:2]
    return pl.pallas_call(
        _moe_kernel,
        grid_spec=pltpu.PrefetchScalarGridSpec(
            num_scalar_prefetch=3,
            grid=(n_blocks,),
            in_specs=[pl.BlockSpec((pl.Element(MOE_TILE), pl.Element(d)), lambda i, be, nu, rs: (rs[i] * SUBLANES, 0)),
                      w_spec(d, d_ff), b_spec(d_ff), w_spec(d, d_ff), b_spec(d_ff), w_spec(d_ff, d), b_spec(d)],
            out_specs=pl.BlockSpec((MOE_TILE, d), lambda i, be, nu, rs: (i, 0)),
            scratch_shapes=[pltpu.VMEM((d, d_ff), bf16), pltpu.VMEM((d, d_ff), bf16), pltpu.VMEM((d_ff, d), bf16)],
        ),
        out_shape=jax.ShapeDtypeStruct((n_blocks * MOE_TILE, d), f32),
        compiler_params=_cparams("arbitrary"),
        name="moe_experts",
    )(block_e, n_used, row_start, xs, w_gate, b_gate.reshape(depth, n_e, 1, d_ff), w_up,
      b_up.reshape(depth, n_e, 1, d_ff), w_down, b_down.reshape(depth, n_e, 1, d))


def _expert_table(expert_ids, table):
    onehot = expert_ids[:, None] == jnp.arange(N_EXPERTS, dtype=jnp.int32)[None, :]
    return jnp.sum(jnp.where(onehot, table[None, :], 0), axis=1)


def _route(top_idx, rank, counts, n_blocks):
    n = top_idx.shape[0]
    n_assign = n * TOP_K
    i32 = jnp.int32
    flat_e = top_idx.reshape(-1)
    flat_rank = rank.reshape(-1)
    experts = jnp.arange(N_EXPERTS, dtype=i32)
    fill = SUBLANES - 1
    grp_size = (counts + fill) // SUBLANES * SUBLANES
    grp_start = jnp.cumsum(grp_size) - grp_size
    nb = (counts + MOE_TILE - 1) // MOE_TILE
    blk_end = jnp.cumsum(nb)
    blk_start = blk_end - nb
    n_sorted = n_assign + N_EXPERTS * fill
    j = jnp.arange(fill, dtype=i32)
    filler_pos = jnp.where(j[None, :] < (grp_size - counts)[:, None], (grp_start + counts)[:, None] + j[None, :],
                           n_sorted)
    pos = jnp.concatenate([_expert_table(flat_e, grp_start) + flat_rank, filler_pos.reshape(-1)])
    tok = jnp.concatenate([jnp.arange(n_assign, dtype=i32) // TOP_K, jnp.zeros((N_EXPERTS * fill,), i32)])
    _, sorted_tok = lax.sort((pos, tok), num_keys=1)
    src_tok = jnp.concatenate([sorted_tok, jnp.zeros((MOE_TILE,), i32)])
    dest = _expert_table(flat_e, blk_start * MOE_TILE) + flat_rank
    blk = jnp.arange(n_blocks, dtype=i32)
    block_e = jnp.minimum(jnp.sum((blk_end[None, :] <= blk[:, None]).astype(i32), axis=1), N_EXPERTS - 1)
    n_used = blk_end[-1:].astype(i32)
    row_start = jnp.where(blk < n_used[0], blk * MOE_TILE - _expert_table(block_e, blk_start * MOE_TILE - grp_start), 0)
    return src_tok, dest.reshape(n, TOP_K), block_e, n_used, row_start // SUBLANES


def _combine_kernel(final, x_ref, rows_ref, gw_ref, mod_ref, *rest):
    ffn = rows_ref[0] * gw_ref[:, 0:1]
    for k in range(1, TOP_K):
        ffn = ffn + rows_ref[k] * gw_ref[:, k:k + 1]
    x = x_ref[...] + mod_ref[5:6, :] * ffn
    if final:
        fg_ref, y_ref = rest
        y_ref[...] = x * lax.rsqrt(jnp.mean(x * x, axis=-1, keepdims=True) + EPS) * fg_ref[...]
    else:
        rest[0][...] = x


def _combine(geom, layer, x, rows, gate_w, mod, final_g):
    n = geom.n_tok
    final = final_g is not None
    tile = pl.BlockSpec((TOK_TILE, D_MODEL), lambda i: (i, 0))
    in_specs = [
        tile,
        pl.BlockSpec((TOP_K, TOK_TILE, D_MODEL), lambda i: (0, i, 0)),
        pl.BlockSpec((TOK_TILE, LANES), lambda i: (i, 0)),
        pl.BlockSpec((None, None, 6, D_MODEL), lambda i: (layer, geom.cond_row(i), 0, 0)),
    ]
    args = [x, rows, gate_w, mod]
    if final:
        in_specs.append(pl.BlockSpec(final_g.shape, lambda i: (0, 0)))
        args.append(final_g)
    return pl.pallas_call(
        functools.partial(_combine_kernel, final),
        grid=(geom.n_tiles,),
        in_specs=in_specs,
        out_specs=tile,
        out_shape=jax.ShapeDtypeStruct((n, D_MODEL), f32),
        compiler_params=_cparams("arbitrary"),
        name="moe_combine",
    )(*args)


def _pad_lanes(a, width=LANES):
    return jnp.pad(a, [(0, 0)] * (a.ndim - 1) + [(0, width - a.shape[-1])])


def kernel(x_prompt, x_sample, c, cache_k, cache_v, state_ssm, c_ctx, norm1_g, norm2_g, w_ada, b_ada, w_in, w_out, attn_lambda, attn_subln_g, conv_dw_w, conv_dw_b, conv_ln_g, conv_ln_b, ssm_conv_w, ssm_conv_b, ssm_dt_bias, ssm_a_log, ssm_d, ssm_norm_g, router_w, router_b, w_gate, b_gate, w_up, b_up, w_down, b_down, final_g):
    n_ctx_seq, ctx_len, d = x_prompt.shape
    n_lat_seq, lat_len, _ = x_sample.shape
    past_len = cache_k.shape[-2]
    geom = _Geom(n_ctx_seq, ctx_len, n_lat_seq, lat_len)
    n = geom.n_tok

    x = jnp.concatenate([x_prompt.reshape(geom.n_ctx, d), x_sample.reshape(geom.n_lat, d)], axis=0)
    cond = jnp.concatenate([c_ctx[None, :], c, jnp.zeros((COND_ROWS - 1 - n_lat_seq, d), f32)], axis=0)
    mod = _ada(cond, w_ada, b_ada).reshape(DEPTH, COND_ROWS, 6, d)
    tables = _rope_tables(geom)

    n_assign = n * TOP_K
    n_blocks = -(-n_assign // MOE_TILE) + N_EXPERTS

    new_k, new_v, new_s = [], [], []
    y_final = None
    for l in range(DEPTH):
        w_main = w_in[l, :, :N_MAIN].astype(bf16)
        w_dt = _pad_lanes(w_in[l, :, N_MAIN:]).astype(bf16)
        q, k, v, glu, z, xbc, dt = _in_proj(geom, l, x, mod, norm1_g[l][None, :], w_main, w_dt, tables)

        lam_init = 0.8 - 0.6 * math.exp(-0.3 * l)
        subln = jnp.tile(attn_subln_g[l], LANES // ATT_V_DIM)[None, :]
        k_cache = cache_k[:, l].transpose(0, 3, 1, 2, 4).reshape(n_lat_seq, past_len, QK_WIDTH)
        v_cache = cache_v[:, l].transpose(0, 2, 1, 3).reshape(n_lat_seq, past_len, ATT_WIDTH)
        k_all = jnp.concatenate([k[geom.n_ctx:].reshape(n_lat_seq, lat_len, QK_WIDTH), k_cache], axis=1).astype(bf16)
        v_all = jnp.concatenate([v[geom.n_ctx:].reshape(n_lat_seq, lat_len, ATT_WIDTH), v_cache], axis=1).astype(bf16)
        attn_ctx = _attention(lam_init, attn_lambda[l], subln, q, k, v, n_ctx_seq, ctx_len, ctx_len, 0, False)
        attn_lat = _attention(lam_init, attn_lambda[l], subln, q, k_all, v_all, n_lat_seq, lat_len,
                              lat_len + past_len, geom.n_ctx, True)

        conv = _cconv(geom, glu, conv_dw_w[l], conv_dw_b[l][None, :], conv_ln_g[l][None, :], conv_ln_b[l][None, :])

        dt_bias = _pad_lanes(ssm_dt_bias[l].reshape(1, 2 * SSM_HEADS))
        a_log = _pad_lanes(ssm_a_log[l].reshape(1, 2 * SSM_HEADS))
        d_skip = jnp.repeat(ssm_d[l], SSM_HEAD_DIM)[None, :]
        ys, fins = [], []
        for direction in range(2):
            init = _pad_states(state_ssm[:, l, direction])
            y_d, fin = _ssd(geom, direction, xbc, dt, init, ssm_conv_w[l], ssm_conv_b[l][None, :], dt_bias, a_log, d_skip)
            ys.append(y_d)
            fins.append(_unpad_states(fin[:n_ctx_seq]))

        x, h2, routed, gate_w, totals = _out_proj(
            geom, l, attn_ctx, attn_lat, conv, ys[0], ys[1], z, x, mod, ssm_norm_g[l][None, :], w_out[l].astype(bf16),
            norm2_g[l][None, :], _pad_lanes(router_w[l]), _pad_lanes(router_b[l][None, :]))

        src_tok, dest, block_e, n_used, row_start = _route(
            routed[:, :TOP_K], routed[:, TOP_K:2 * TOP_K], totals[0, :N_EXPERTS].astype(jnp.int32), n_blocks)
        xs = jnp.take(h2, src_tok, axis=0, mode="clip")
        eo = _moe(l, xs, block_e, n_used, row_start, w_gate, b_gate, w_up, b_up, w_down, b_down)
        rows = jnp.take(eo, dest.T.reshape(-1), axis=0, mode="clip").reshape(TOP_K, n, d)
        if l == DEPTH - 1:
            y_final = _combine(geom, l, x, rows, gate_w, mod, final_g[None, :])
        else:
            x = _combine(geom, l, x, rows, gate_w, mod, None)

        new_k.append(k[:geom.n_ctx].reshape(n_ctx_seq, ctx_len, ATT_HEADS, 2, ATT_QK_DIM).transpose(0, 2, 3, 1, 4))
        new_v.append(v[:geom.n_ctx].reshape(n_ctx_seq, ctx_len, ATT_HEADS, ATT_V_DIM).transpose(0, 2, 1, 3))
        new_s.append(jnp.stack(fins, axis=1))

    y_prompt = y_final[:geom.n_ctx].reshape(n_ctx_seq, ctx_len, d)
    y_sample = y_final[geom.n_ctx:].reshape(n_lat_seq, lat_len, d)
    return (y_prompt, y_sample, jnp.stack(new_k, axis=1), jnp.stack(new_v, axis=1), jnp.stack(new_s, axis=1))
```

```python
import functools
import math

import numpy as np
import jax
import jax.numpy as jnp
from jax import lax
from jax.experimental import pallas as pl
from jax.experimental.pallas import tpu as pltpu

f32 = jnp.float32
bf16 = jnp.bfloat16
HIGHEST = lax.Precision.HIGHEST

D_MODEL = 1024
DEPTH = 2
GRID_W = 64
ATT_HEADS = 4
ATT_QK_DIM = 32
ATT_V_DIM = 64
ATT_WIDTH = 256
QK_WIDTH = 256
ROPE_AXIS_DIM = 16
ROPE_BASE = 10000.0
CONV_CH = 256
CONV_K = 31
SSM_WIDTH = 512
SSM_HEAD_DIM = 64
SSM_HEADS = 8
SSM_GROUPS = 2
SSM_STATE = 64
SSM_CONV_K = 5
SSM_XBC = 768
N_MAIN = 2560
N_EXPERTS = 32
TOP_K = 4
SWIGLU_LIMIT = 7.0
SWIGLU_ALPHA = 1.702
EPS = 1e-6

LANES = 128
SUBLANES = 8
TOK_TILE = 256
SSD_CHUNK = 128
CONV_HALO = 16
SSM_HALO = 8
MOE_TILE = 512
MOE_FF_CHUNK = 256
TOKEN_KEY_RANGE = 2 ** 15
COND_ROWS = 16
VMEM_LIMIT = 56 * 1024 * 1024


def _cparams(*sem):
    return pltpu.CompilerParams(dimension_semantics=sem, vmem_limit_bytes=VMEM_LIMIT)


def _silu(x):
    return x * jax.nn.sigmoid(x)


def _ada_kernel(cond_ref, w_ref, b_ref, o_ref):
    s = _silu(cond_ref[...]).astype(bf16)
    o_ref[0] = jnp.dot(s, w_ref[0].astype(bf16), preferred_element_type=f32) + b_ref[0]


def _ada(cond, w_ada, b_ada):
    depth, d, n6 = w_ada.shape
    tn = 1536
    return pl.pallas_call(
        _ada_kernel,
        grid=(depth, n6 // tn),
        in_specs=[
            pl.BlockSpec((COND_ROWS, d), lambda l, j: (0, 0)),
            pl.BlockSpec((1, d, tn), lambda l, j: (l, 0, j)),
            pl.BlockSpec((1, 1, tn), lambda l, j: (l, 0, j)),
        ],
        out_specs=pl.BlockSpec((1, COND_ROWS, tn), lambda l, j: (l, 0, j)),
        out_shape=jax.ShapeDtypeStruct((depth, COND_ROWS, n6), f32),
        compiler_params=_cparams("arbitrary", "arbitrary"),
        name="ada",
    )(cond, w_ada, b_ada.reshape(depth, 1, n6))


class _Geom:
    def __init__(self, n_ctx_seq, ctx_len, n_lat_seq, lat_len):
        self.n_ctx_seq, self.ctx_len, self.n_lat_seq, self.lat_len = n_ctx_seq, ctx_len, n_lat_seq, lat_len
        self.n_ctx = n_ctx_seq * ctx_len
        self.n_lat = n_lat_seq * lat_len
        self.n_tok = self.n_ctx + self.n_lat
        assert ctx_len % TOK_TILE == 0 and lat_len % TOK_TILE == 0
        self.ctx_tiles = self.n_ctx // TOK_TILE
        self.tiles_per_ctx = ctx_len // TOK_TILE
        self.tiles_per_lat = lat_len // TOK_TILE
        self.n_tiles = self.n_tok // TOK_TILE

    def cond_row(self, i):
        return jnp.where(i < self.ctx_tiles, 0, 1 + (i - self.ctx_tiles) // self.tiles_per_lat)

    def seq_pos(self, i):
        j_ctx = i % self.tiles_per_ctx
        j_lat = (i - self.ctx_tiles) % self.tiles_per_lat
        is_ctx = i < self.ctx_tiles
        first = jnp.where(is_ctx, j_ctx == 0, j_lat == 0)
        last = jnp.where(is_ctx, j_ctx == self.tiles_per_ctx - 1, j_lat == self.tiles_per_lat - 1)
        return first, last

    def seq_index(self, i):
        return jnp.where(i < self.ctx_tiles, i // self.tiles_per_ctx,
                         self.n_ctx_seq + (i - self.ctx_tiles) // self.tiles_per_lat)


def _rope_tables(geom):
    t = np.arange(geom.lat_len)
    row = (t // GRID_W).astype(np.float64)
    col = (t % GRID_W).astype(np.float64)
    inv_freq = 1.0 / (ROPE_BASE ** (np.arange(0, ROPE_AXIS_DIM, 2, dtype=np.float64) / ROPE_AXIS_DIM))
    ang_r = row[:, None] * inv_freq[None, :]
    ang_c = col[:, None] * inv_freq[None, :]
    ang = np.concatenate([ang_r, ang_r, ang_c, ang_c], axis=-1)
    cos, sin = np.cos(ang), np.sin(ang)
    half = ROPE_AXIS_DIM // 2
    lane = np.arange(ATT_QK_DIM)
    first_half = (lane % ROPE_AXIS_DIM) < half
    sin_a = np.where(first_half[None, :], -sin, 0.0)
    sin_b = np.where(first_half[None, :], 0.0, sin)
    rep = LANES // ATT_QK_DIM

    def full(tab, ident):
        tab = np.tile(tab, (1, rep))
        return jnp.asarray(np.concatenate([np.full((TOK_TILE, LANES), ident), tab], axis=0), f32)

    return full(cos, 1.0), full(sin_a, 0.0), full(sin_b, 0.0)


def _in_proj_kernel(x_ref, mod_ref, g_ref, w_ref, wdt_ref, cos_ref, sa_ref, sb_ref,
                    q_ref, k_ref, v_ref, glu_ref, z_ref, xbc_ref, dt_ref):
    x = x_ref[...]
    y = x * lax.rsqrt(jnp.mean(x * x, axis=-1, keepdims=True) + EPS) * g_ref[...]
    h = (y * (1.0 + mod_ref[1:2, :]) + mod_ref[0:1, :]).astype(bf16)
    p = jnp.dot(h, w_ref[...], preferred_element_type=f32)
    dt_ref[...] = jnp.dot(h, wdt_ref[...], preferred_element_type=f32)
    cos, sa, sb = cos_ref[...], sa_ref[...], sb_ref[...]
    half = ROPE_AXIS_DIM // 2

    def rope(u):
        return u * cos + pltpu.roll(u, LANES - half, 1) * sa + pltpu.roll(u, half, 1) * sb

    scale = ATT_QK_DIM ** -0.5 * math.log2(math.e)
    for j in range(QK_WIDTH // LANES):
        lo = j * LANES
        q_ref[:, lo:lo + LANES] = (rope(p[:, lo:lo + LANES]) * scale).astype(bf16)
        k_ref[:, lo:lo + LANES] = rope(p[:, QK_WIDTH + lo:QK_WIDTH + lo + LANES])
    o = 2 * QK_WIDTH
    v_ref[...] = p[:, o:o + ATT_WIDTH]
    o += ATT_WIDTH
    glu_ref[...] = p[:, o:o + 2 * CONV_CH]
    o += 2 * CONV_CH
    z_ref[...] = p[:, o:o + SSM_WIDTH]
    o += SSM_WIDTH
    xbc_ref[...] = p[:, o:o + SSM_XBC]


def _in_proj(geom, layer, x, mod, norm_g, w_main, w_dt, tables):
    n = geom.n_tok
    tile = lambda w: pl.BlockSpec((TOK_TILE, w), lambda i: (i, 0))
    const = lambda a: pl.BlockSpec(a.shape, lambda i: (0,) * a.ndim)
    tab_spec = pl.BlockSpec(
        (TOK_TILE, LANES),
        lambda i: (jnp.where(i < geom.ctx_tiles, 0, 1 + (i - geom.ctx_tiles) % geom.tiles_per_lat), 0))
    widths = (QK_WIDTH, QK_WIDTH, ATT_WIDTH, 2 * CONV_CH, SSM_WIDTH, SSM_XBC, LANES)
    dtypes = (bf16, f32, f32, f32, f32, f32, f32)
    return pl.pallas_call(
        _in_proj_kernel,
        grid=(geom.n_tiles,),
        in_specs=[
            tile(D_MODEL),
            pl.BlockSpec((None, None, 6, D_MODEL), lambda i: (layer, geom.cond_row(i), 0, 0)),
            const(norm_g), const(w_main), const(w_dt), tab_spec, tab_spec, tab_spec,
        ],
        out_specs=[tile(w) for w in widths],
        out_shape=[jax.ShapeDtypeStruct((n, w), dt) for w, dt in zip(widths, dtypes)],
        compiler_params=_cparams("arbitrary"),
        name="in_proj",
    )(x, mod, norm_g, w_main, w_dt, *tables)


def _attn_kernel(lam_init, al_ref, g_ref, q_ref, k_ref, v_ref, o_ref):
    al = al_ref[...]
    lam = (jnp.exp(jnp.sum(al[0:1] * al[1:2], axis=-1, keepdims=True))
           - jnp.exp(jnp.sum(al[2:3] * al[3:4], axis=-1, keepdims=True)) + lam_init)
    lane = lax.broadcasted_iota(jnp.int32, (1, LANES), 1)
    heads_per_slab = LANES // ATT_V_DIM
    for j in range(ATT_WIDTH // LANES):
        sl = slice(j * LANES, (j + 1) * LANES)
        qs = q_ref[:, sl]
        ks = k_ref[:, sl].astype(bf16)
        vs = v_ref[:, sl].astype(bf16)
        a_heads = []
        for hl in range(heads_per_slab):
            own = (lane >= hl * ATT_V_DIM) & (lane < (hl + 1) * ATT_V_DIM)
            vh = jnp.where(own, vs, jnp.ones_like(vs))
            sum_lane = (1 - hl) * ATT_V_DIM
            comps = []
            for c in range(2):
                lo = hl * ATT_V_DIM + c * ATT_QK_DIM
                qm = jnp.where((lane >= lo) & (lane < lo + ATT_QK_DIM), qs, jnp.zeros_like(qs))
                s = lax.dot_general(qm, ks, (((1,), (1,)), ((), ())), preferred_element_type=f32)
                e = jnp.exp2(s - jnp.max(s, axis=-1, keepdims=True))
                pv = jnp.dot(e.astype(bf16), vh, preferred_element_type=f32)
                comps.append(pv / pv[:, sum_lane:sum_lane + 1])
            a_heads.append(comps[0] - lam * comps[1])
        a = jnp.where(lane < ATT_V_DIM, a_heads[0], a_heads[1])
        a2 = a * a
        ms0 = jnp.sum(jnp.where(lane < ATT_V_DIM, a2, 0.0), axis=-1, keepdims=True)
        ms1 = jnp.sum(jnp.where(lane < ATT_V_DIM, 0.0, a2), axis=-1, keepdims=True)
        ms = jnp.where(lane < ATT_V_DIM, ms0, ms1) * (1.0 / ATT_V_DIM)
        o_ref[:, sl] = a * lax.rsqrt(ms + EPS) * g_ref[...] * (1.0 - lam_init)


def _attention(lam_init, attn_lambda, subln_g, q, k, v, n_seq, q_len, k_len, q_row0, kv_is_3d):
    tq = TOK_TILE
    qb = q_len // tq
    q0 = q_row0 // tq
    if kv_is_3d:
        kv_spec = pl.BlockSpec((None, k_len, ATT_WIDTH), lambda b, i: (b, 0, 0))
    else:
        kv_spec = pl.BlockSpec((k_len, ATT_WIDTH), lambda b, i: (q_row0 // k_len + b, 0))
    return pl.pallas_call(
        functools.partial(_attn_kernel, lam_init),
        grid=(n_seq, qb),
        in_specs=[
            pl.BlockSpec(attn_lambda.shape, lambda b, i: (0, 0)),
            pl.BlockSpec(subln_g.shape, lambda b, i: (0, 0)),
            pl.BlockSpec((tq, ATT_WIDTH), lambda b, i: (q0 + b * qb + i, 0)),
            kv_spec, kv_spec,
        ],
        out_specs=pl.BlockSpec((tq, ATT_WIDTH), lambda b, i: (b * qb + i, 0)),
        out_shape=jax.ShapeDtypeStruct((n_seq * q_len, ATT_WIDTH), f32),
        compiler_params=_cparams("arbitrary", "arbitrary"),
        name="attention",
    )(attn_lambda, subln_g, q, k, v)


def _halo_specs(geom, width, halo):
    per = TOK_TILE // halo
    n_halo = geom.n_tok // halo
    prev = pl.BlockSpec((halo, width), lambda i: (jnp.maximum(i * per - 1, 0), 0))
    nxt = pl.BlockSpec((halo, width), lambda i: (jnp.minimum((i + 1) * per, n_halo - 1), 0))
    return prev, nxt


def _cconv_kernel(geom, cur_ref, prev_ref, next_ref, w_ref, b_ref, lg_ref, lb_ref, o_ref, pad_ref):
    first, last = geom.seq_pos(pl.program_id(0))

    def glu(ref):
        return ref[:, :CONV_CH] * jax.nn.sigmoid(ref[:, CONV_CH:])

    pad_ref[0:CONV_HALO, :] = jnp.where(first, 0.0, glu(prev_ref))
    pad_ref[CONV_HALO:CONV_HALO + TOK_TILE, :] = glu(cur_ref)
    pad_ref[CONV_HALO + TOK_TILE:, :] = jnp.where(last, 0.0, glu(next_ref))
    rows = 64
    base = CONV_HALO - CONV_K // 2
    for r0 in range(0, TOK_TILE, rows):
        acc = jnp.broadcast_to(b_ref[...], (rows, CONV_CH))
        for k in range(CONV_K):
            acc = acc + w_ref[k:k + 1, :] * pad_ref[r0 + base + k:r0 + base + k + rows, :]
        xc = acc - jnp.mean(acc, axis=-1, keepdims=True)
        var = jnp.mean(xc * xc, axis=-1, keepdims=True)
        o_ref[r0:r0 + rows, :] = _silu(xc * lax.rsqrt(var + EPS) * lg_ref[...] + lb_ref[...])


def _cconv(geom, glu, w, b, ln_g, ln_b):
    prev, nxt = _halo_specs(geom, 2 * CONV_CH, CONV_HALO)
    const = lambda a: pl.BlockSpec(a.shape, lambda i: (0,) * a.ndim)
    return pl.pallas_call(
        functools.partial(_cconv_kernel, geom),
        grid=(geom.n_tiles,),
        in_specs=[pl.BlockSpec((TOK_TILE, 2 * CONV_CH), lambda i: (i, 0)), prev, nxt,
                  const(w), const(b), const(ln_g), const(ln_b)],
        out_specs=pl.BlockSpec((TOK_TILE, CONV_CH), lambda i: (i, 0)),
        out_shape=jax.ShapeDtypeStruct((geom.n_tok, CONV_CH), f32),
        scratch_shapes=[pltpu.VMEM((TOK_TILE + 2 * CONV_HALO, CONV_CH), f32)],
        compiler_params=_cparams("arbitrary"),
        name="conformer_conv",
    )(glu, glu, glu, w, b, ln_g, ln_b)


def _softplus(x):
    return jnp.maximum(x, 0.0) + jnp.log1p(jnp.exp(-jnp.abs(x)))


def _ssd_kernel(geom, direction, *refs):
    fwd = direction == 0
    if fwd:
        (cur_ref, prev_ref, next_ref, dt_ref, init_ref, cw_ref, cb_ref, dtb_ref, alog_ref, dskip_ref,
         y_ref, fin_ref, xc_ref, pad_ref, st_ref) = refs
    else:
        xc_ref, dt_ref, init_ref, dtb_ref, alog_ref, dskip_ref, y_ref, fin_ref, st_ref = refs
    i = pl.program_id(0)
    t = i if fwd else geom.n_tiles - 1 - i
    first, last = geom.seq_pos(t)
    enter, leave = (first, last) if fwd else (last, first)
    is_lat = t >= geom.ctx_tiles

    @pl.when(enter)
    def _():
        st_ref[...] = jnp.where(is_lat, init_ref[...], 0.0)

    if fwd:
        pad_ref[0:SSM_HALO, :] = jnp.where(first, 0.0, prev_ref[...])
        pad_ref[SSM_HALO:SSM_HALO + TOK_TILE, :] = cur_ref[...]
        pad_ref[SSM_HALO + TOK_TILE:, :] = jnp.where(last, 0.0, next_ref[...])
        base = SSM_HALO - SSM_CONV_K // 2
        acc = jnp.broadcast_to(cb_ref[...], (TOK_TILE, SSM_XBC))
        for k in range(SSM_CONV_K):
            acc = acc + cw_ref[k:k + 1, :] * pad_ref[base + k:base + k + TOK_TILE, :]
        xc_ref[...] = _silu(acc)

    L = SSD_CHUNK
    a_row = -jnp.exp(alog_ref[...])
    ri = lax.broadcasted_iota(jnp.int32, (L, L), 0)
    ci = lax.broadcasted_iota(jnp.int32, (L, L), 1)
    keep = (ci <= ri) if fwd else (ci >= ri)
    tri_col = keep.astype(f32)
    tri_row = ((ri <= ci) if fwd else (ri >= ci)).astype(f32)
    lane = lax.broadcasted_iota(jnp.int32, (1, LANES), 1)
    er = lax.broadcasted_iota(jnp.int32, (LANES, SSM_WIDTH), 0)
    ec = lax.broadcasted_iota(jnp.int32, (LANES, SSM_WIDTH), 1)
    expand = (er == direction * SSM_HEADS + ec // SSM_HEAD_DIM).astype(f32)
    heads_per_group = SSM_HEADS // SSM_GROUPS
    heads_per_slab = LANES // SSM_HEAD_DIM
    tot_at = L - 1 if fwd else 0

    chunks = range(TOK_TILE // L)
    for cidx in (chunks if fwd else reversed(chunks)):
        r0 = cidx * L
        xs = xc_ref[r0:r0 + L, 0:SSM_WIDTH]
        bm = xc_ref[r0:r0 + L, SSM_WIDTH:SSM_WIDTH + LANES]
        cm = xc_ref[r0:r0 + L, SSM_WIDTH + LANES:SSM_XBC]
        dt = _softplus(dt_ref[r0:r0 + L, :] + dtb_ref[...])
        da = dt * a_row
        acum_col = jnp.dot(tri_col, da, precision=HIGHEST, preferred_element_type=f32)
        acum_row = jnp.dot(da.T, tri_row, precision=HIGHEST, preferred_element_type=f32)
        tot_row = acum_col[tot_at:tot_at + 1, :]
        tot_col = acum_row[:, tot_at:tot_at + 1]
        decay_out = jnp.exp(tot_col - acum_row)
        decay_in = jnp.exp(acum_col)
        chunk_decay = jnp.exp(tot_row)
        xdt = (xs * jnp.dot(dt, expand, precision=HIGHEST, preferred_element_type=f32)).astype(bf16)
        bt = bm.T
        bb = bm.astype(bf16)
        ys = []
        for g in range(SSM_GROUPS):
            in_group = (lane >= g * SSM_STATE) & (lane < (g + 1) * SSM_STATE)
            cg = jnp.where(in_group, cm, 0.0)
            gmat = lax.dot_general(cg.astype(bf16), bb, (((1,), (1,)), ((), ())), preferred_element_type=f32)
            for hh in range(heads_per_group):
                h = g * heads_per_group + hh
                hl = direction * SSM_HEADS + h
                slab = xdt[:, (h // heads_per_slab) * LANES:(h // heads_per_slab + 1) * LANES]
                seg = acum_col[:, hl:hl + 1] - acum_row[hl:hl + 1, :]
                m = (gmat * jnp.exp(jnp.where(keep, seg, -jnp.inf))).astype(bf16)
                y_h = jnp.dot(m, slab, preferred_element_type=f32)
                c_in = (cg * decay_in[:, hl:hl + 1]).astype(bf16)
                state = st_ref[h]
                y_h = y_h + jnp.dot(c_in, state.astype(bf16), preferred_element_type=f32)
                b_out = (bt * decay_out[hl:hl + 1, :]).astype(bf16)
                st_ref[h] = chunk_decay[:, hl:hl + 1] * state + jnp.dot(b_out, slab, preferred_element_type=f32)
                ys.append(y_h)
        for j in range(SSM_WIDTH // LANES):
            y_slab = jnp.where(lane < SSM_HEAD_DIM, ys[heads_per_slab * j], ys[heads_per_slab * j + 1])
            if fwd:
                y_slab = y_slab + dskip_ref[:, j * LANES:(j + 1) * LANES] * xs[:, j * LANES:(j + 1) * LANES]
            y_ref[r0:r0 + L, j * LANES:(j + 1) * LANES] = y_slab

    @pl.when(leave)
    def _():
        fin_ref[...] = st_ref[...]


def _ssd(geom, direction, xbc, dt, init, conv_w, conv_b, dt_bias, a_log, d_skip):
    nt = geom.n_tiles
    fwd = direction == 0
    tidx = (lambda i: i) if fwd else (lambda i: nt - 1 - i)
    per = TOK_TILE // SSM_HALO
    n_halo = geom.n_tok // SSM_HALO
    n_seq = geom.n_ctx_seq + geom.n_lat_seq
    const = lambda a: pl.BlockSpec(a.shape, lambda i: (0,) * a.ndim)
    st_block = (None, SSM_HEADS, LANES, LANES)
    tile = lambda w: pl.BlockSpec((TOK_TILE, w), lambda i: (tidx(i), 0))
    init_spec = pl.BlockSpec(st_block, lambda i: (jnp.maximum(geom.seq_index(tidx(i)) - geom.n_ctx_seq, 0), 0, 0, 0))
    out_specs = [tile(SSM_WIDTH), pl.BlockSpec(st_block, lambda i: (geom.seq_index(tidx(i)), 0, 0, 0))]
    out_shape = [jax.ShapeDtypeStruct((geom.n_tok, SSM_WIDTH), f32),
                 jax.ShapeDtypeStruct((n_seq, SSM_HEADS, LANES, LANES), f32)]
    state_scratch = pltpu.VMEM((SSM_HEADS, LANES, LANES), f32)
    if fwd:
        in_specs = [
            tile(SSM_XBC),
            pl.BlockSpec((SSM_HALO, SSM_XBC), lambda i: (jnp.maximum(i * per - 1, 0), 0)),
            pl.BlockSpec((SSM_HALO, SSM_XBC), lambda i: (jnp.minimum((i + 1) * per, n_halo - 1), 0)),
            tile(LANES), init_spec, const(conv_w), const(conv_b), const(dt_bias), const(a_log), const(d_skip),
        ]
        args = (xbc, xbc, xbc, dt, init, conv_w, conv_b, dt_bias, a_log, d_skip)
        out_specs.append(tile(SSM_XBC))
        out_shape.append(jax.ShapeDtypeStruct((geom.n_tok, SSM_XBC), f32))
        scratch = [pltpu.VMEM((TOK_TILE + 2 * SSM_HALO, SSM_XBC), f32), state_scratch]
    else:
        in_specs = [tile(SSM_XBC), tile(LANES), init_spec, const(dt_bias), const(a_log), const(d_skip)]
        args = (xbc, dt, init, dt_bias, a_log, d_skip)
        scratch = [state_scratch]
    return pl.pallas_call(
        functools.partial(_ssd_kernel, geom, direction),
        grid=(nt,),
        in_specs=in_specs,
        out_specs=out_specs,
        out_shape=out_shape,
        scratch_shapes=scratch,
        compiler_params=_cparams("arbitrary"),
        name="ssd_scan",
    )(*args)


def _pad_states(s):
    return jnp.tile(jnp.swapaxes(s, -1, -2), (1, 1, LANES // SSM_STATE, LANES // SSM_HEAD_DIM))


def _unpad_states(fin):
    hpg = SSM_HEADS // SSM_GROUPS
    outs = []
    for h in range(SSM_HEADS):
        g, side = h // hpg, h % (LANES // SSM_HEAD_DIM)
        q = fin[:, h, g * SSM_STATE:(g + 1) * SSM_STATE, side * SSM_HEAD_DIM:(side + 1) * SSM_HEAD_DIM]
        outs.append(jnp.swapaxes(q, -1, -2))
    return jnp.stack(outs, axis=1)


def _out_proj_kernel(geom, actx_ref, alat_ref, conv_ref, yf_ref, yb_ref, z_ref, x_ref, mod_ref, sg_ref, w_ref,
                     g2_ref, rwh_ref, rwl_ref, rb_ref, xo_ref, h2_ref, ti_ref, gw_ref, cnt_ref):
    attn = jnp.where(pl.program_id(0) < geom.ctx_tiles, actx_ref[...], alat_ref[...])
    gated = (yf_ref[...] + yb_ref[...]) * _silu(z_ref[...])
    ssm = gated * lax.rsqrt(jnp.mean(gated * gated, axis=-1, keepdims=True) + EPS) * sg_ref[...]
    o1 = ATT_WIDTH
    o2 = ATT_WIDTH + CONV_CH
    mix = (jnp.dot(attn.astype(bf16), w_ref[0:o1, :], preferred_element_type=f32)
           + jnp.dot(conv_ref[...].astype(bf16), w_ref[o1:o2, :], preferred_element_type=f32)
           + jnp.dot(ssm.astype(bf16), w_ref[o2:, :], preferred_element_type=f32))
    x = x_ref[...] + mod_ref[2:3, :] * mix
    xo_ref[...] = x
    y = x * lax.rsqrt(jnp.mean(x * x, axis=-1, keepdims=True) + EPS) * g2_ref[...]
    h2 = y * (1.0 + mod_ref[4:5, :]) + mod_ref[3:4, :]
    h2_ref[...] = h2
    h_hi = h2.astype(bf16)
    h_lo = (h2 - h_hi.astype(f32)).astype(bf16)
    logits = (jnp.dot(h_hi, rwh_ref[...], preferred_element_type=f32)
              + jnp.dot(h_lo, rwh_ref[...], preferred_element_type=f32)
              + jnp.dot(h_hi, rwl_ref[...], preferred_element_type=f32)) + rb_ref[...]
    lane = lax.broadcasted_iota(jnp.int32, (1, LANES), 1)
    lane_f = lane.astype(f32)
    lg = jnp.where(lane < N_EXPERTS, logits, -jnp.inf)
    idx_out = jnp.zeros(logits.shape, jnp.int32)
    gate_out = jnp.zeros(logits.shape, f32)
    top0 = None
    denom = None
    picks = []
    for k in range(TOP_K):
        m = jnp.max(lg, axis=-1, keepdims=True)
        idx = jnp.min(jnp.where(lg == m, lane_f, float(LANES)), axis=-1, keepdims=True).astype(jnp.int32)
        picked = lane == idx
        picks.append(picked)
        lg = jnp.where(picked, -jnp.inf, lg)
        if k == 0:
            top0 = m
        e = jnp.exp(m - top0)
        denom = e if k == 0 else denom + e
        idx_out = jnp.where(lane == k, idx, idx_out)
        gate_out = jnp.where(lane == k, e, gate_out)
    gw_ref[...] = gate_out / denom

    @pl.when(pl.program_id(0) == 0)
    def _():
        cnt_ref[...] = jnp.zeros_like(cnt_ref)

    chosen = picks[0] | picks[1] | picks[2] | picks[3]
    sel = jnp.where(chosen, 1.0, 0.0)
    ri = lax.broadcasted_iota(jnp.int32, (TOK_TILE, TOK_TILE), 0)
    ci = lax.broadcasted_iota(jnp.int32, (TOK_TILE, TOK_TILE), 1)
    before = jnp.where(ci < ri, 1.0, 0.0).astype(bf16)
    prior = jnp.dot(before, sel.astype(bf16), preferred_element_type=f32) + cnt_ref[...]
    for k in range(TOP_K):
        rank = jnp.sum(jnp.where(picks[k], prior, 0.0), axis=-1, keepdims=True).astype(jnp.int32)
        idx_out = jnp.where(lane == TOP_K + k, rank, idx_out)
    ti_ref[...] = idx_out
    cnt_ref[...] = cnt_ref[...] + jnp.sum(sel, axis=0, keepdims=True)


def _out_proj(geom, layer, attn_ctx, attn_lat, conv, y_f, y_b, z, x, mod, ssm_norm_g, w_out, norm2_g, router_w,
              router_b):
    n = geom.n_tok
    tile = lambda w: pl.BlockSpec((TOK_TILE, w), lambda i: (i, 0))
    const = lambda a: pl.BlockSpec(a.shape, lambda i: (0,) * a.ndim)
    nct = geom.ctx_tiles
    rw_hi = router_w.astype(bf16)
    rw_lo = (router_w - rw_hi.astype(f32)).astype(bf16)
    return pl.pallas_call(
        functools.partial(_out_proj_kernel, geom),
        grid=(geom.n_tiles,),
        in_specs=[
            pl.BlockSpec((TOK_TILE, ATT_WIDTH), lambda i: (jnp.minimum(i, nct - 1), 0)),
            pl.BlockSpec((TOK_TILE, ATT_WIDTH), lambda i: (jnp.maximum(i - nct, 0), 0)),
            tile(CONV_CH), tile(SSM_WIDTH), tile(SSM_WIDTH), tile(SSM_WIDTH), tile(D_MODEL),
            pl.BlockSpec((None, None, 6, D_MODEL), lambda i: (layer, geom.cond_row(i), 0, 0)),
            const(ssm_norm_g), const(w_out), const(norm2_g), const(rw_hi), const(rw_lo), const(router_b),
        ],
        out_specs=[tile(D_MODEL), tile(D_MODEL), tile(LANES), tile(LANES),
                   pl.BlockSpec((1, LANES), lambda i: (0, 0))],
        out_shape=[jax.ShapeDtypeStruct((n, D_MODEL), f32), jax.ShapeDtypeStruct((n, D_MODEL), f32),
                   jax.ShapeDtypeStruct((n, LANES), jnp.int32), jax.ShapeDtypeStruct((n, LANES), f32),
                   jax.ShapeDtypeStruct((1, LANES), f32)],
        compiler_params=_cparams("arbitrary"),
        name="out_proj",
    )(attn_ctx, attn_lat, conv, y_f, y_b, z, x, mod, ssm_norm_g, w_out, norm2_g, rw_hi, rw_lo, router_b)


def _moe_kernel(be_ref, nused_ref, rs_ref, x_ref, wg_ref, bg_ref, wu_ref, bu_ref, wd_ref, bd_ref, o_ref,
                wg_s, wu_s, wd_s):
    i = pl.program_id(0)
    prev_e = be_ref[jnp.maximum(i - 1, 0)]
    new_expert = (i == 0) | (be_ref[i] != prev_e)

    @pl.when(new_expert)
    def _():
        wg_s[...] = wg_ref[0].astype(bf16)
        wu_s[...] = wu_ref[0].astype(bf16)
        wd_s[...] = wd_ref[0].astype(bf16)

    @pl.when(i < nused_ref[0])
    def _():
        x = x_ref[...].astype(bf16)
        acc = None
        for c0 in range(0, wg_s.shape[1], MOE_FF_CHUNK):
            sl = slice(c0, c0 + MOE_FF_CHUNK)
            g = jnp.dot(x, wg_s[:, sl], preferred_element_type=f32) + bg_ref[0, :, sl]
            u = jnp.dot(x, wu_s[:, sl], preferred_element_type=f32) + bu_ref[0, :, sl]
            g = jnp.minimum(g, SWIGLU_LIMIT)
            u = jnp.clip(u, -SWIGLU_LIMIT, SWIGLU_LIMIT)
            act = g * jax.nn.sigmoid(SWIGLU_ALPHA * g) * (u + 1.0)
            part = jnp.dot(act.astype(bf16), wd_s[sl, :], preferred_element_type=f32)
            acc = part if acc is None else acc + part
        o_ref[...] = acc + bd_ref[0]

    @pl.when(i >= nused_ref[0])
    def _():
        o_ref[...] = jnp.zeros_like(o_ref)


def _moe(layer, xs, block_e, n_used, row_start, w_gate, b_gate, w_up, b_up, w_down, b_down):
    d = xs.shape[1]
    n_blocks = block_e.shape[0]
    d_ff = w_gate.shape[-1]
    w_spec = lambda r, c: pl.BlockSpec((None, 1, r, c), lambda i, be, nu, rs: (layer, be[i], 0, 0))
    b_spec = lambda c: pl.BlockSpec((None, 1, 1, c), lambda i, be, nu, rs: (layer, be[i], 0, 0))
    depth, n_e = b_gate.shape[:2]
    return pl.pallas_call(
        _moe_kernel,
        grid_spec=pltpu.PrefetchScalarGridSpec(
            num_scalar_prefetch=3,
            grid=(n_blocks,),
            in_specs=[pl.BlockSpec((pl.Element(MOE_TILE), pl.Element(d)),
                                   lambda i, be, nu, rs: (rs[i] * SUBLANES, 0)),
                      w_spec(d, d_ff), b_spec(d_ff), w_spec(d, d_ff), b_spec(d_ff), w_spec(d_ff, d), b_spec(d)],
            out_specs=pl.BlockSpec((MOE_TILE, d), lambda i, be, nu, rs: (i, 0)),
            scratch_shapes=[pltpu.VMEM((d, d_ff), bf16), pltpu.VMEM((d, d_ff), bf16), pltpu.VMEM((d_ff, d), bf16)],
        ),
        out_shape=jax.ShapeDtypeStruct((n_blocks * MOE_TILE, d), f32),
        compiler_params=_cparams("arbitrary"),
        name="moe_experts",
    )(block_e, n_used, row_start, xs, w_gate, b_gate.reshape(depth, n_e, 1, d_ff), w_up,
      b_up.reshape(depth, n_e, 1, d_ff), w_down, b_down.reshape(depth, n_e, 1, d))


def _expert_table(expert_ids, table):
    onehot = expert_ids[:, None] == jnp.arange(N_EXPERTS, dtype=jnp.int32)[None, :]
    return jnp.sum(jnp.where(onehot, table[None, :], 0), axis=1)


def _route(top_idx, rank, counts, n_blocks):
    n = top_idx.shape[0]
    n_assign = n * TOP_K
    i32 = jnp.int32
    flat_e = top_idx.reshape(-1)
    flat_rank = rank.reshape(-1)
    experts = jnp.arange(N_EXPERTS, dtype=i32)
    fill = SUBLANES - 1
    grp_size = (counts + fill) // SUBLANES * SUBLANES
    grp_start = jnp.cumsum(grp_size) - grp_size
    nb = (counts + MOE_TILE - 1) // MOE_TILE
    blk_end = jnp.cumsum(nb)
    blk_start = blk_end - nb
    n_sorted = n_assign + N_EXPERTS * fill
    j = jnp.arange(fill, dtype=i32)
    filler_pos = jnp.where(j[None, :] < (grp_size - counts)[:, None], (grp_start + counts)[:, None] + j[None, :],
                           n_sorted)
    pos = jnp.concatenate([_expert_table(flat_e, grp_start) + flat_rank, filler_pos.reshape(-1)])
    tok = jnp.concatenate([jnp.arange(n_assign, dtype=i32) // TOP_K, jnp.zeros((N_EXPERTS * fill,), i32)])
    assert n < TOKEN_KEY_RANGE and (n_sorted + 1) * TOKEN_KEY_RANGE <= 2 ** 32
    keyed = lax.sort(pos.astype(jnp.uint32) * TOKEN_KEY_RANGE + tok.astype(jnp.uint32))
    sorted_tok = (keyed % TOKEN_KEY_RANGE).astype(i32)
    src_tok = jnp.concatenate([sorted_tok, jnp.zeros((MOE_TILE,), i32)])
    dest = _expert_table(flat_e, blk_start * MOE_TILE) + flat_rank
    blk = jnp.arange(n_blocks, dtype=i32)
    block_e = jnp.minimum(jnp.sum((blk_end[None, :] <= blk[:, None]).astype(i32), axis=1), N_EXPERTS - 1)
    n_used = blk_end[-1:].astype(i32)
    row_start = jnp.where(blk < n_used[0], blk * MOE_TILE - _expert_table(block_e, blk_start * MOE_TILE - grp_start), 0)
    return src_tok, dest.reshape(n, TOP_K), block_e, n_used, row_start // SUBLANES


def _combine_kernel(final, x_ref, rows_ref, gw_ref, mod_ref, *rest):
    ffn = rows_ref[0] * gw_ref[:, 0:1]
    for k in range(1, TOP_K):
        ffn = ffn + rows_ref[k] * gw_ref[:, k:k + 1]
    x = x_ref[...] + mod_ref[5:6, :] * ffn
    if final:
        fg_ref, y_ref = rest
        y_ref[...] = x * lax.rsqrt(jnp.mean(x * x, axis=-1, keepdims=True) + EPS) * fg_ref[...]
    else:
        rest[0][...] = x


def _combine(geom, layer, x, rows, gate_w, mod, final_g):
    n = geom.n_tok
    final = final_g is not None
    tile = pl.BlockSpec((TOK_TILE, D_MODEL), lambda i: (i, 0))
    in_specs = [
        tile,
        pl.BlockSpec((TOP_K, TOK_TILE, D_MODEL), lambda i: (0, i, 0)),
        pl.BlockSpec((TOK_TILE, LANES), lambda i: (i, 0)),
        pl.BlockSpec((None, None, 6, D_MODEL), lambda i: (layer, geom.cond_row(i), 0, 0)),
    ]
    args = [x, rows, gate_w, mod]
    if final:
        in_specs.append(pl.BlockSpec(final_g.shape, lambda i: (0, 0)))
        args.append(final_g)
    return pl.pallas_call(
        functools.partial(_combine_kernel, final),
        grid=(geom.n_tiles,),
        in_specs=in_specs,
        out_specs=tile,
        out_shape=jax.ShapeDtypeStruct((n, D_MODEL), f32),
        compiler_params=_cparams("arbitrary"),
        name="moe_combine",
    )(*args)


def _pad_lanes(a, width=LANES):
    return jnp.pad(a, [(0, 0)] * (a.ndim - 1) + [(0, width - a.shape[-1])])


def kernel(x_prompt, x_sample, c, cache_k, cache_v, state_ssm, c_ctx, norm1_g, norm2_g, w_ada, b_ada, w_in, w_out, attn_lambda, attn_subln_g, conv_dw_w, conv_dw_b, conv_ln_g, conv_ln_b, ssm_conv_w, ssm_conv_b, ssm_dt_bias, ssm_a_log, ssm_d, ssm_norm_g, router_w, router_b, w_gate, b_gate, w_up, b_up, w_down, b_down, final_g):
    n_ctx_seq, ctx_len, d = x_prompt.shape
    n_lat_seq, lat_len, _ = x_sample.shape
    past_len = cache_k.shape[-2]
    geom = _Geom(n_ctx_seq, ctx_len, n_lat_seq, lat_len)
    n = geom.n_tok

    x = jnp.concatenate([x_prompt.reshape(geom.n_ctx, d), x_sample.reshape(geom.n_lat, d)], axis=0)
    cond = jnp.concatenate([c_ctx[None, :], c, jnp.zeros((COND_ROWS - 1 - n_lat_seq, d), f32)], axis=0)
    mod = _ada(cond, w_ada, b_ada).reshape(DEPTH, COND_ROWS, 6, d)
    tables = _rope_tables(geom)

    n_assign = n * TOP_K
    n_blocks = -(-n_assign // MOE_TILE) + N_EXPERTS

    new_k, new_v, new_s = [], [], []
    y_final = None
    for l in range(DEPTH):
        w_main = w_in[l, :, :N_MAIN].astype(bf16)
        w_dt = _pad_lanes(w_in[l, :, N_MAIN:]).astype(bf16)
        q, k, v, glu, z, xbc, dt = _in_proj(geom, l, x, mod, norm1_g[l][None, :], w_main, w_dt, tables)

        lam_init = 0.8 - 0.6 * math.exp(-0.3 * l)
        subln = jnp.tile(attn_subln_g[l], LANES // ATT_V_DIM)[None, :]
        k_cache = cache_k[:, l].transpose(0, 3, 1, 2, 4).reshape(n_lat_seq, past_len, QK_WIDTH)
        v_cache = cache_v[:, l].transpose(0, 2, 1, 3).reshape(n_lat_seq, past_len, ATT_WIDTH)
        k_all = jnp.concatenate([k[geom.n_ctx:].reshape(n_lat_seq, lat_len, QK_WIDTH), k_cache], axis=1).astype(bf16)
        v_all = jnp.concatenate([v[geom.n_ctx:].reshape(n_lat_seq, lat_len, ATT_WIDTH), v_cache], axis=1).astype(bf16)
        attn_ctx = _attention(lam_init, attn_lambda[l], subln, q, k, v, n_ctx_seq, ctx_len, ctx_len, 0, False)
        attn_lat = _attention(lam_init, attn_lambda[l], subln, q, k_all, v_all, n_lat_seq, lat_len,
                              lat_len + past_len, geom.n_ctx, True)

        conv = _cconv(geom, glu, conv_dw_w[l], conv_dw_b[l][None, :], conv_ln_g[l][None, :], conv_ln_b[l][None, :])

        dt_bias = _pad_lanes(ssm_dt_bias[l].reshape(1, 2 * SSM_HEADS))
        a_log = _pad_lanes(ssm_a_log[l].reshape(1, 2 * SSM_HEADS))
        d_skip = jnp.repeat(ssm_d[l], SSM_HEAD_DIM)[None, :]
        conv_w, conv_b = ssm_conv_w[l], ssm_conv_b[l][None, :]
        y_f, fin_f, xbc_act = _ssd(geom, 0, xbc, dt, _pad_states(state_ssm[:, l, 0]), conv_w, conv_b, dt_bias, a_log,
                                   d_skip)
        y_b, fin_b = _ssd(geom, 1, xbc_act, dt, _pad_states(state_ssm[:, l, 1]), conv_w, conv_b, dt_bias, a_log, d_skip)
        ys = [y_f, y_b]
        fins = [_unpad_states(fin_f[:n_ctx_seq]), _unpad_states(fin_b[:n_ctx_seq])]

        x, h2, routed, gate_w, totals = _out_proj(
            geom, l, attn_ctx, attn_lat, conv, ys[0], ys[1], z, x, mod, ssm_norm_g[l][None, :], w_out[l].astype(bf16),
            norm2_g[l][None, :], _pad_lanes(router_w[l]), _pad_lanes(router_b[l][None, :]))

        src_tok, dest, block_e, n_used, row_start = _route(
            routed[:, :TOP_K], routed[:, TOP_K:2 * TOP_K], totals[0, :N_EXPERTS].astype(jnp.int32), n_blocks)
        xs = jnp.take(h2, src_tok, axis=0, mode="clip")
        eo = _moe(l, xs, block_e, n_used, row_start, w_gate, b_gate, w_up, b_up, w_down, b_down)
        rows = jnp.take(eo, dest.T.reshape(-1), axis=0, mode="clip").reshape(TOP_K, n, d)
        if l == DEPTH - 1:
            y_final = _combine(geom, l, x, rows, gate_w, mod, final_g[None, :])
        else:
            x = _combine(geom, l, x, rows, gate_w, mod, None)

        new_k.append(k[:geom.n_ctx].reshape(n_ctx_seq, ctx_len, ATT_HEADS, 2, ATT_QK_DIM).transpose(0, 2, 3, 1, 4))
        new_v.append(v[:geom.n_ctx].reshape(n_ctx_seq, ctx_len, ATT_HEADS, ATT_V_DIM).transpose(0, 2, 1, 3))
        new_s.append(jnp.stack(fins, axis=1))

    y_prompt = y_final[:geom.n_ctx].reshape(n_ctx_seq, ctx_len, d)
    y_sample = y_final[geom.n_ctx:].reshape(n_lat_seq, lat_len, d)
    return (y_prompt, y_sample, jnp.stack(new_k, axis=1), jnp.stack(new_v, axis=1), jnp.stack(new_s, axis=1))
```

```python
import functools
import math

import numpy as np
import jax
import jax.numpy as jnp
from jax import lax
from jax.experimental import pallas as pl
from jax.experimental.pallas import tpu as pltpu

f32 = jnp.float32
bf16 = jnp.bfloat16
HIGHEST = lax.Precision.HIGHEST

D_MODEL = 1024
DEPTH = 2
GRID_W = 64
ATT_HEADS = 4
ATT_QK_DIM = 32
ATT_V_DIM = 64
ATT_WIDTH = 256
QK_WIDTH = 256
ROPE_AXIS_DIM = 16
ROPE_BASE = 10000.0
CONV_CH = 256
CONV_K = 31
SSM_WIDTH = 512
SSM_HEAD_DIM = 64
SSM_HEADS = 8
SSM_GROUPS = 2
SSM_STATE = 64
SSM_CONV_K = 5
SSM_XBC = 768
N_MAIN = 2560
N_EXPERTS = 32
TOP_K = 4
SWIGLU_LIMIT = 7.0
SWIGLU_ALPHA = 1.702
EPS = 1e-6

LANES = 128
SUBLANES = 8
TOK_TILE = 256
SSD_CHUNK = 128
CONV_HALO = 16
SSM_HALO = 8
MOE_TILE = 512
MOE_FF_CHUNK = 256
MOE_ROW_STEP = 128
TOKEN_KEY_RANGE = 2 ** 15
COND_ROWS = 16
VMEM_LIMIT = 56 * 1024 * 1024


def _cparams(*sem):
    return pltpu.CompilerParams(dimension_semantics=sem, vmem_limit_bytes=VMEM_LIMIT)


def _silu(x):
    return x * jax.nn.sigmoid(x)


def _ada_kernel(cond_ref, w_ref, b_ref, o_ref):
    s = _silu(cond_ref[...]).astype(bf16)
    o_ref[0] = jnp.dot(s, w_ref[0].astype(bf16), preferred_element_type=f32) + b_ref[0]


def _ada(cond, w_ada, b_ada):
    depth, d, n6 = w_ada.shape
    tn = 1536
    return pl.pallas_call(
        _ada_kernel,
        grid=(depth, n6 // tn),
        in_specs=[
            pl.BlockSpec((COND_ROWS, d), lambda l, j: (0, 0)),
            pl.BlockSpec((1, d, tn), lambda l, j: (l, 0, j)),
            pl.BlockSpec((1, 1, tn), lambda l, j: (l, 0, j)),
        ],
        out_specs=pl.BlockSpec((1, COND_ROWS, tn), lambda l, j: (l, 0, j)),
        out_shape=jax.ShapeDtypeStruct((depth, COND_ROWS, n6), f32),
        compiler_params=_cparams("arbitrary", "arbitrary"),
        name="ada",
    )(cond, w_ada, b_ada.reshape(depth, 1, n6))


class _Geom:
    def __init__(self, n_ctx_seq, ctx_len, n_lat_seq, lat_len):
        self.n_ctx_seq, self.ctx_len, self.n_lat_seq, self.lat_len = n_ctx_seq, ctx_len, n_lat_seq, lat_len
        self.n_ctx = n_ctx_seq * ctx_len
        self.n_lat = n_lat_seq * lat_len
        self.n_tok = self.n_ctx + self.n_lat
        assert ctx_len % TOK_TILE == 0 and lat_len % TOK_TILE == 0
        self.ctx_tiles = self.n_ctx // TOK_TILE
        self.tiles_per_ctx = ctx_len // TOK_TILE
        self.tiles_per_lat = lat_len // TOK_TILE
        self.n_tiles = self.n_tok // TOK_TILE

    def cond_row(self, i):
        return jnp.where(i < self.ctx_tiles, 0, 1 + (i - self.ctx_tiles) // self.tiles_per_lat)

    def seq_pos(self, i):
        j_ctx = i % self.tiles_per_ctx
        j_lat = (i - self.ctx_tiles) % self.tiles_per_lat
        is_ctx = i < self.ctx_tiles
        first = jnp.where(is_ctx, j_ctx == 0, j_lat == 0)
        last = jnp.where(is_ctx, j_ctx == self.tiles_per_ctx - 1, j_lat == self.tiles_per_lat - 1)
        return first, last

    def seq_index(self, i):
        return jnp.where(i < self.ctx_tiles, i // self.tiles_per_ctx,
                         self.n_ctx_seq + (i - self.ctx_tiles) // self.tiles_per_lat)


def _rope_tables(geom):
    t = np.arange(geom.lat_len)
    row = (t // GRID_W).astype(np.float64)
    col = (t % GRID_W).astype(np.float64)
    inv_freq = 1.0 / (ROPE_BASE ** (np.arange(0, ROPE_AXIS_DIM, 2, dtype=np.float64) / ROPE_AXIS_DIM))
    ang_r = row[:, None] * inv_freq[None, :]
    ang_c = col[:, None] * inv_freq[None, :]
    ang = np.concatenate([ang_r, ang_r, ang_c, ang_c], axis=-1)
    cos, sin = np.cos(ang), np.sin(ang)
    half = ROPE_AXIS_DIM // 2
    lane = np.arange(ATT_QK_DIM)
    first_half = (lane % ROPE_AXIS_DIM) < half
    sin_a = np.where(first_half[None, :], -sin, 0.0)
    sin_b = np.where(first_half[None, :], 0.0, sin)
    rep = LANES // ATT_QK_DIM

    def full(tab, ident):
        tab = np.tile(tab, (1, rep))
        return jnp.asarray(np.concatenate([np.full((TOK_TILE, LANES), ident), tab], axis=0), f32)

    return full(cos, 1.0), full(sin_a, 0.0), full(sin_b, 0.0)


def _in_proj_kernel(x_ref, mod_ref, g_ref, w_ref, wdt_ref, cos_ref, sa_ref, sb_ref,
                    q_ref, k_ref, v_ref, glu_ref, z_ref, xbc_ref, dt_ref):
    x = x_ref[...]
    y = x * lax.rsqrt(jnp.mean(x * x, axis=-1, keepdims=True) + EPS) * g_ref[...]
    h = (y * (1.0 + mod_ref[1:2, :]) + mod_ref[0:1, :]).astype(bf16)
    p = jnp.dot(h, w_ref[...], preferred_element_type=f32)
    dt_ref[...] = jnp.dot(h, wdt_ref[...], preferred_element_type=f32)
    cos, sa, sb = cos_ref[...], sa_ref[...], sb_ref[...]
    half = ROPE_AXIS_DIM // 2

    def rope(u):
        return u * cos + pltpu.roll(u, LANES - half, 1) * sa + pltpu.roll(u, half, 1) * sb

    scale = ATT_QK_DIM ** -0.5 * math.log2(math.e)
    for j in range(QK_WIDTH // LANES):
        lo = j * LANES
        q_ref[:, lo:lo + LANES] = (rope(p[:, lo:lo + LANES]) * scale).astype(bf16)
        k_ref[:, lo:lo + LANES] = rope(p[:, QK_WIDTH + lo:QK_WIDTH + lo + LANES])
    o = 2 * QK_WIDTH
    v_ref[...] = p[:, o:o + ATT_WIDTH]
    o += ATT_WIDTH
    glu_ref[...] = p[:, o:o + 2 * CONV_CH]
    o += 2 * CONV_CH
    z_ref[...] = p[:, o:o + SSM_WIDTH]
    o += SSM_WIDTH
    xbc_ref[...] = p[:, o:o + SSM_XBC]


def _in_proj(geom, layer, x, mod, norm_g, w_main, w_dt, tables):
    n = geom.n_tok
    tile = lambda w: pl.BlockSpec((TOK_TILE, w), lambda i: (i, 0))
    const = lambda a: pl.BlockSpec(a.shape, lambda i: (0,) * a.ndim)
    tab_spec = pl.BlockSpec(
        (TOK_TILE, LANES),
        lambda i: (jnp.where(i < geom.ctx_tiles, 0, 1 + (i - geom.ctx_tiles) % geom.tiles_per_lat), 0))
    widths = (QK_WIDTH, QK_WIDTH, ATT_WIDTH, 2 * CONV_CH, SSM_WIDTH, SSM_XBC, LANES)
    dtypes = (bf16, f32, f32, f32, f32, f32, f32)
    return pl.pallas_call(
        _in_proj_kernel,
        grid=(geom.n_tiles,),
        in_specs=[
            tile(D_MODEL),
            pl.BlockSpec((None, None, 6, D_MODEL), lambda i: (layer, geom.cond_row(i), 0, 0)),
            const(norm_g), const(w_main), const(w_dt), tab_spec, tab_spec, tab_spec,
        ],
        out_specs=[tile(w) for w in widths],
        out_shape=[jax.ShapeDtypeStruct((n, w), dt) for w, dt in zip(widths, dtypes)],
        compiler_params=_cparams("arbitrary"),
        name="in_proj",
    )(x, mod, norm_g, w_main, w_dt, *tables)


def _attn_kernel(lam_init, al_ref, g_ref, q_ref, k_ref, v_ref, o_ref):
    al = al_ref[...]
    lam = (jnp.exp(jnp.sum(al[0:1] * al[1:2], axis=-1, keepdims=True))
           - jnp.exp(jnp.sum(al[2:3] * al[3:4], axis=-1, keepdims=True)) + lam_init)
    lane = lax.broadcasted_iota(jnp.int32, (1, LANES), 1)
    heads_per_slab = LANES // ATT_V_DIM
    for j in range(ATT_WIDTH // LANES):
        sl = slice(j * LANES, (j + 1) * LANES)
        qs = q_ref[:, sl]
        ks = k_ref[:, sl].astype(bf16)
        vs = v_ref[:, sl].astype(bf16)
        a_heads = []
        for hl in range(heads_per_slab):
            own = (lane >= hl * ATT_V_DIM) & (lane < (hl + 1) * ATT_V_DIM)
            vh = jnp.where(own, vs, jnp.ones_like(vs))
            sum_lane = (1 - hl) * ATT_V_DIM
            comps = []
            for c in range(2):
                lo = hl * ATT_V_DIM + c * ATT_QK_DIM
                qm = jnp.where((lane >= lo) & (lane < lo + ATT_QK_DIM), qs, jnp.zeros_like(qs))
                s = lax.dot_general(qm, ks, (((1,), (1,)), ((), ())), preferred_element_type=f32)
                e = jnp.exp2(s - jnp.max(s, axis=-1, keepdims=True))
                pv = jnp.dot(e.astype(bf16), vh, preferred_element_type=f32)
                comps.append(pv / pv[:, sum_lane:sum_lane + 1])
            a_heads.append(comps[0] - lam * comps[1])
        a = jnp.where(lane < ATT_V_DIM, a_heads[0], a_heads[1])
        a2 = a * a
        ms0 = jnp.sum(jnp.where(lane < ATT_V_DIM, a2, 0.0), axis=-1, keepdims=True)
        ms1 = jnp.sum(jnp.where(lane < ATT_V_DIM, 0.0, a2), axis=-1, keepdims=True)
        ms = jnp.where(lane < ATT_V_DIM, ms0, ms1) * (1.0 / ATT_V_DIM)
        o_ref[:, sl] = a * lax.rsqrt(ms + EPS) * g_ref[...] * (1.0 - lam_init)


def _attention(lam_init, attn_lambda, subln_g, q, k, v, n_seq, q_len, k_len, q_row0, kv_is_3d):
    tq = TOK_TILE
    qb = q_len // tq
    q0 = q_row0 // tq
    if kv_is_3d:
        kv_spec = pl.BlockSpec((None, k_len, ATT_WIDTH), lambda b, i: (b, 0, 0))
    else:
        kv_spec = pl.BlockSpec((k_len, ATT_WIDTH), lambda b, i: (q_row0 // k_len + b, 0))
    return pl.pallas_call(
        functools.partial(_attn_kernel, lam_init),
        grid=(n_seq, qb),
        in_specs=[
            pl.BlockSpec(attn_lambda.shape, lambda b, i: (0, 0)),
            pl.BlockSpec(subln_g.shape, lambda b, i: (0, 0)),
            pl.BlockSpec((tq, ATT_WIDTH), lambda b, i: (q0 + b * qb + i, 0)),
            kv_spec, kv_spec,
        ],
        out_specs=pl.BlockSpec((tq, ATT_WIDTH), lambda b, i: (b * qb + i, 0)),
        out_shape=jax.ShapeDtypeStruct((n_seq * q_len, ATT_WIDTH), f32),
        compiler_params=_cparams("arbitrary", "arbitrary"),
        name="attention",
    )(attn_lambda, subln_g, q, k, v)


def _halo_specs(geom, width, halo):
    per = TOK_TILE // halo
    n_halo = geom.n_tok // halo
    prev = pl.BlockSpec((halo, width), lambda i: (jnp.maximum(i * per - 1, 0), 0))
    nxt = pl.BlockSpec((halo, width), lambda i: (jnp.minimum((i + 1) * per, n_halo - 1), 0))
    return prev, nxt


def _cconv_kernel(geom, cur_ref, prev_ref, next_ref, w_ref, b_ref, lg_ref, lb_ref, o_ref, pad_ref):
    first, last = geom.seq_pos(pl.program_id(0))

    def glu(ref):
        return ref[:, :CONV_CH] * jax.nn.sigmoid(ref[:, CONV_CH:])

    pad_ref[0:CONV_HALO, :] = jnp.where(first, 0.0, glu(prev_ref))
    pad_ref[CONV_HALO:CONV_HALO + TOK_TILE, :] = glu(cur_ref)
    pad_ref[CONV_HALO + TOK_TILE:, :] = jnp.where(last, 0.0, glu(next_ref))
    rows = 64
    base = CONV_HALO - CONV_K // 2
    for r0 in range(0, TOK_TILE, rows):
        acc = jnp.broadcast_to(b_ref[...], (rows, CONV_CH))
        for k in range(CONV_K):
            acc = acc + w_ref[k:k + 1, :] * pad_ref[r0 + base + k:r0 + base + k + rows, :]
        xc = acc - jnp.mean(acc, axis=-1, keepdims=True)
        var = jnp.mean(xc * xc, axis=-1, keepdims=True)
        o_ref[r0:r0 + rows, :] = _silu(xc * lax.rsqrt(var + EPS) * lg_ref[...] + lb_ref[...])


def _cconv(geom, glu, w, b, ln_g, ln_b):
    prev, nxt = _halo_specs(geom, 2 * CONV_CH, CONV_HALO)
    const = lambda a: pl.BlockSpec(a.shape, lambda i: (0,) * a.ndim)
    return pl.pallas_call(
        functools.partial(_cconv_kernel, geom),
        grid=(geom.n_tiles,),
        in_specs=[pl.BlockSpec((TOK_TILE, 2 * CONV_CH), lambda i: (i, 0)), prev, nxt,
                  const(w), const(b), const(ln_g), const(ln_b)],
        out_specs=pl.BlockSpec((TOK_TILE, CONV_CH), lambda i: (i, 0)),
        out_shape=jax.ShapeDtypeStruct((geom.n_tok, CONV_CH), f32),
        scratch_shapes=[pltpu.VMEM((TOK_TILE + 2 * CONV_HALO, CONV_CH), f32)],
        compiler_params=_cparams("arbitrary"),
        name="conformer_conv",
    )(glu, glu, glu, w, b, ln_g, ln_b)


def _softplus(x):
    return jnp.maximum(x, 0.0) + jnp.log1p(jnp.exp(-jnp.abs(x)))


def _ssd_kernel(geom, direction, *refs):
    fwd = direction == 0
    if fwd:
        (cur_ref, prev_ref, next_ref, dt_ref, init_ref, cw_ref, cb_ref, dtb_ref, alog_ref, dskip_ref,
         y_ref, fin_ref, xc_ref, pad_ref, st_ref) = refs
    else:
        xc_ref, dt_ref, init_ref, dtb_ref, alog_ref, dskip_ref, y_ref, fin_ref, st_ref = refs
    i = pl.program_id(0)
    t = i if fwd else geom.n_tiles - 1 - i
    first, last = geom.seq_pos(t)
    enter, leave = (first, last) if fwd else (last, first)
    is_lat = t >= geom.ctx_tiles

    @pl.when(enter)
    def _():
        st_ref[...] = jnp.where(is_lat, init_ref[...], 0.0)

    if fwd:
        pad_ref[0:SSM_HALO, :] = jnp.where(first, 0.0, prev_ref[...])
        pad_ref[SSM_HALO:SSM_HALO + TOK_TILE, :] = cur_ref[...]
        pad_ref[SSM_HALO + TOK_TILE:, :] = jnp.where(last, 0.0, next_ref[...])
        base = SSM_HALO - SSM_CONV_K // 2
        acc = jnp.broadcast_to(cb_ref[...], (TOK_TILE, SSM_XBC))
        for k in range(SSM_CONV_K):
            acc = acc + cw_ref[k:k + 1, :] * pad_ref[base + k:base + k + TOK_TILE, :]
        xc_ref[...] = _silu(acc)

    L = SSD_CHUNK
    a_row = -jnp.exp(alog_ref[...])
    ri = lax.broadcasted_iota(jnp.int32, (L, L), 0)
    ci = lax.broadcasted_iota(jnp.int32, (L, L), 1)
    keep = (ci <= ri) if fwd else (ci >= ri)
    tri_col = keep.astype(f32)
    tri_row = ((ri <= ci) if fwd else (ri >= ci)).astype(f32)
    lane = lax.broadcasted_iota(jnp.int32, (1, LANES), 1)
    er = lax.broadcasted_iota(jnp.int32, (LANES, SSM_WIDTH), 0)
    ec = lax.broadcasted_iota(jnp.int32, (LANES, SSM_WIDTH), 1)
    expand = (er == direction * SSM_HEADS + ec // SSM_HEAD_DIM).astype(f32)
    heads_per_group = SSM_HEADS // SSM_GROUPS
    heads_per_slab = LANES // SSM_HEAD_DIM
    tot_at = L - 1 if fwd else 0

    chunks = range(TOK_TILE // L)
    for cidx in (chunks if fwd else reversed(chunks)):
        r0 = cidx * L
        xs = xc_ref[r0:r0 + L, 0:SSM_WIDTH]
        bm = xc_ref[r0:r0 + L, SSM_WIDTH:SSM_WIDTH + LANES]
        cm = xc_ref[r0:r0 + L, SSM_WIDTH + LANES:SSM_XBC]
        dt = _softplus(dt_ref[r0:r0 + L, :] + dtb_ref[...])
        da = dt * a_row
        acum_col = jnp.dot(tri_col, da, precision=HIGHEST, preferred_element_type=f32)
        acum_row = jnp.dot(da.T, tri_row, precision=HIGHEST, preferred_element_type=f32)
        tot_row = acum_col[tot_at:tot_at + 1, :]
        tot_col = acum_row[:, tot_at:tot_at + 1]
        decay_out = jnp.exp(tot_col - acum_row)
        decay_in = jnp.exp(acum_col)
        chunk_decay = jnp.exp(tot_row)
        xdt = (xs * jnp.dot(dt, expand, precision=HIGHEST, preferred_element_type=f32)).astype(bf16)
        bt = bm.T
        bb = bm.astype(bf16)
        ys = []
        for g in range(SSM_GROUPS):
            in_group = (lane >= g * SSM_STATE) & (lane < (g + 1) * SSM_STATE)
            cg = jnp.where(in_group, cm, 0.0)
            gmat = lax.dot_general(cg.astype(bf16), bb, (((1,), (1,)), ((), ())), preferred_element_type=f32)
            for hh in range(heads_per_group):
                h = g * heads_per_group + hh
                hl = direction * SSM_HEADS + h
                slab = xdt[:, (h // heads_per_slab) * LANES:(h // heads_per_slab + 1) * LANES]
                seg = acum_col[:, hl:hl + 1] - acum_row[hl:hl + 1, :]
                m = (gmat * jnp.exp(jnp.where(keep, seg, -jnp.inf))).astype(bf16)
                y_h = jnp.dot(m, slab, preferred_element_type=f32)
                c_in = (cg * decay_in[:, hl:hl + 1]).astype(bf16)
                state = st_ref[h]
                y_h = y_h + jnp.dot(c_in, state.astype(bf16), preferred_element_type=f32)
                b_out = (bt * decay_out[hl:hl + 1, :]).astype(bf16)
                st_ref[h] = chunk_decay[:, hl:hl + 1] * state + jnp.dot(b_out, slab, preferred_element_type=f32)
                ys.append(y_h)
        for j in range(SSM_WIDTH // LANES):
            y_slab = jnp.where(lane < SSM_HEAD_DIM, ys[heads_per_slab * j], ys[heads_per_slab * j + 1])
            if fwd:
                y_slab = y_slab + dskip_ref[:, j * LANES:(j + 1) * LANES] * xs[:, j * LANES:(j + 1) * LANES]
            y_ref[r0:r0 + L, j * LANES:(j + 1) * LANES] = y_slab

    @pl.when(leave)
    def _():
        fin_ref[...] = st_ref[...]


def _ssd(geom, direction, xbc, dt, init, conv_w, conv_b, dt_bias, a_log, d_skip):
    nt = geom.n_tiles
    fwd = direction == 0
    tidx = (lambda i: i) if fwd else (lambda i: nt - 1 - i)
    per = TOK_TILE // SSM_HALO
    n_halo = geom.n_tok // SSM_HALO
    n_seq = geom.n_ctx_seq + geom.n_lat_seq
    const = lambda a: pl.BlockSpec(a.shape, lambda i: (0,) * a.ndim)
    st_block = (None, SSM_HEADS, LANES, LANES)
    tile = lambda w: pl.BlockSpec((TOK_TILE, w), lambda i: (tidx(i), 0))
    init_spec = pl.BlockSpec(st_block, lambda i: (jnp.maximum(geom.seq_index(tidx(i)) - geom.n_ctx_seq, 0), 0, 0, 0))
    out_specs = [tile(SSM_WIDTH), pl.BlockSpec(st_block, lambda i: (geom.seq_index(tidx(i)), 0, 0, 0))]
    out_shape = [jax.ShapeDtypeStruct((geom.n_tok, SSM_WIDTH), f32),
                 jax.ShapeDtypeStruct((n_seq, SSM_HEADS, LANES, LANES), f32)]
    state_scratch = pltpu.VMEM((SSM_HEADS, LANES, LANES), f32)
    if fwd:
        in_specs = [
            tile(SSM_XBC),
            pl.BlockSpec((SSM_HALO, SSM_XBC), lambda i: (jnp.maximum(i * per - 1, 0), 0)),
            pl.BlockSpec((SSM_HALO, SSM_XBC), lambda i: (jnp.minimum((i + 1) * per, n_halo - 1), 0)),
            tile(LANES), init_spec, const(conv_w), const(conv_b), const(dt_bias), const(a_log), const(d_skip),
        ]
        args = (xbc, xbc, xbc, dt, init, conv_w, conv_b, dt_bias, a_log, d_skip)
        out_specs.append(tile(SSM_XBC))
        out_shape.append(jax.ShapeDtypeStruct((geom.n_tok, SSM_XBC), f32))
        scratch = [pltpu.VMEM((TOK_TILE + 2 * SSM_HALO, SSM_XBC), f32), state_scratch]
    else:
        in_specs = [tile(SSM_XBC), tile(LANES), init_spec, const(dt_bias), const(a_log), const(d_skip)]
        args = (xbc, dt, init, dt_bias, a_log, d_skip)
        scratch = [state_scratch]
    return pl.pallas_call(
        functools.partial(_ssd_kernel, geom, direction),
        grid=(nt,),
        in_specs=in_specs,
        out_specs=out_specs,
        out_shape=out_shape,
        scratch_shapes=scratch,
        compiler_params=_cparams("arbitrary"),
        name="ssd_scan",
    )(*args)


def _pad_states(s):
    return jnp.tile(jnp.swapaxes(s, -1, -2), (1, 1, LANES // SSM_STATE, LANES // SSM_HEAD_DIM))


def _unpad_states(fin):
    hpg = SSM_HEADS // SSM_GROUPS
    outs = []
    for h in range(SSM_HEADS):
        g, side = h // hpg, h % (LANES // SSM_HEAD_DIM)
        q = fin[:, h, g * SSM_STATE:(g + 1) * SSM_STATE, side * SSM_HEAD_DIM:(side + 1) * SSM_HEAD_DIM]
        outs.append(jnp.swapaxes(q, -1, -2))
    return jnp.stack(outs, axis=1)


def _out_proj_kernel(geom, actx_ref, alat_ref, conv_ref, yf_ref, yb_ref, z_ref, x_ref, mod_ref, sg_ref, w_ref,
                     g2_ref, rwh_ref, rwl_ref, rb_ref, xo_ref, h2_ref, ti_ref, gw_ref, cnt_ref):
    attn = jnp.where(pl.program_id(0) < geom.ctx_tiles, actx_ref[...], alat_ref[...])
    gated = (yf_ref[...] + yb_ref[...]) * _silu(z_ref[...])
    ssm = gated * lax.rsqrt(jnp.mean(gated * gated, axis=-1, keepdims=True) + EPS) * sg_ref[...]
    o1 = ATT_WIDTH
    o2 = ATT_WIDTH + CONV_CH
    mix = (jnp.dot(attn.astype(bf16), w_ref[0:o1, :], preferred_element_type=f32)
           + jnp.dot(conv_ref[...].astype(bf16), w_ref[o1:o2, :], preferred_element_type=f32)
           + jnp.dot(ssm.astype(bf16), w_ref[o2:, :], preferred_element_type=f32))
    x = x_ref[...] + mod_ref[2:3, :] * mix
    xo_ref[...] = x
    y = x * lax.rsqrt(jnp.mean(x * x, axis=-1, keepdims=True) + EPS) * g2_ref[...]
    h2 = y * (1.0 + mod_ref[4:5, :]) + mod_ref[3:4, :]
    h2_ref[...] = h2
    h_hi = h2.astype(bf16)
    h_lo = (h2 - h_hi.astype(f32)).astype(bf16)
    logits = (jnp.dot(h_hi, rwh_ref[...], preferred_element_type=f32)
              + jnp.dot(h_lo, rwh_ref[...], preferred_element_type=f32)
              + jnp.dot(h_hi, rwl_ref[...], preferred_element_type=f32)) + rb_ref[...]
    lane = lax.broadcasted_iota(jnp.int32, (1, LANES), 1)
    lane_f = lane.astype(f32)
    lg = jnp.where(lane < N_EXPERTS, logits, -jnp.inf)
    idx_out = jnp.zeros(logits.shape, jnp.int32)
    gate_out = jnp.zeros(logits.shape, f32)
    top0 = None
    denom = None
    picks = []
    for k in range(TOP_K):
        m = jnp.max(lg, axis=-1, keepdims=True)
        idx = jnp.min(jnp.where(lg == m, lane_f, float(LANES)), axis=-1, keepdims=True).astype(jnp.int32)
        picked = lane == idx
        picks.append(picked)
        lg = jnp.where(picked, -jnp.inf, lg)
        if k == 0:
            top0 = m
        e = jnp.exp(m - top0)
        denom = e if k == 0 else denom + e
        idx_out = jnp.where(lane == k, idx, idx_out)
        gate_out = jnp.where(lane == k, e, gate_out)
    gw_ref[...] = gate_out / denom

    @pl.when(pl.program_id(0) == 0)
    def _():
        cnt_ref[...] = jnp.zeros_like(cnt_ref)

    chosen = picks[0] | picks[1] | picks[2] | picks[3]
    sel = jnp.where(chosen, 1.0, 0.0)
    ri = lax.broadcasted_iota(jnp.int32, (TOK_TILE, TOK_TILE), 0)
    ci = lax.broadcasted_iota(jnp.int32, (TOK_TILE, TOK_TILE), 1)
    before = jnp.where(ci < ri, 1.0, 0.0).astype(bf16)
    prior = jnp.dot(before, sel.astype(bf16), preferred_element_type=f32) + cnt_ref[...]
    for k in range(TOP_K):
        rank = jnp.sum(jnp.where(picks[k], prior, 0.0), axis=-1, keepdims=True).astype(jnp.int32)
        idx_out = jnp.where(lane == TOP_K + k, rank, idx_out)
    ti_ref[...] = idx_out
    cnt_ref[...] = cnt_ref[...] + jnp.sum(sel, axis=0, keepdims=True)


def _out_proj(geom, layer, attn_ctx, attn_lat, conv, y_f, y_b, z, x, mod, ssm_norm_g, w_out, norm2_g, router_w,
              router_b):
    n = geom.n_tok
    tile = lambda w: pl.BlockSpec((TOK_TILE, w), lambda i: (i, 0))
    const = lambda a: pl.BlockSpec(a.shape, lambda i: (0,) * a.ndim)
    nct = geom.ctx_tiles
    rw_hi = router_w.astype(bf16)
    rw_lo = (router_w - rw_hi.astype(f32)).astype(bf16)
    return pl.pallas_call(
        functools.partial(_out_proj_kernel, geom),
        grid=(geom.n_tiles,),
        in_specs=[
            pl.BlockSpec((TOK_TILE, ATT_WIDTH), lambda i: (jnp.minimum(i, nct - 1), 0)),
            pl.BlockSpec((TOK_TILE, ATT_WIDTH), lambda i: (jnp.maximum(i - nct, 0), 0)),
            tile(CONV_CH), tile(SSM_WIDTH), tile(SSM_WIDTH), tile(SSM_WIDTH), tile(D_MODEL),
            pl.BlockSpec((None, None, 6, D_MODEL), lambda i: (layer, geom.cond_row(i), 0, 0)),
            const(ssm_norm_g), const(w_out), const(norm2_g), const(rw_hi), const(rw_lo), const(router_b),
        ],
        out_specs=[tile(D_MODEL), tile(D_MODEL), tile(LANES), tile(LANES),
                   pl.BlockSpec((1, LANES), lambda i: (0, 0))],
        out_shape=[jax.ShapeDtypeStruct((n, D_MODEL), f32), jax.ShapeDtypeStruct((n, D_MODEL), f32),
                   jax.ShapeDtypeStruct((n, LANES), jnp.int32), jax.ShapeDtypeStruct((n, LANES), f32),
                   jax.ShapeDtypeStruct((1, LANES), f32)],
        compiler_params=_cparams("arbitrary"),
        name="out_proj",
    )(attn_ctx, attn_lat, conv, y_f, y_b, z, x, mod, ssm_norm_g, w_out, norm2_g, rw_hi, rw_lo, router_b)


def _moe_kernel(be_ref, rows_ref, rs_ref, x_ref, wg_ref, bg_ref, wu_ref, bu_ref, wd_ref, bd_ref, o_ref,
                wg_s, wu_s, wd_s):
    i = pl.program_id(0)
    prev_e = be_ref[jnp.maximum(i - 1, 0)]
    new_expert = (i == 0) | (be_ref[i] != prev_e)

    @pl.when(new_expert)
    def _():
        wg_s[...] = wg_ref[0].astype(bf16)
        wu_s[...] = wu_ref[0].astype(bf16)
        wd_s[...] = wd_ref[0].astype(bf16)

    def ffn(r):
        x = x_ref[0:r, :].astype(bf16)
        acc = None
        for c0 in range(0, wg_s.shape[1], MOE_FF_CHUNK):
            sl = slice(c0, c0 + MOE_FF_CHUNK)
            g = jnp.dot(x, wg_s[:, sl], preferred_element_type=f32) + bg_ref[0, :, sl]
            u = jnp.dot(x, wu_s[:, sl], preferred_element_type=f32) + bu_ref[0, :, sl]
            g = jnp.minimum(g, SWIGLU_LIMIT)
            u = jnp.clip(u, -SWIGLU_LIMIT, SWIGLU_LIMIT)
            act = g * jax.nn.sigmoid(SWIGLU_ALPHA * g) * (u + 1.0)
            part = jnp.dot(act.astype(bf16), wd_s[sl, :], preferred_element_type=f32)
            acc = part if acc is None else acc + part
        o_ref[0:r, :] = acc + bd_ref[0]
        if r < MOE_TILE:
            o_ref[r:, :] = jnp.zeros((MOE_TILE - r, o_ref.shape[1]), f32)

    rows = rows_ref[i]
    for r in range(MOE_ROW_STEP, MOE_TILE + 1, MOE_ROW_STEP):
        pl.when(rows == r)(functools.partial(ffn, r))

    @pl.when(rows == 0)
    def _():
        o_ref[...] = jnp.zeros_like(o_ref)


def _moe(layer, xs, block_e, block_rows, row_start, w_gate, b_gate, w_up, b_up, w_down, b_down):
    d = xs.shape[1]
    n_blocks = block_e.shape[0]
    d_ff = w_gate.shape[-1]
    w_spec = lambda r, c: pl.BlockSpec((None, 1, r, c), lambda i, be, nu, rs: (layer, be[i], 0, 0))
    b_spec = lambda c: pl.BlockSpec((None, 1, 1, c), lambda i, be, nu, rs: (layer, be[i], 0, 0))
    depth, n_e = b_gate.shape[:2]
    return pl.pallas_call(
        _moe_kernel,
        grid_spec=pltpu.PrefetchScalarGridSpec(
            num_scalar_prefetch=3,
            grid=(n_blocks,),
            in_specs=[pl.BlockSpec((pl.Element(MOE_TILE), pl.Element(d)),
                                   lambda i, be, nu, rs: (rs[i] * SUBLANES, 0)),
                      w_spec(d, d_ff), b_spec(d_ff), w_spec(d, d_ff), b_spec(d_ff), w_spec(d_ff, d), b_spec(d)],
            out_specs=pl.BlockSpec((MOE_TILE, d), lambda i, be, nu, rs: (i, 0)),
            scratch_shapes=[pltpu.VMEM((d, d_ff), bf16), pltpu.VMEM((d, d_ff), bf16), pltpu.VMEM((d_ff, d), bf16)],
        ),
        out_shape=jax.ShapeDtypeStruct((n_blocks * MOE_TILE, d), f32),
        compiler_params=_cparams("arbitrary"),
        name="moe_experts",
    )(block_e, block_rows, row_start, xs, w_gate, b_gate.reshape(depth, n_e, 1, d_ff), w_up,
      b_up.reshape(depth, n_e, 1, d_ff), w_down, b_down.reshape(depth, n_e, 1, d))


def _expert_table(expert_ids, table):
    onehot = expert_ids[:, None] == jnp.arange(N_EXPERTS, dtype=jnp.int32)[None, :]
    return jnp.sum(jnp.where(onehot, table[None, :], 0), axis=1)


def _route(top_idx, rank, counts, n_blocks):
    n = top_idx.shape[0]
    n_assign = n * TOP_K
    i32 = jnp.int32
    flat_e = top_idx.reshape(-1)
    flat_rank = rank.reshape(-1)
    experts = jnp.arange(N_EXPERTS, dtype=i32)
    fill = SUBLANES - 1
    grp_size = (counts + fill) // SUBLANES * SUBLANES
    grp_start = jnp.cumsum(grp_size) - grp_size
    nb = (counts + MOE_TILE - 1) // MOE_TILE
    blk_end = jnp.cumsum(nb)
    blk_start = blk_end - nb
    n_sorted = n_assign + N_EXPERTS * fill
    j = jnp.arange(fill, dtype=i32)
    filler_pos = jnp.where(j[None, :] < (grp_size - counts)[:, None], (grp_start + counts)[:, None] + j[None, :],
                           n_sorted)
    pos = jnp.concatenate([_expert_table(flat_e, grp_start) + flat_rank, filler_pos.reshape(-1)])
    tok = jnp.concatenate([jnp.arange(n_assign, dtype=i32) // TOP_K, jnp.zeros((N_EXPERTS * fill,), i32)])
    assert n < TOKEN_KEY_RANGE and (n_sorted + 1) * TOKEN_KEY_RANGE <= 2 ** 32
    keyed = lax.sort(pos.astype(jnp.uint32) * TOKEN_KEY_RANGE + tok.astype(jnp.uint32))
    sorted_tok = (keyed % TOKEN_KEY_RANGE).astype(i32)
    src_tok = jnp.concatenate([sorted_tok, jnp.zeros((MOE_TILE,), i32)])
    dest = _expert_table(flat_e, blk_start * MOE_TILE) + flat_rank
    blk = jnp.arange(n_blocks, dtype=i32)
    block_e = jnp.minimum(jnp.sum((blk_end[None, :] <= blk[:, None]).astype(i32), axis=1), N_EXPERTS - 1)
    used = blk < blk_end[-1]
    row_start = jnp.where(used, blk * MOE_TILE - _expert_table(block_e, blk_start * MOE_TILE - grp_start), 0)
    filled = jnp.clip(_expert_table(block_e, counts + blk_start * MOE_TILE) - blk * MOE_TILE, 0, MOE_TILE)
    block_rows = jnp.where(used, (filled + MOE_ROW_STEP - 1) // MOE_ROW_STEP * MOE_ROW_STEP, 0)
    return src_tok, dest.reshape(n, TOP_K), block_e, block_rows, row_start // SUBLANES


def _combine_kernel(final, x_ref, rows_ref, gw_ref, mod_ref, *rest):
    ffn = rows_ref[0] * gw_ref[:, 0:1]
    for k in range(1, TOP_K):
        ffn = ffn + rows_ref[k] * gw_ref[:, k:k + 1]
    x = x_ref[...] + mod_ref[5:6, :] * ffn
    if final:
        fg_ref, y_ref = rest
        y_ref[...] = x * lax.rsqrt(jnp.mean(x * x, axis=-1, keepdims=True) + EPS) * fg_ref[...]
    else:
        rest[0][...] = x


def _combine(geom, layer, x, rows, gate_w, mod, final_g):
    n = geom.n_tok
    final = final_g is not None
    tile = pl.BlockSpec((TOK_TILE, D_MODEL), lambda i: (i, 0))
    in_specs = [
        tile,
        pl.BlockSpec((TOP_K, TOK_TILE, D_MODEL), lambda i: (0, i, 0)),
        pl.BlockSpec((TOK_TILE, LANES), lambda i: (i, 0)),
        pl.BlockSpec((None, None, 6, D_MODEL), lambda i: (layer, geom.cond_row(i), 0, 0)),
    ]
    args = [x, rows, gate_w, mod]
    if final:
        in_specs.append(pl.BlockSpec(final_g.shape, lambda i: (0, 0)))
        args.append(final_g)
    return pl.pallas_call(
        functools.partial(_combine_kernel, final),
        grid=(geom.n_tiles,),
        in_specs=in_specs,
        out_specs=tile,
        out_shape=jax.ShapeDtypeStruct((n, D_MODEL), f32),
        compiler_params=_cparams("arbitrary"),
        name="moe_combine",
    )(*args)


def _pad_lanes(a, width=LANES):
    return jnp.pad(a, [(0, 0)] * (a.ndim - 1) + [(0, width - a.shape[-1])])


def kernel(x_prompt, x_sample, c, cache_k, cache_v, state_ssm, c_ctx, norm1_g, norm2_g, w_ada, b_ada, w_in, w_out, attn_lambda, attn_subln_g, conv_dw_w, conv_dw_b, conv_ln_g, conv_ln_b, ssm_conv_w, ssm_conv_b, ssm_dt_bias, ssm_a_log, ssm_d, ssm_norm_g, router_w, router_b, w_gate, b_gate, w_up, b_up, w_down, b_down, final_g):
    n_ctx_seq, ctx_len, d = x_prompt.shape
    n_lat_seq, lat_len, _ = x_sample.shape
    past_len = cache_k.shape[-2]
    geom = _Geom(n_ctx_seq, ctx_len, n_lat_seq, lat_len)
    n = geom.n_tok

    x = jnp.concatenate([x_prompt.reshape(geom.n_ctx, d), x_sample.reshape(geom.n_lat, d)], axis=0)
    cond = jnp.concatenate([c_ctx[None, :], c, jnp.zeros((COND_ROWS - 1 - n_lat_seq, d), f32)], axis=0)
    mod = _ada(cond, w_ada, b_ada).reshape(DEPTH, COND_ROWS, 6, d)
    tables = _rope_tables(geom)

    n_assign = n * TOP_K
    n_blocks = -(-n_assign // MOE_TILE) + N_EXPERTS

    new_k, new_v, new_s = [], [], []
    y_final = None
    for l in range(DEPTH):
        w_main = w_in[l, :, :N_MAIN].astype(bf16)
        w_dt = _pad_lanes(w_in[l, :, N_MAIN:]).astype(bf16)
        q, k, v, glu, z, xbc, dt = _in_proj(geom, l, x, mod, norm1_g[l][None, :], w_main, w_dt, tables)

        lam_init = 0.8 - 0.6 * math.exp(-0.3 * l)
        subln = jnp.tile(attn_subln_g[l], LANES // ATT_V_DIM)[None, :]
        k_cache = cache_k[:, l].transpose(0, 3, 1, 2, 4).reshape(n_lat_seq, past_len, QK_WIDTH)
        v_cache = cache_v[:, l].transpose(0, 2, 1, 3).reshape(n_lat_seq, past_len, ATT_WIDTH)
        k_all = jnp.concatenate([k[geom.n_ctx:].reshape(n_lat_seq, lat_len, QK_WIDTH), k_cache], axis=1).astype(bf16)
        v_all = jnp.concatenate([v[geom.n_ctx:].reshape(n_lat_seq, lat_len, ATT_WIDTH), v_cache], axis=1).astype(bf16)
        attn_ctx = _attention(lam_init, attn_lambda[l], subln, q, k, v, n_ctx_seq, ctx_len, ctx_len, 0, False)
        attn_lat = _attention(lam_init, attn_lambda[l], subln, q, k_all, v_all, n_lat_seq, lat_len,
                              lat_len + past_len, geom.n_ctx, True)

        conv = _cconv(geom, glu, conv_dw_w[l], conv_dw_b[l][None, :], conv_ln_g[l][None, :], conv_ln_b[l][None, :])

        dt_bias = _pad_lanes(ssm_dt_bias[l].reshape(1, 2 * SSM_HEADS))
        a_log = _pad_lanes(ssm_a_log[l].reshape(1, 2 * SSM_HEADS))
        d_skip = jnp.repeat(ssm_d[l], SSM_HEAD_DIM)[None, :]
        conv_w, conv_b = ssm_conv_w[l], ssm_conv_b[l][None, :]
        y_f, fin_f, xbc_act = _ssd(geom, 0, xbc, dt, _pad_states(state_ssm[:, l, 0]), conv_w, conv_b, dt_bias, a_log,
                                   d_skip)
        y_b, fin_b = _ssd(geom, 1, xbc_act, dt, _pad_states(state_ssm[:, l, 1]), conv_w, conv_b, dt_bias, a_log, d_skip)
        ys = [y_f, y_b]
        fins = [_unpad_states(fin_f[:n_ctx_seq]), _unpad_states(fin_b[:n_ctx_seq])]

        x, h2, routed, gate_w, totals = _out_proj(
            geom, l, attn_ctx, attn_lat, conv, ys[0], ys[1], z, x, mod, ssm_norm_g[l][None, :], w_out[l].astype(bf16),
            norm2_g[l][None, :], _pad_lanes(router_w[l]), _pad_lanes(router_b[l][None, :]))

        src_tok, dest, block_e, block_rows, row_start = _route(
            routed[:, :TOP_K], routed[:, TOP_K:2 * TOP_K], totals[0, :N_EXPERTS].astype(jnp.int32), n_blocks)
        xs = jnp.take(h2, src_tok, axis=0, mode="clip")
        eo = _moe(l, xs, block_e, block_rows, row_start, w_gate, b_gate, w_up, b_up, w_down, b_down)
        rows = jnp.take(eo, dest.T.reshape(-1), axis=0, mode="clip").reshape(TOP_K, n, d)
        if l == DEPTH - 1:
            y_final = _combine(geom, l, x, rows, gate_w, mod, final_g[None, :])
        else:
            x = _combine(geom, l, x, rows, gate_w, mod, None)

        new_k.append(k[:geom.n_ctx].reshape(n_ctx_seq, ctx_len, ATT_HEADS, 2, ATT_QK_DIM).transpose(0, 2, 3, 1, 4))
        new_v.append(v[:geom.n_ctx].reshape(n_ctx_seq, ctx_len, ATT_HEADS, ATT_V_DIM).transpose(0, 2, 1, 3))
        new_s.append(jnp.stack(fins, axis=1))

    y_prompt = y_final[:geom.n_ctx].reshape(n_ctx_seq, ctx_len, d)
    y_sample = y_final[geom.n_ctx:].reshape(n_lat_seq, lat_len, d)
    return (y_prompt, y_sample, jnp.stack(new_k, axis=1), jnp.stack(new_v, axis=1), jnp.stack(new_s, axis=1))
```

```python
import functools
import math

import numpy as np
import jax
import jax.numpy as jnp
from jax import lax
from jax.experimental import pallas as pl
from jax.experimental.pallas import tpu as pltpu

f32 = jnp.float32
bf16 = jnp.bfloat16
HIGHEST = lax.Precision.HIGHEST

D_MODEL = 1024
DEPTH = 2
GRID_W = 64
ATT_HEADS = 4
ATT_QK_DIM = 32
ATT_V_DIM = 64
ATT_WIDTH = 256
QK_WIDTH = 256
ROPE_AXIS_DIM = 16
ROPE_BASE = 10000.0
CONV_CH = 256
CONV_K = 31
SSM_WIDTH = 512
SSM_HEAD_DIM = 64
SSM_HEADS = 8
SSM_GROUPS = 2
SSM_STATE = 64
SSM_CONV_K = 5
SSM_XBC = 768
N_MAIN = 2560
N_EXPERTS = 32
TOP_K = 4
SWIGLU_LIMIT = 7.0
SWIGLU_ALPHA = 1.702
EPS = 1e-6

LANES = 128
SUBLANES = 8
TOK_TILE = 256
SSD_CHUNK = 128
CONV_HALO = 16
SSM_HALO = 8
MOE_TILE = 1024
MOE_FF_CHUNK = 256
MOE_ROW_STEP = 128
TOKEN_KEY_RANGE = 2 ** 15
COND_ROWS = 16
VMEM_LIMIT = 56 * 1024 * 1024


def _cparams(*sem):
    return pltpu.CompilerParams(dimension_semantics=sem, vmem_limit_bytes=VMEM_LIMIT)


def _silu(x):
    return x * jax.nn.sigmoid(x)


def _ada_kernel(cond_ref, w_ref, b_ref, o_ref):
    s = _silu(cond_ref[...]).astype(bf16)
    o_ref[0] = jnp.dot(s, w_ref[0].astype(bf16), preferred_element_type=f32) + b_ref[0]


def _ada(cond, w_ada, b_ada):
    depth, d, n6 = w_ada.shape
    tn = 1536
    return pl.pallas_call(
        _ada_kernel,
        grid=(depth, n6 // tn),
        in_specs=[
            pl.BlockSpec((COND_ROWS, d), lambda l, j: (0, 0)),
            pl.BlockSpec((1, d, tn), lambda l, j: (l, 0, j)),
            pl.BlockSpec((1, 1, tn), lambda l, j: (l, 0, j)),
        ],
        out_specs=pl.BlockSpec((1, COND_ROWS, tn), lambda l, j: (l, 0, j)),
        out_shape=jax.ShapeDtypeStruct((depth, COND_ROWS, n6), f32),
        compiler_params=_cparams("arbitrary", "arbitrary"),
        name="ada",
    )(cond, w_ada, b_ada.reshape(depth, 1, n6))


class _Geom:
    def __init__(self, n_ctx_seq, ctx_len, n_lat_seq, lat_len):
        self.n_ctx_seq, self.ctx_len, self.n_lat_seq, self.lat_len = n_ctx_seq, ctx_len, n_lat_seq, lat_len
        self.n_ctx = n_ctx_seq * ctx_len
        self.n_lat = n_lat_seq * lat_len
        self.n_tok = self.n_ctx + self.n_lat
        assert ctx_len % TOK_TILE == 0 and lat_len % TOK_TILE == 0
        self.ctx_tiles = self.n_ctx // TOK_TILE
        self.tiles_per_ctx = ctx_len // TOK_TILE
        self.tiles_per_lat = lat_len // TOK_TILE
        self.n_tiles = self.n_tok // TOK_TILE

    def cond_row(self, i):
        return jnp.where(i < self.ctx_tiles, 0, 1 + (i - self.ctx_tiles) // self.tiles_per_lat)

    def seq_pos(self, i):
        j_ctx = i % self.tiles_per_ctx
        j_lat = (i - self.ctx_tiles) % self.tiles_per_lat
        is_ctx = i < self.ctx_tiles
        first = jnp.where(is_ctx, j_ctx == 0, j_lat == 0)
        last = jnp.where(is_ctx, j_ctx == self.tiles_per_ctx - 1, j_lat == self.tiles_per_lat - 1)
        return first, last

    def seq_index(self, i):
        return jnp.where(i < self.ctx_tiles, i // self.tiles_per_ctx,
                         self.n_ctx_seq + (i - self.ctx_tiles) // self.tiles_per_lat)


def _rope_tables(geom):
    t = np.arange(geom.lat_len)
    row = (t // GRID_W).astype(np.float64)
    col = (t % GRID_W).astype(np.float64)
    inv_freq = 1.0 / (ROPE_BASE ** (np.arange(0, ROPE_AXIS_DIM, 2, dtype=np.float64) / ROPE_AXIS_DIM))
    ang_r = row[:, None] * inv_freq[None, :]
    ang_c = col[:, None] * inv_freq[None, :]
    ang = np.concatenate([ang_r, ang_r, ang_c, ang_c], axis=-1)
    cos, sin = np.cos(ang), np.sin(ang)
    half = ROPE_AXIS_DIM // 2
    lane = np.arange(ATT_QK_DIM)
    first_half = (lane % ROPE_AXIS_DIM) < half
    sin_a = np.where(first_half[None, :], -sin, 0.0)
    sin_b = np.where(first_half[None, :], 0.0, sin)
    rep = LANES // ATT_QK_DIM

    def full(tab, ident):
        tab = np.tile(tab, (1, rep))
        return jnp.asarray(np.concatenate([np.full((TOK_TILE, LANES), ident), tab], axis=0), f32)

    return full(cos, 1.0), full(sin_a, 0.0), full(sin_b, 0.0)


def _in_proj_kernel(x_ref, mod_ref, g_ref, w_ref, wdt_ref, cos_ref, sa_ref, sb_ref,
                    q_ref, k_ref, v_ref, glu_ref, z_ref, xbc_ref, dt_ref):
    x = x_ref[...]
    y = x * lax.rsqrt(jnp.mean(x * x, axis=-1, keepdims=True) + EPS) * g_ref[...]
    h = (y * (1.0 + mod_ref[1:2, :]) + mod_ref[0:1, :]).astype(bf16)
    p = jnp.dot(h, w_ref[...], preferred_element_type=f32)
    dt_ref[...] = jnp.dot(h, wdt_ref[...], preferred_element_type=f32)
    cos, sa, sb = cos_ref[...], sa_ref[...], sb_ref[...]
    half = ROPE_AXIS_DIM // 2

    def rope(u):
        return u * cos + pltpu.roll(u, LANES - half, 1) * sa + pltpu.roll(u, half, 1) * sb

    scale = ATT_QK_DIM ** -0.5 * math.log2(math.e)
    for j in range(QK_WIDTH // LANES):
        lo = j * LANES
        q_ref[:, lo:lo + LANES] = (rope(p[:, lo:lo + LANES]) * scale).astype(bf16)
        k_ref[:, lo:lo + LANES] = rope(p[:, QK_WIDTH + lo:QK_WIDTH + lo + LANES])
    o = 2 * QK_WIDTH
    v_ref[...] = p[:, o:o + ATT_WIDTH]
    o += ATT_WIDTH
    glu_ref[...] = p[:, o:o + 2 * CONV_CH]
    o += 2 * CONV_CH
    z_ref[...] = p[:, o:o + SSM_WIDTH]
    o += SSM_WIDTH
    xbc_ref[...] = p[:, o:o + SSM_XBC]


def _in_proj(geom, layer, x, mod, norm_g, w_main, w_dt, tables):
    n = geom.n_tok
    tile = lambda w: pl.BlockSpec((TOK_TILE, w), lambda i: (i, 0))
    const = lambda a: pl.BlockSpec(a.shape, lambda i: (0,) * a.ndim)
    tab_spec = pl.BlockSpec(
        (TOK_TILE, LANES),
        lambda i: (jnp.where(i < geom.ctx_tiles, 0, 1 + (i - geom.ctx_tiles) % geom.tiles_per_lat), 0))
    widths = (QK_WIDTH, QK_WIDTH, ATT_WIDTH, 2 * CONV_CH, SSM_WIDTH, SSM_XBC, LANES)
    dtypes = (bf16, f32, f32, f32, f32, f32, f32)
    return pl.pallas_call(
        _in_proj_kernel,
        grid=(geom.n_tiles,),
        in_specs=[
            tile(D_MODEL),
            pl.BlockSpec((None, None, 6, D_MODEL), lambda i: (layer, geom.cond_row(i), 0, 0)),
            const(norm_g), const(w_main), const(w_dt), tab_spec, tab_spec, tab_spec,
        ],
        out_specs=[tile(w) for w in widths],
        out_shape=[jax.ShapeDtypeStruct((n, w), dt) for w, dt in zip(widths, dtypes)],
        compiler_params=_cparams("arbitrary"),
        name="in_proj",
    )(x, mod, norm_g, w_main, w_dt, *tables)


def _attn_kernel(lam_init, al_ref, g_ref, q_ref, k_ref, v_ref, o_ref):
    al = al_ref[...]
    lam = (jnp.exp(jnp.sum(al[0:1] * al[1:2], axis=-1, keepdims=True))
           - jnp.exp(jnp.sum(al[2:3] * al[3:4], axis=-1, keepdims=True)) + lam_init)
    lane = lax.broadcasted_iota(jnp.int32, (1, LANES), 1)
    heads_per_slab = LANES // ATT_V_DIM
    for j in range(ATT_WIDTH // LANES):
        sl = slice(j * LANES, (j + 1) * LANES)
        qs = q_ref[:, sl]
        ks = k_ref[:, sl].astype(bf16)
        vs = v_ref[:, sl].astype(bf16)
        a_heads = []
        for hl in range(heads_per_slab):
            own = (lane >= hl * ATT_V_DIM) & (lane < (hl + 1) * ATT_V_DIM)
            vh = jnp.where(own, vs, jnp.ones_like(vs))
            sum_lane = (1 - hl) * ATT_V_DIM
            comps = []
            for c in range(2):
                lo = hl * ATT_V_DIM + c * ATT_QK_DIM
                qm = jnp.where((lane >= lo) & (lane < lo + ATT_QK_DIM), qs, jnp.zeros_like(qs))
                s = lax.dot_general(qm, ks, (((1,), (1,)), ((), ())), preferred_element_type=f32)
                e = jnp.exp2(s - jnp.max(s, axis=-1, keepdims=True))
                pv = jnp.dot(e.astype(bf16), vh, preferred_element_type=f32)
                comps.append(pv / pv[:, sum_lane:sum_lane + 1])
            a_heads.append(comps[0] - lam * comps[1])
        a = jnp.where(lane < ATT_V_DIM, a_heads[0], a_heads[1])
        a2 = a * a
        ms0 = jnp.sum(jnp.where(lane < ATT_V_DIM, a2, 0.0), axis=-1, keepdims=True)
        ms1 = jnp.sum(jnp.where(lane < ATT_V_DIM, 0.0, a2), axis=-1, keepdims=True)
        ms = jnp.where(lane < ATT_V_DIM, ms0, ms1) * (1.0 / ATT_V_DIM)
        o_ref[:, sl] = a * lax.rsqrt(ms + EPS) * g_ref[...] * (1.0 - lam_init)


def _attention(lam_init, attn_lambda, subln_g, q, k, v, n_seq, q_len, k_len, q_row0, kv_is_3d):
    tq = TOK_TILE
    qb = q_len // tq
    q0 = q_row0 // tq
    if kv_is_3d:
        kv_spec = pl.BlockSpec((None, k_len, ATT_WIDTH), lambda b, i: (b, 0, 0))
    else:
        kv_spec = pl.BlockSpec((k_len, ATT_WIDTH), lambda b, i: (q_row0 // k_len + b, 0))
    return pl.pallas_call(
        functools.partial(_attn_kernel, lam_init),
        grid=(n_seq, qb),
        in_specs=[
            pl.BlockSpec(attn_lambda.shape, lambda b, i: (0, 0)),
            pl.BlockSpec(subln_g.shape, lambda b, i: (0, 0)),
            pl.BlockSpec((tq, ATT_WIDTH), lambda b, i: (q0 + b * qb + i, 0)),
            kv_spec, kv_spec,
        ],
        out_specs=pl.BlockSpec((tq, ATT_WIDTH), lambda b, i: (b * qb + i, 0)),
        out_shape=jax.ShapeDtypeStruct((n_seq * q_len, ATT_WIDTH), f32),
        compiler_params=_cparams("arbitrary", "arbitrary"),
        name="attention",
    )(attn_lambda, subln_g, q, k, v)


def _halo_specs(geom, width, halo):
    per = TOK_TILE // halo
    n_halo = geom.n_tok // halo
    prev = pl.BlockSpec((halo, width), lambda i: (jnp.maximum(i * per - 1, 0), 0))
    nxt = pl.BlockSpec((halo, width), lambda i: (jnp.minimum((i + 1) * per, n_halo - 1), 0))
    return prev, nxt


def _cconv_kernel(geom, cur_ref, prev_ref, next_ref, w_ref, b_ref, lg_ref, lb_ref, o_ref, pad_ref):
    first, last = geom.seq_pos(pl.program_id(0))

    def glu(ref):
        return ref[:, :CONV_CH] * jax.nn.sigmoid(ref[:, CONV_CH:])

    pad_ref[0:CONV_HALO, :] = jnp.where(first, 0.0, glu(prev_ref))
    pad_ref[CONV_HALO:CONV_HALO + TOK_TILE, :] = glu(cur_ref)
    pad_ref[CONV_HALO + TOK_TILE:, :] = jnp.where(last, 0.0, glu(next_ref))
    rows = 64
    base = CONV_HALO - CONV_K // 2
    for r0 in range(0, TOK_TILE, rows):
        acc = jnp.broadcast_to(b_ref[...], (rows, CONV_CH))
        for k in range(CONV_K):
            acc = acc + w_ref[k:k + 1, :] * pad_ref[r0 + base + k:r0 + base + k + rows, :]
        xc = acc - jnp.mean(acc, axis=-1, keepdims=True)
        var = jnp.mean(xc * xc, axis=-1, keepdims=True)
        o_ref[r0:r0 + rows, :] = _silu(xc * lax.rsqrt(var + EPS) * lg_ref[...] + lb_ref[...])


def _cconv(geom, glu, w, b, ln_g, ln_b):
    prev, nxt = _halo_specs(geom, 2 * CONV_CH, CONV_HALO)
    const = lambda a: pl.BlockSpec(a.shape, lambda i: (0,) * a.ndim)
    return pl.pallas_call(
        functools.partial(_cconv_kernel, geom),
        grid=(geom.n_tiles,),
        in_specs=[pl.BlockSpec((TOK_TILE, 2 * CONV_CH), lambda i: (i, 0)), prev, nxt,
                  const(w), const(b), const(ln_g), const(ln_b)],
        out_specs=pl.BlockSpec((TOK_TILE, CONV_CH), lambda i: (i, 0)),
        out_shape=jax.ShapeDtypeStruct((geom.n_tok, CONV_CH), f32),
        scratch_shapes=[pltpu.VMEM((TOK_TILE + 2 * CONV_HALO, CONV_CH), f32)],
        compiler_params=_cparams("arbitrary"),
        name="conformer_conv",
    )(glu, glu, glu, w, b, ln_g, ln_b)


def _softplus(x):
    return jnp.maximum(x, 0.0) + jnp.log1p(jnp.exp(-jnp.abs(x)))


def _ssd_kernel(geom, direction, *refs):
    fwd = direction == 0
    if fwd:
        (cur_ref, prev_ref, next_ref, dt_ref, init_ref, cw_ref, cb_ref, dtb_ref, alog_ref, dskip_ref,
         y_ref, fin_ref, xc_ref, pad_ref, st_ref) = refs
    else:
        xc_ref, dt_ref, init_ref, dtb_ref, alog_ref, dskip_ref, y_ref, fin_ref, st_ref = refs
    i = pl.program_id(0)
    t = i if fwd else geom.n_tiles - 1 - i
    first, last = geom.seq_pos(t)
    enter, leave = (first, last) if fwd else (last, first)
    is_lat = t >= geom.ctx_tiles

    @pl.when(enter)
    def _():
        st_ref[...] = jnp.where(is_lat, init_ref[...], 0.0)

    if fwd:
        pad_ref[0:SSM_HALO, :] = jnp.where(first, 0.0, prev_ref[...])
        pad_ref[SSM_HALO:SSM_HALO + TOK_TILE, :] = cur_ref[...]
        pad_ref[SSM_HALO + TOK_TILE:, :] = jnp.where(last, 0.0, next_ref[...])
        base = SSM_HALO - SSM_CONV_K // 2
        acc = jnp.broadcast_to(cb_ref[...], (TOK_TILE, SSM_XBC))
        for k in range(SSM_CONV_K):
            acc = acc + cw_ref[k:k + 1, :] * pad_ref[base + k:base + k + TOK_TILE, :]
        xc_ref[...] = _silu(acc)

    L = SSD_CHUNK
    a_row = -jnp.exp(alog_ref[...])
    ri = lax.broadcasted_iota(jnp.int32, (L, L), 0)
    ci = lax.broadcasted_iota(jnp.int32, (L, L), 1)
    keep = (ci <= ri) if fwd else (ci >= ri)
    tri_col = keep.astype(f32)
    tri_row = ((ri <= ci) if fwd else (ri >= ci)).astype(f32)
    lane = lax.broadcasted_iota(jnp.int32, (1, LANES), 1)
    er = lax.broadcasted_iota(jnp.int32, (LANES, SSM_WIDTH), 0)
    ec = lax.broadcasted_iota(jnp.int32, (LANES, SSM_WIDTH), 1)
    expand = (er == direction * SSM_HEADS + ec // SSM_HEAD_DIM).astype(f32)
    heads_per_group = SSM_HEADS // SSM_GROUPS
    heads_per_slab = LANES // SSM_HEAD_DIM
    tot_at = L - 1 if fwd else 0

    chunks = range(TOK_TILE // L)
    for cidx in (chunks if fwd else reversed(chunks)):
        r0 = cidx * L
        xs = xc_ref[r0:r0 + L, 0:SSM_WIDTH]
        bm = xc_ref[r0:r0 + L, SSM_WIDTH:SSM_WIDTH + LANES]
        cm = xc_ref[r0:r0 + L, SSM_WIDTH + LANES:SSM_XBC]
        dt = _softplus(dt_ref[r0:r0 + L, :] + dtb_ref[...])
        da = dt * a_row
        acum_col = jnp.dot(tri_col, da, precision=HIGHEST, preferred_element_type=f32)
        acum_row = jnp.dot(da.T, tri_row, precision=HIGHEST, preferred_element_type=f32)
        tot_row = acum_col[tot_at:tot_at + 1, :]
        tot_col = acum_row[:, tot_at:tot_at + 1]
        decay_out = jnp.exp(tot_col - acum_row)
        decay_in = jnp.exp(acum_col)
        chunk_decay = jnp.exp(tot_row)
        xdt = (xs * jnp.dot(dt, expand, precision=HIGHEST, preferred_element_type=f32)).astype(bf16)
        bt = bm.T
        bb = bm.astype(bf16)
        ys = []
        for g in range(SSM_GROUPS):
            in_group = (lane >= g * SSM_STATE) & (lane < (g + 1) * SSM_STATE)
            cg = jnp.where(in_group, cm, 0.0)
            gmat = lax.dot_general(cg.astype(bf16), bb, (((1,), (1,)), ((), ())), preferred_element_type=f32)
            for hh in range(heads_per_group):
                h = g * heads_per_group + hh
                hl = direction * SSM_HEADS + h
                slab = xdt[:, (h // heads_per_slab) * LANES:(h // heads_per_slab + 1) * LANES]
                seg = acum_col[:, hl:hl + 1] - acum_row[hl:hl + 1, :]
                m = (gmat * jnp.exp(jnp.where(keep, seg, -jnp.inf))).astype(bf16)
                y_h = jnp.dot(m, slab, preferred_element_type=f32)
                c_in = (cg * decay_in[:, hl:hl + 1]).astype(bf16)
                state = st_ref[h]
                y_h = y_h + jnp.dot(c_in, state.astype(bf16), preferred_element_type=f32)
                b_out = (bt * decay_out[hl:hl + 1, :]).astype(bf16)
                st_ref[h] = chunk_decay[:, hl:hl + 1] * state + jnp.dot(b_out, slab, preferred_element_type=f32)
                ys.append(y_h)
        for j in range(SSM_WIDTH // LANES):
            y_slab = jnp.where(lane < SSM_HEAD_DIM, ys[heads_per_slab * j], ys[heads_per_slab * j + 1])
            if fwd:
                y_slab = y_slab + dskip_ref[:, j * LANES:(j + 1) * LANES] * xs[:, j * LANES:(j + 1) * LANES]
            y_ref[r0:r0 + L, j * LANES:(j + 1) * LANES] = y_slab

    @pl.when(leave)
    def _():
        fin_ref[...] = st_ref[...]


def _ssd(geom, direction, xbc, dt, init, conv_w, conv_b, dt_bias, a_log, d_skip):
    nt = geom.n_tiles
    fwd = direction == 0
    tidx = (lambda i: i) if fwd else (lambda i: nt - 1 - i)
    per = TOK_TILE // SSM_HALO
    n_halo = geom.n_tok // SSM_HALO
    n_seq = geom.n_ctx_seq + geom.n_lat_seq
    const = lambda a: pl.BlockSpec(a.shape, lambda i: (0,) * a.ndim)
    st_block = (None, SSM_HEADS, LANES, LANES)
    tile = lambda w: pl.BlockSpec((TOK_TILE, w), lambda i: (tidx(i), 0))
    init_spec = pl.BlockSpec(st_block, lambda i: (jnp.maximum(geom.seq_index(tidx(i)) - geom.n_ctx_seq, 0), 0, 0, 0))
    out_specs = [tile(SSM_WIDTH), pl.BlockSpec(st_block, lambda i: (geom.seq_index(tidx(i)), 0, 0, 0))]
    out_shape = [jax.ShapeDtypeStruct((geom.n_tok, SSM_WIDTH), f32),
                 jax.ShapeDtypeStruct((n_seq, SSM_HEADS, LANES, LANES), f32)]
    state_scratch = pltpu.VMEM((SSM_HEADS, LANES, LANES), f32)
    if fwd:
        in_specs = [
            tile(SSM_XBC),
            pl.BlockSpec((SSM_HALO, SSM_XBC), lambda i: (jnp.maximum(i * per - 1, 0), 0)),
            pl.BlockSpec((SSM_HALO, SSM_XBC), lambda i: (jnp.minimum((i + 1) * per, n_halo - 1), 0)),
            tile(LANES), init_spec, const(conv_w), const(conv_b), const(dt_bias), const(a_log), const(d_skip),
        ]
        args = (xbc, xbc, xbc, dt, init, conv_w, conv_b, dt_bias, a_log, d_skip)
        out_specs.append(tile(SSM_XBC))
        out_shape.append(jax.ShapeDtypeStruct((geom.n_tok, SSM_XBC), f32))
        scratch = [pltpu.VMEM((TOK_TILE + 2 * SSM_HALO, SSM_XBC), f32), state_scratch]
    else:
        in_specs = [tile(SSM_XBC), tile(LANES), init_spec, const(dt_bias), const(a_log), const(d_skip)]
        args = (xbc, dt, init, dt_bias, a_log, d_skip)
        scratch = [state_scratch]
    return pl.pallas_call(
        functools.partial(_ssd_kernel, geom, direction),
        grid=(nt,),
        in_specs=in_specs,
        out_specs=out_specs,
        out_shape=out_shape,
        scratch_shapes=scratch,
        compiler_params=_cparams("arbitrary"),
        name="ssd_scan",
    )(*args)


def _pad_states(s):
    return jnp.tile(jnp.swapaxes(s, -1, -2), (1, 1, LANES // SSM_STATE, LANES // SSM_HEAD_DIM))


def _unpad_states(fin):
    hpg = SSM_HEADS // SSM_GROUPS
    outs = []
    for h in range(SSM_HEADS):
        g, side = h // hpg, h % (LANES // SSM_HEAD_DIM)
        q = fin[:, h, g * SSM_STATE:(g + 1) * SSM_STATE, side * SSM_HEAD_DIM:(side + 1) * SSM_HEAD_DIM]
        outs.append(jnp.swapaxes(q, -1, -2))
    return jnp.stack(outs, axis=1)


def _out_proj_kernel(geom, actx_ref, alat_ref, conv_ref, yf_ref, yb_ref, z_ref, x_ref, mod_ref, sg_ref, w_ref,
                     g2_ref, rwh_ref, rwl_ref, rb_ref, xo_ref, h2_ref, ti_ref, gw_ref, cnt_ref):
    attn = jnp.where(pl.program_id(0) < geom.ctx_tiles, actx_ref[...], alat_ref[...])
    gated = (yf_ref[...] + yb_ref[...]) * _silu(z_ref[...])
    ssm = gated * lax.rsqrt(jnp.mean(gated * gated, axis=-1, keepdims=True) + EPS) * sg_ref[...]
    o1 = ATT_WIDTH
    o2 = ATT_WIDTH + CONV_CH
    mix = (jnp.dot(attn.astype(bf16), w_ref[0:o1, :], preferred_element_type=f32)
           + jnp.dot(conv_ref[...].astype(bf16), w_ref[o1:o2, :], preferred_element_type=f32)
           + jnp.dot(ssm.astype(bf16), w_ref[o2:, :], preferred_element_type=f32))
    x = x_ref[...] + mod_ref[2:3, :] * mix
    xo_ref[...] = x
    y = x * lax.rsqrt(jnp.mean(x * x, axis=-1, keepdims=True) + EPS) * g2_ref[...]
    h2 = y * (1.0 + mod_ref[4:5, :]) + mod_ref[3:4, :]
    h2_ref[...] = h2
    h_hi = h2.astype(bf16)
    h_lo = (h2 - h_hi.astype(f32)).astype(bf16)
    logits = (jnp.dot(h_hi, rwh_ref[...], preferred_element_type=f32)
              + jnp.dot(h_lo, rwh_ref[...], preferred_element_type=f32)
              + jnp.dot(h_hi, rwl_ref[...], preferred_element_type=f32)) + rb_ref[...]
    lane = lax.broadcasted_iota(jnp.int32, (1, LANES), 1)
    lane_f = lane.astype(f32)
    lg = jnp.where(lane < N_EXPERTS, logits, -jnp.inf)
    idx_out = jnp.zeros(logits.shape, jnp.int32)
    gate_out = jnp.zeros(logits.shape, f32)
    top0 = None
    denom = None
    picks = []
    for k in range(TOP_K):
        m = jnp.max(lg, axis=-1, keepdims=True)
        idx = jnp.min(jnp.where(lg == m, lane_f, float(LANES)), axis=-1, keepdims=True).astype(jnp.int32)
        picked = lane == idx
        picks.append(picked)
        lg = jnp.where(picked, -jnp.inf, lg)
        if k == 0:
            top0 = m
        e = jnp.exp(m - top0)
        denom = e if k == 0 else denom + e
        idx_out = jnp.where(lane == k, idx, idx_out)
        gate_out = jnp.where(lane == k, e, gate_out)
    gw_ref[...] = gate_out / denom

    @pl.when(pl.program_id(0) == 0)
    def _():
        cnt_ref[...] = jnp.zeros_like(cnt_ref)

    chosen = picks[0] | picks[1] | picks[2] | picks[3]
    sel = jnp.where(chosen, 1.0, 0.0)
    ri = lax.broadcasted_iota(jnp.int32, (TOK_TILE, TOK_TILE), 0)
    ci = lax.broadcasted_iota(jnp.int32, (TOK_TILE, TOK_TILE), 1)
    before = jnp.where(ci < ri, 1.0, 0.0).astype(bf16)
    prior = jnp.dot(before, sel.astype(bf16), preferred_element_type=f32) + cnt_ref[...]
    for k in range(TOP_K):
        rank = jnp.sum(jnp.where(picks[k], prior, 0.0), axis=-1, keepdims=True).astype(jnp.int32)
        idx_out = jnp.where(lane == TOP_K + k, rank, idx_out)
    ti_ref[...] = idx_out
    cnt_ref[...] = cnt_ref[...] + jnp.sum(sel, axis=0, keepdims=True)


def _out_proj(geom, layer, attn_ctx, attn_lat, conv, y_f, y_b, z, x, mod, ssm_norm_g, w_out, norm2_g, router_w,
              router_b):
    n = geom.n_tok
    tile = lambda w: pl.BlockSpec((TOK_TILE, w), lambda i: (i, 0))
    const = lambda a: pl.BlockSpec(a.shape, lambda i: (0,) * a.ndim)
    nct = geom.ctx_tiles
    rw_hi = router_w.astype(bf16)
    rw_lo = (router_w - rw_hi.astype(f32)).astype(bf16)
    return pl.pallas_call(
        functools.partial(_out_proj_kernel, geom),
        grid=(geom.n_tiles,),
        in_specs=[
            pl.BlockSpec((TOK_TILE, ATT_WIDTH), lambda i: (jnp.minimum(i, nct - 1), 0)),
            pl.BlockSpec((TOK_TILE, ATT_WIDTH), lambda i: (jnp.maximum(i - nct, 0), 0)),
            tile(CONV_CH), tile(SSM_WIDTH), tile(SSM_WIDTH), tile(SSM_WIDTH), tile(D_MODEL),
            pl.BlockSpec((None, None, 6, D_MODEL), lambda i: (layer, geom.cond_row(i), 0, 0)),
            const(ssm_norm_g), const(w_out), const(norm2_g), const(rw_hi), const(rw_lo), const(router_b),
        ],
        out_specs=[tile(D_MODEL), tile(D_MODEL), tile(LANES), tile(LANES),
                   pl.BlockSpec((1, LANES), lambda i: (0, 0))],
        out_shape=[jax.ShapeDtypeStruct((n, D_MODEL), f32), jax.ShapeDtypeStruct((n, D_MODEL), f32),
                   jax.ShapeDtypeStruct((n, LANES), jnp.int32), jax.ShapeDtypeStruct((n, LANES), f32),
                   jax.ShapeDtypeStruct((1, LANES), f32)],
        compiler_params=_cparams("arbitrary"),
        name="out_proj",
    )(attn_ctx, attn_lat, conv, y_f, y_b, z, x, mod, ssm_norm_g, w_out, norm2_g, rw_hi, rw_lo, router_b)


def _moe_kernel(be_ref, rows_ref, rs_ref, x_ref, wg_ref, bg_ref, wu_ref, bu_ref, wd_ref, bd_ref, o_ref,
                wg_s, wu_s, wd_s):
    i = pl.program_id(0)
    prev_e = be_ref[jnp.maximum(i - 1, 0)]
    new_expert = (i == 0) | (be_ref[i] != prev_e)

    @pl.when(new_expert)
    def _():
        wg_s[...] = wg_ref[0].astype(bf16)
        wu_s[...] = wu_ref[0].astype(bf16)
        wd_s[...] = wd_ref[0].astype(bf16)

    def ffn(r):
        x = x_ref[0:r, :].astype(bf16)
        acc = None
        for c0 in range(0, wg_s.shape[1], MOE_FF_CHUNK):
            sl = slice(c0, c0 + MOE_FF_CHUNK)
            g = jnp.dot(x, wg_s[:, sl], preferred_element_type=f32) + bg_ref[0, :, sl]
            u = jnp.dot(x, wu_s[:, sl], preferred_element_type=f32) + bu_ref[0, :, sl]
            g = jnp.minimum(g, SWIGLU_LIMIT)
            u = jnp.clip(u, -SWIGLU_LIMIT, SWIGLU_LIMIT)
            act = g * jax.nn.sigmoid(SWIGLU_ALPHA * g) * (u + 1.0)
            part = jnp.dot(act.astype(bf16), wd_s[sl, :], preferred_element_type=f32)
            acc = part if acc is None else acc + part
        o_ref[0:r, :] = acc + bd_ref[0]
        if r < MOE_TILE:
            o_ref[r:, :] = jnp.zeros((MOE_TILE - r, o_ref.shape[1]), f32)

    rows = rows_ref[i]
    for r in range(MOE_ROW_STEP, MOE_TILE + 1, MOE_ROW_STEP):
        pl.when(rows == r)(functools.partial(ffn, r))

    @pl.when(rows == 0)
    def _():
        o_ref[...] = jnp.zeros_like(o_ref)


def _moe(layer, xs, block_e, block_rows, row_start, w_gate, b_gate, w_up, b_up, w_down, b_down):
    d = xs.shape[1]
    n_blocks = block_e.shape[0]
    d_ff = w_gate.shape[-1]
    w_spec = lambda r, c: pl.BlockSpec((None, 1, r, c), lambda i, be, nu, rs: (layer, be[i], 0, 0))
    b_spec = lambda c: pl.BlockSpec((None, 1, 1, c), lambda i, be, nu, rs: (layer, be[i], 0, 0))
    depth, n_e = b_gate.shape[:2]
    return pl.pallas_call(
        _moe_kernel,
        grid_spec=pltpu.PrefetchScalarGridSpec(
            num_scalar_prefetch=3,
            grid=(n_blocks,),
            in_specs=[pl.BlockSpec((pl.Element(MOE_TILE), pl.Element(d)),
                                   lambda i, be, nu, rs: (rs[i] * SUBLANES, 0)),
                      w_spec(d, d_ff), b_spec(d_ff), w_spec(d, d_ff), b_spec(d_ff), w_spec(d_ff, d), b_spec(d)],
            out_specs=pl.BlockSpec((MOE_TILE, d), lambda i, be, nu, rs: (i, 0)),
            scratch_shapes=[pltpu.VMEM((d, d_ff), bf16), pltpu.VMEM((d, d_ff), bf16), pltpu.VMEM((d_ff, d), bf16)],
        ),
        out_shape=jax.ShapeDtypeStruct((n_blocks * MOE_TILE, d), f32),
        compiler_params=_cparams("arbitrary"),
        name="moe_experts",
    )(block_e, block_rows, row_start, xs, w_gate, b_gate.reshape(depth, n_e, 1, d_ff), w_up,
      b_up.reshape(depth, n_e, 1, d_ff), w_down, b_down.reshape(depth, n_e, 1, d))


def _expert_table(expert_ids, table):
    onehot = expert_ids[:, None] == jnp.arange(N_EXPERTS, dtype=jnp.int32)[None, :]
    return jnp.sum(jnp.where(onehot, table[None, :], 0), axis=1)


def _route(top_idx, rank, counts, n_blocks):
    n = top_idx.shape[0]
    n_assign = n * TOP_K
    i32 = jnp.int32
    flat_e = top_idx.reshape(-1)
    flat_rank = rank.reshape(-1)
    experts = jnp.arange(N_EXPERTS, dtype=i32)
    fill = SUBLANES - 1
    grp_size = (counts + fill) // SUBLANES * SUBLANES
    grp_start = jnp.cumsum(grp_size) - grp_size
    nb = (counts + MOE_TILE - 1) // MOE_TILE
    blk_end = jnp.cumsum(nb)
    blk_start = blk_end - nb
    n_sorted = n_assign + N_EXPERTS * fill
    j = jnp.arange(fill, dtype=i32)
    filler_pos = jnp.where(j[None, :] < (grp_size - counts)[:, None], (grp_start + counts)[:, None] + j[None, :],
                           n_sorted)
    pos = jnp.concatenate([_expert_table(flat_e, grp_start) + flat_rank, filler_pos.reshape(-1)])
    tok = jnp.concatenate([jnp.arange(n_assign, dtype=i32) // TOP_K, jnp.zeros((N_EXPERTS * fill,), i32)])
    assert n < TOKEN_KEY_RANGE and (n_sorted + 1) * TOKEN_KEY_RANGE <= 2 ** 32
    keyed = lax.sort(pos.astype(jnp.uint32) * TOKEN_KEY_RANGE + tok.astype(jnp.uint32))
    sorted_tok = (keyed % TOKEN_KEY_RANGE).astype(i32)
    src_tok = jnp.concatenate([sorted_tok, jnp.zeros((MOE_TILE,), i32)])
    dest = _expert_table(flat_e, blk_start * MOE_TILE) + flat_rank
    blk = jnp.arange(n_blocks, dtype=i32)
    block_e = jnp.minimum(jnp.sum((blk_end[None, :] <= blk[:, None]).astype(i32), axis=1), N_EXPERTS - 1)
    used = blk < blk_end[-1]
    row_start = jnp.where(used, blk * MOE_TILE - _expert_table(block_e, blk_start * MOE_TILE - grp_start), 0)
    filled = jnp.clip(_expert_table(block_e, counts + blk_start * MOE_TILE) - blk * MOE_TILE, 0, MOE_TILE)
    block_rows = jnp.where(used, (filled + MOE_ROW_STEP - 1) // MOE_ROW_STEP * MOE_ROW_STEP, 0)
    return src_tok, dest.reshape(n, TOP_K), block_e, block_rows, row_start // SUBLANES


def _combine_kernel(final, x_ref, rows_ref, gw_ref, mod_ref, *rest):
    ffn = rows_ref[0] * gw_ref[:, 0:1]
    for k in range(1, TOP_K):
        ffn = ffn + rows_ref[k] * gw_ref[:, k:k + 1]
    x = x_ref[...] + mod_ref[5:6, :] * ffn
    if final:
        fg_ref, y_ref = rest
        y_ref[...] = x * lax.rsqrt(jnp.mean(x * x, axis=-1, keepdims=True) + EPS) * fg_ref[...]
    else:
        rest[0][...] = x


def _combine(geom, layer, x, rows, gate_w, mod, final_g):
    n = geom.n_tok
    final = final_g is not None
    tile = pl.BlockSpec((TOK_TILE, D_MODEL), lambda i: (i, 0))
    in_specs = [
        tile,
        pl.BlockSpec((TOP_K, TOK_TILE, D_MODEL), lambda i: (0, i, 0)),
        pl.BlockSpec((TOK_TILE, LANES), lambda i: (i, 0)),
        pl.BlockSpec((None, None, 6, D_MODEL), lambda i: (layer, geom.cond_row(i), 0, 0)),
    ]
    args = [x, rows, gate_w, mod]
    if final:
        in_specs.append(pl.BlockSpec(final_g.shape, lambda i: (0, 0)))
        args.append(final_g)
    return pl.pallas_call(
        functools.partial(_combine_kernel, final),
        grid=(geom.n_tiles,),
        in_specs=in_specs,
        out_specs=tile,
        out_shape=jax.ShapeDtypeStruct((n, D_MODEL), f32),
        compiler_params=_cparams("arbitrary"),
        name="moe_combine",
    )(*args)


def _pad_lanes(a, width=LANES):
    return jnp.pad(a, [(0, 0)] * (a.ndim - 1) + [(0, width - a.shape[-1])])


def kernel(x_prompt, x_sample, c, cache_k, cache_v, state_ssm, c_ctx, norm1_g, norm2_g, w_ada, b_ada, w_in, w_out, attn_lambda, attn_subln_g, conv_dw_w, conv_dw_b, conv_ln_g, conv_ln_b, ssm_conv_w, ssm_conv_b, ssm_dt_bias, ssm_a_log, ssm_d, ssm_norm_g, router_w, router_b, w_gate, b_gate, w_up, b_up, w_down, b_down, final_g):
    n_ctx_seq, ctx_len, d = x_prompt.shape
    n_lat_seq, lat_len, _ = x_sample.shape
    past_len = cache_k.shape[-2]
    geom = _Geom(n_ctx_seq, ctx_len, n_lat_seq, lat_len)
    n = geom.n_tok

    x = jnp.concatenate([x_prompt.reshape(geom.n_ctx, d), x_sample.reshape(geom.n_lat, d)], axis=0)
    cond = jnp.concatenate([c_ctx[None, :], c, jnp.zeros((COND_ROWS - 1 - n_lat_seq, d), f32)], axis=0)
    mod = _ada(cond, w_ada, b_ada).reshape(DEPTH, COND_ROWS, 6, d)
    tables = _rope_tables(geom)

    n_assign = n * TOP_K
    n_blocks = -(-n_assign // MOE_TILE) + N_EXPERTS

    new_k, new_v, new_s = [], [], []
    y_final = None
    for l in range(DEPTH):
        w_main = w_in[l, :, :N_MAIN].astype(bf16)
        w_dt = _pad_lanes(w_in[l, :, N_MAIN:]).astype(bf16)
        q, k, v, glu, z, xbc, dt = _in_proj(geom, l, x, mod, norm1_g[l][None, :], w_main, w_dt, tables)

        lam_init = 0.8 - 0.6 * math.exp(-0.3 * l)
        subln = jnp.tile(attn_subln_g[l], LANES // ATT_V_DIM)[None, :]
        k_cache = cache_k[:, l].transpose(0, 3, 1, 2, 4).reshape(n_lat_seq, past_len, QK_WIDTH)
        v_cache = cache_v[:, l].transpose(0, 2, 1, 3).reshape(n_lat_seq, past_len, ATT_WIDTH)
        k_all = jnp.concatenate([k[geom.n_ctx:].reshape(n_lat_seq, lat_len, QK_WIDTH), k_cache], axis=1).astype(bf16)
        v_all = jnp.concatenate([v[geom.n_ctx:].reshape(n_lat_seq, lat_len, ATT_WIDTH), v_cache], axis=1).astype(bf16)
        attn_ctx = _attention(lam_init, attn_lambda[l], subln, q, k, v, n_ctx_seq, ctx_len, ctx_len, 0, False)
        attn_lat = _attention(lam_init, attn_lambda[l], subln, q, k_all, v_all, n_lat_seq, lat_len,
                              lat_len + past_len, geom.n_ctx, True)

        conv = _cconv(geom, glu, conv_dw_w[l], conv_dw_b[l][None, :], conv_ln_g[l][None, :], conv_ln_b[l][None, :])

        dt_bias = _pad_lanes(ssm_dt_bias[l].reshape(1, 2 * SSM_HEADS))
        a_log = _pad_lanes(ssm_a_log[l].reshape(1, 2 * SSM_HEADS))
        d_skip = jnp.repeat(ssm_d[l], SSM_HEAD_DIM)[None, :]
        conv_w, conv_b = ssm_conv_w[l], ssm_conv_b[l][None, :]
        y_f, fin_f, xbc_act = _ssd(geom, 0, xbc, dt, _pad_states(state_ssm[:, l, 0]), conv_w, conv_b, dt_bias, a_log,
                                   d_skip)
        y_b, fin_b = _ssd(geom, 1, xbc_act, dt, _pad_states(state_ssm[:, l, 1]), conv_w, conv_b, dt_bias, a_log, d_skip)
        ys = [y_f, y_b]
        fins = [_unpad_states(fin_f[:n_ctx_seq]), _unpad_states(fin_b[:n_ctx_seq])]

        x, h2, routed, gate_w, totals = _out_proj(
            geom, l, attn_ctx, attn_lat, conv, ys[0], ys[1], z, x, mod, ssm_norm_g[l][None, :], w_out[l].astype(bf16),
            norm2_g[l][None, :], _pad_lanes(router_w[l]), _pad_lanes(router_b[l][None, :]))

        src_tok, dest, block_e, block_rows, row_start = _route(
            routed[:, :TOP_K], routed[:, TOP_K:2 * TOP_K], totals[0, :N_EXPERTS].astype(jnp.int32), n_blocks)
        xs = jnp.take(h2, src_tok, axis=0, mode="clip")
        eo = _moe(l, xs, block_e, block_rows, row_start, w_gate, b_gate, w_up, b_up, w_down, b_down)
        rows = jnp.take(eo, dest.T.reshape(-1), axis=0, mode="clip").reshape(TOP_K, n, d)
        if l == DEPTH - 1:
            y_final = _combine(geom, l, x, rows, gate_w, mod, final_g[None, :])
        else:
            x = _combine(geom, l, x, rows, gate_w, mod, None)

        new_k.append(k[:geom.n_ctx].reshape(n_ctx_seq, ctx_len, ATT_HEADS, 2, ATT_QK_DIM).transpose(0, 2, 3, 1, 4))
        new_v.append(v[:geom.n_ctx].reshape(n_ctx_seq, ctx_len, ATT_HEADS, ATT_V_DIM).transpose(0, 2, 1, 3))
        new_s.append(jnp.stack(fins, axis=1))

    y_prompt = y_final[:geom.n_ctx].reshape(n_ctx_seq, ctx_len, d)
    y_sample = y_final[geom.n_ctx:].reshape(n_lat_seq, lat_len, d)
    return (y_prompt, y_sample, jnp.stack(new_k, axis=1), jnp.stack(new_v, axis=1), jnp.stack(new_s, axis=1))
```

```python
import functools
import math

import numpy as np
import jax
import jax.numpy as jnp
from jax import lax
from jax.experimental import pallas as pl
from jax.experimental.pallas import tpu as pltpu

f32 = jnp.float32
bf16 = jnp.bfloat16
HIGHEST = lax.Precision.HIGHEST

D_MODEL = 1024
DEPTH = 2
GRID_W = 64
ATT_HEADS = 4
ATT_QK_DIM = 32
ATT_V_DIM = 64
ATT_WIDTH = 256
QK_WIDTH = 256
ROPE_AXIS_DIM = 16
ROPE_BASE = 10000.0
CONV_CH = 256
CONV_K = 31
SSM_WIDTH = 512
SSM_HEAD_DIM = 64
SSM_HEADS = 8
SSM_GROUPS = 2
SSM_STATE = 64
SSM_CONV_K = 5
SSM_XBC = 768
N_MAIN = 2560
N_EXPERTS = 32
TOP_K = 4
SWIGLU_LIMIT = 7.0
SWIGLU_ALPHA = 1.702
EPS = 1e-6

LANES = 128
SUBLANES = 8
TOK_TILE = 256
SSD_CHUNK = 128
ATT_Q_TILE = 512
CONV_HALO = 16
SSM_HALO = 8
MOE_TILE = 512
MOE_FF_CHUNK = 256
TOKEN_KEY_RANGE = 2 ** 15
COND_ROWS = 16
VMEM_LIMIT = 56 * 1024 * 1024


def _cparams(*sem):
    return pltpu.CompilerParams(dimension_semantics=sem, vmem_limit_bytes=VMEM_LIMIT)


def _silu(x):
    return x * jax.nn.sigmoid(x)


def _ada_kernel(cond_ref, w_ref, b_ref, o_ref):
    s = _silu(cond_ref[...]).astype(bf16)
    o_ref[0] = jnp.dot(s, w_ref[0].astype(bf16), preferred_element_type=f32) + b_ref[0]


def _ada(cond, w_ada, b_ada):
    depth, d, n6 = w_ada.shape
    tn = 1536
    return pl.pallas_call(
        _ada_kernel,
        grid=(depth, n6 // tn),
        in_specs=[
            pl.BlockSpec((COND_ROWS, d), lambda l, j: (0, 0)),
            pl.BlockSpec((1, d, tn), lambda l, j: (l, 0, j)),
            pl.BlockSpec((1, 1, tn), lambda l, j: (l, 0, j)),
        ],
        out_specs=pl.BlockSpec((1, COND_ROWS, tn), lambda l, j: (l, 0, j)),
        out_shape=jax.ShapeDtypeStruct((depth, COND_ROWS, n6), f32),
        compiler_params=_cparams("arbitrary", "arbitrary"),
        name="ada",
    )(cond, w_ada, b_ada.reshape(depth, 1, n6))


class _Geom:
    def __init__(self, n_ctx_seq, ctx_len, n_lat_seq, lat_len):
        self.n_ctx_seq, self.ctx_len, self.n_lat_seq, self.lat_len = n_ctx_seq, ctx_len, n_lat_seq, lat_len
        self.n_ctx = n_ctx_seq * ctx_len
        self.n_lat = n_lat_seq * lat_len
        self.n_tok = self.n_ctx + self.n_lat
        assert ctx_len % TOK_TILE == 0 and lat_len % TOK_TILE == 0
        self.ctx_tiles = self.n_ctx // TOK_TILE
        self.tiles_per_ctx = ctx_len // TOK_TILE
        self.tiles_per_lat = lat_len // TOK_TILE
        self.n_tiles = self.n_tok // TOK_TILE

    def cond_row(self, i):
        return jnp.where(i < self.ctx_tiles, 0, 1 + (i - self.ctx_tiles) // self.tiles_per_lat)

    def seq_pos(self, i):
        j_ctx = i % self.tiles_per_ctx
        j_lat = (i - self.ctx_tiles) % self.tiles_per_lat
        is_ctx = i < self.ctx_tiles
        first = jnp.where(is_ctx, j_ctx == 0, j_lat == 0)
        last = jnp.where(is_ctx, j_ctx == self.tiles_per_ctx - 1, j_lat == self.tiles_per_lat - 1)
        return first, last

    def seq_index(self, i):
        return jnp.where(i < self.ctx_tiles, i // self.tiles_per_ctx,
                         self.n_ctx_seq + (i - self.ctx_tiles) // self.tiles_per_lat)


def _rope_tables(geom):
    t = np.arange(geom.lat_len)
    row = (t // GRID_W).astype(np.float64)
    col = (t % GRID_W).astype(np.float64)
    inv_freq = 1.0 / (ROPE_BASE ** (np.arange(0, ROPE_AXIS_DIM, 2, dtype=np.float64) / ROPE_AXIS_DIM))
    ang_r = row[:, None] * inv_freq[None, :]
    ang_c = col[:, None] * inv_freq[None, :]
    ang = np.concatenate([ang_r, ang_r, ang_c, ang_c], axis=-1)
    cos, sin = np.cos(ang), np.sin(ang)
    half = ROPE_AXIS_DIM // 2
    lane = np.arange(ATT_QK_DIM)
    first_half = (lane % ROPE_AXIS_DIM) < half
    sin_a = np.where(first_half[None, :], -sin, 0.0)
    sin_b = np.where(first_half[None, :], 0.0, sin)
    rep = LANES // ATT_QK_DIM

    def full(tab, ident):
        tab = np.tile(tab, (1, rep))
        return jnp.asarray(np.concatenate([np.full((TOK_TILE, LANES), ident), tab], axis=0), f32)

    return full(cos, 1.0), full(sin_a, 0.0), full(sin_b, 0.0)


def _in_proj_kernel(x_ref, mod_ref, g_ref, w_ref, wdt_ref, cos_ref, sa_ref, sb_ref,
                    q_ref, k_ref, v_ref, glu_ref, z_ref, xbc_ref, dt_ref):
    x = x_ref[...]
    y = x * lax.rsqrt(jnp.mean(x * x, axis=-1, keepdims=True) + EPS) * g_ref[...]
    h = (y * (1.0 + mod_ref[1:2, :]) + mod_ref[0:1, :]).astype(bf16)
    p = jnp.dot(h, w_ref[...], preferred_element_type=f32)
    dt_ref[...] = jnp.dot(h, wdt_ref[...], preferred_element_type=f32)
    cos, sa, sb = cos_ref[...], sa_ref[...], sb_ref[...]
    half = ROPE_AXIS_DIM // 2

    def rope(u):
        return u * cos + pltpu.roll(u, LANES - half, 1) * sa + pltpu.roll(u, half, 1) * sb

    scale = ATT_QK_DIM ** -0.5 * math.log2(math.e)
    for j in range(QK_WIDTH // LANES):
        lo = j * LANES
        q_ref[:, lo:lo + LANES] = (rope(p[:, lo:lo + LANES]) * scale).astype(bf16)
        k_ref[:, lo:lo + LANES] = rope(p[:, QK_WIDTH + lo:QK_WIDTH + lo + LANES])
    o = 2 * QK_WIDTH
    v_ref[...] = p[:, o:o + ATT_WIDTH]
    o += ATT_WIDTH
    glu_ref[...] = p[:, o:o + 2 * CONV_CH]
    o += 2 * CONV_CH
    z_ref[...] = p[:, o:o + SSM_WIDTH]
    o += SSM_WIDTH
    xbc_ref[...] = p[:, o:o + SSM_XBC]


def _in_proj(geom, layer, x, mod, norm_g, w_main, w_dt, tables):
    n = geom.n_tok
    tile = lambda w: pl.BlockSpec((TOK_TILE, w), lambda i: (i, 0))
    const = lambda a: pl.BlockSpec(a.shape, lambda i: (0,) * a.ndim)
    tab_spec = pl.BlockSpec(
        (TOK_TILE, LANES),
        lambda i: (jnp.where(i < geom.ctx_tiles, 0, 1 + (i - geom.ctx_tiles) % geom.tiles_per_lat), 0))
    widths = (QK_WIDTH, QK_WIDTH, ATT_WIDTH, 2 * CONV_CH, SSM_WIDTH, SSM_XBC, LANES)
    dtypes = (bf16, f32, f32, f32, f32, f32, f32)
    return pl.pallas_call(
        _in_proj_kernel,
        grid=(geom.n_tiles,),
        in_specs=[
            tile(D_MODEL),
            pl.BlockSpec((None, None, 6, D_MODEL), lambda i: (layer, geom.cond_row(i), 0, 0)),
            const(norm_g), const(w_main), const(w_dt), tab_spec, tab_spec, tab_spec,
        ],
        out_specs=[tile(w) for w in widths],
        out_shape=[jax.ShapeDtypeStruct((n, w), dt) for w, dt in zip(widths, dtypes)],
        compiler_params=_cparams("arbitrary"),
        name="in_proj",
    )(x, mod, norm_g, w_main, w_dt, *tables)


def _attn_kernel(lam_init, al_ref, g_ref, q_ref, k_ref, v_ref, o_ref):
    al = al_ref[...]
    lam = (jnp.exp(jnp.sum(al[0:1] * al[1:2], axis=-1, keepdims=True))
           - jnp.exp(jnp.sum(al[2:3] * al[3:4], axis=-1, keepdims=True)) + lam_init)
    lane = lax.broadcasted_iota(jnp.int32, (1, LANES), 1)
    heads_per_slab = LANES // ATT_V_DIM
    for j in range(ATT_WIDTH // LANES):
        sl = slice(j * LANES, (j + 1) * LANES)
        qs = q_ref[:, sl]
        ks = k_ref[:, sl].astype(bf16)
        vs = v_ref[:, sl].astype(bf16)
        a_heads = []
        for hl in range(heads_per_slab):
            own = (lane >= hl * ATT_V_DIM) & (lane < (hl + 1) * ATT_V_DIM)
            vh = jnp.where(own, vs, jnp.ones_like(vs))
            sum_lane = (1 - hl) * ATT_V_DIM
            comps = []
            for c in range(2):
                lo = hl * ATT_V_DIM + c * ATT_QK_DIM
                qm = jnp.where((lane >= lo) & (lane < lo + ATT_QK_DIM), qs, jnp.zeros_like(qs))
                s = lax.dot_general(qm, ks, (((1,), (1,)), ((), ())), preferred_element_type=f32)
                e = jnp.exp2(s - jnp.max(s, axis=-1, keepdims=True))
                pv = jnp.dot(e.astype(bf16), vh, preferred_element_type=f32)
                comps.append(pv / pv[:, sum_lane:sum_lane + 1])
            a_heads.append(comps[0] - lam * comps[1])
        a = jnp.where(lane < ATT_V_DIM, a_heads[0], a_heads[1])
        a2 = a * a
        ms0 = jnp.sum(jnp.where(lane < ATT_V_DIM, a2, 0.0), axis=-1, keepdims=True)
        ms1 = jnp.sum(jnp.where(lane < ATT_V_DIM, 0.0, a2), axis=-1, keepdims=True)
        ms = jnp.where(lane < ATT_V_DIM, ms0, ms1) * (1.0 / ATT_V_DIM)
        o_ref[:, sl] = a * lax.rsqrt(ms + EPS) * g_ref[...] * (1.0 - lam_init)


def _attention(lam_init, attn_lambda, subln_g, q, k, v, n_seq, q_len, k_len, q_row0, kv_is_3d):
    tq = min(ATT_Q_TILE, q_len)
    qb = q_len // tq
    q0 = q_row0 // tq
    if kv_is_3d:
        kv_spec = pl.BlockSpec((None, k_len, ATT_WIDTH), lambda b, i: (b, 0, 0))
    else:
        kv_spec = pl.BlockSpec((k_len, ATT_WIDTH), lambda b, i: (q_row0 // k_len + b, 0))
    return pl.pallas_call(
        functools.partial(_attn_kernel, lam_init),
        grid=(n_seq, qb),
        in_specs=[
            pl.BlockSpec(attn_lambda.shape, lambda b, i: (0, 0)),
            pl.BlockSpec(subln_g.shape, lambda b, i: (0, 0)),
            pl.BlockSpec((tq, ATT_WIDTH), lambda b, i: (q0 + b * qb + i, 0)),
            kv_spec, kv_spec,
        ],
        out_specs=pl.BlockSpec((tq, ATT_WIDTH), lambda b, i: (b * qb + i, 0)),
        out_shape=jax.ShapeDtypeStruct((n_seq * q_len, ATT_WIDTH), f32),
        compiler_params=_cparams("arbitrary", "arbitrary"),
        name="attention",
    )(attn_lambda, subln_g, q, k, v)


def _halo_specs(geom, width, halo):
    per = TOK_TILE // halo
    n_halo = geom.n_tok // halo
    prev = pl.BlockSpec((halo, width), lambda i: (jnp.maximum(i * per - 1, 0), 0))
    nxt = pl.BlockSpec((halo, width), lambda i: (jnp.minimum((i + 1) * per, n_halo - 1), 0))
    return prev, nxt


def _cconv_kernel(geom, cur_ref, prev_ref, next_ref, w_ref, b_ref, lg_ref, lb_ref, o_ref, pad_ref, shift_ref):
    first, last = geom.seq_pos(pl.program_id(0))

    def glu(ref):
        return ref[:, :CONV_CH] * jax.nn.sigmoid(ref[:, CONV_CH:])

    pad_ref[0:CONV_HALO, :] = jnp.where(first, 0.0, glu(prev_ref))
    pad_ref[CONV_HALO:CONV_HALO + TOK_TILE, :] = glu(cur_ref)
    pad_ref[CONV_HALO + TOK_TILE:, :] = jnp.where(last, 0.0, glu(next_ref))
    n_shifted = TOK_TILE + 2 * CONV_HALO - SUBLANES
    for j in range(1, SUBLANES):
        shift_ref[j - 1, 0:n_shifted, :] = pad_ref[j:j + n_shifted, :]
    rows = 64
    base = CONV_HALO - CONV_K // 2
    for r0 in range(0, TOK_TILE, rows):
        acc = jnp.broadcast_to(b_ref[...], (rows, CONV_CH))
        for k in range(CONV_K):
            q, j = divmod(r0 + base + k, SUBLANES)
            lo = q * SUBLANES
            window = pad_ref[lo:lo + rows, :] if j == 0 else shift_ref[j - 1, lo:lo + rows, :]
            acc = acc + w_ref[k:k + 1, :] * window
        xc = acc - jnp.mean(acc, axis=-1, keepdims=True)
        var = jnp.mean(xc * xc, axis=-1, keepdims=True)
        o_ref[r0:r0 + rows, :] = _silu(xc * lax.rsqrt(var + EPS) * lg_ref[...] + lb_ref[...])


def _cconv(geom, glu, w, b, ln_g, ln_b):
    prev, nxt = _halo_specs(geom, 2 * CONV_CH, CONV_HALO)
    const = lambda a: pl.BlockSpec(a.shape, lambda i: (0,) * a.ndim)
    return pl.pallas_call(
        functools.partial(_cconv_kernel, geom),
        grid=(geom.n_tiles,),
        in_specs=[pl.BlockSpec((TOK_TILE, 2 * CONV_CH), lambda i: (i, 0)), prev, nxt,
                  const(w), const(b), const(ln_g), const(ln_b)],
        out_specs=pl.BlockSpec((TOK_TILE, CONV_CH), lambda i: (i, 0)),
        out_shape=jax.ShapeDtypeStruct((geom.n_tok, CONV_CH), f32),
        scratch_shapes=[pltpu.VMEM((TOK_TILE + 2 * CONV_HALO, CONV_CH), f32),
                        pltpu.VMEM((SUBLANES - 1, TOK_TILE + 2 * CONV_HALO, CONV_CH), f32)],
        compiler_params=_cparams("arbitrary"),
        name="conformer_conv",
    )(glu, glu, glu, w, b, ln_g, ln_b)


def _softplus(x):
    return jnp.maximum(x, 0.0) + jnp.log1p(jnp.exp(-jnp.abs(x)))


def _ssd_kernel(geom, direction, *refs):
    fwd = direction == 0
    if fwd:
        (cur_ref, prev_ref, next_ref, dt_ref, init_ref, cw_ref, cb_ref, dtb_ref, alog_ref, dskip_ref,
         y_ref, fin_ref, xc_ref, pad_ref, st_ref) = refs
    else:
        xc_ref, dt_ref, init_ref, dtb_ref, alog_ref, dskip_ref, y_ref, fin_ref, st_ref = refs
    i = pl.program_id(0)
    t = i if fwd else geom.n_tiles - 1 - i
    first, last = geom.seq_pos(t)
    enter, leave = (first, last) if fwd else (last, first)
    is_lat = t >= geom.ctx_tiles

    @pl.when(enter)
    def _():
        st_ref[...] = jnp.where(is_lat, init_ref[...], 0.0)

    if fwd:
        pad_ref[0:SSM_HALO, :] = jnp.where(first, 0.0, prev_ref[...])
        pad_ref[SSM_HALO:SSM_HALO + TOK_TILE, :] = cur_ref[...]
        pad_ref[SSM_HALO + TOK_TILE:, :] = jnp.where(last, 0.0, next_ref[...])
        base = SSM_HALO - SSM_CONV_K // 2
        acc = jnp.broadcast_to(cb_ref[...], (TOK_TILE, SSM_XBC))
        for k in range(SSM_CONV_K):
            acc = acc + cw_ref[k:k + 1, :] * pad_ref[base + k:base + k + TOK_TILE, :]
        xc_ref[...] = _silu(acc)

    L = SSD_CHUNK
    a_row = -jnp.exp(alog_ref[...])
    ri = lax.broadcasted_iota(jnp.int32, (L, L), 0)
    ci = lax.broadcasted_iota(jnp.int32, (L, L), 1)
    keep = (ci <= ri) if fwd else (ci >= ri)
    tri_col = keep.astype(f32)
    tri_row = ((ri <= ci) if fwd else (ri >= ci)).astype(f32)
    lane = lax.broadcasted_iota(jnp.int32, (1, LANES), 1)
    er = lax.broadcasted_iota(jnp.int32, (LANES, SSM_WIDTH), 0)
    ec = lax.broadcasted_iota(jnp.int32, (LANES, SSM_WIDTH), 1)
    expand = (er == direction * SSM_HEADS + ec // SSM_HEAD_DIM).astype(f32)
    heads_per_group = SSM_HEADS // SSM_GROUPS
    heads_per_slab = LANES // SSM_HEAD_DIM
    tot_at = L - 1 if fwd else 0

    chunks = range(TOK_TILE // L)
    for cidx in (chunks if fwd else reversed(chunks)):
        r0 = cidx * L
        xs = xc_ref[r0:r0 + L, 0:SSM_WIDTH]
        bm = xc_ref[r0:r0 + L, SSM_WIDTH:SSM_WIDTH + LANES]
        cm = xc_ref[r0:r0 + L, SSM_WIDTH + LANES:SSM_XBC]
        dt = _softplus(dt_ref[r0:r0 + L, :] + dtb_ref[...])
        da = dt * a_row
        acum_col = jnp.dot(tri_col, da, precision=HIGHEST, preferred_element_type=f32)
        acum_row = jnp.dot(da.T, tri_row, precision=HIGHEST, preferred_element_type=f32)
        tot_row = acum_col[tot_at:tot_at + 1, :]
        tot_col = acum_row[:, tot_at:tot_at + 1]
        decay_out = jnp.exp(tot_col - acum_row)
        decay_in = jnp.exp(acum_col)
        chunk_decay = jnp.exp(tot_row)
        xdt = (xs * jnp.dot(dt, expand, precision=HIGHEST, preferred_element_type=f32)).astype(bf16)
        bt = bm.T
        bb = bm.astype(bf16)
        ys = []
        for g in range(SSM_GROUPS):
            in_group = (lane >= g * SSM_STATE) & (lane < (g + 1) * SSM_STATE)
            cg = jnp.where(in_group, cm, 0.0)
            gmat = lax.dot_general(cg.astype(bf16), bb, (((1,), (1,)), ((), ())), preferred_element_type=f32)
            for hh in range(heads_per_group):
                h = g * heads_per_group + hh
                hl = direction * SSM_HEADS + h
                slab = xdt[:, (h // heads_per_slab) * LANES:(h // heads_per_slab + 1) * LANES]
                seg = acum_col[:, hl:hl + 1] - acum_row[hl:hl + 1, :]
                m = (gmat * jnp.exp(jnp.where(keep, seg, -jnp.inf))).astype(bf16)
                y_h = jnp.dot(m, slab, preferred_element_type=f32)
                c_in = (cg * decay_in[:, hl:hl + 1]).astype(bf16)
                state = st_ref[h]
                y_h = y_h + jnp.dot(c_in, state.astype(bf16), preferred_element_type=f32)
                b_out = (bt * decay_out[hl:hl + 1, :]).astype(bf16)
                st_ref[h] = chunk_decay[:, hl:hl + 1] * state + jnp.dot(b_out, slab, preferred_element_type=f32)
                ys.append(y_h)
        for j in range(SSM_WIDTH // LANES):
            y_slab = jnp.where(lane < SSM_HEAD_DIM, ys[heads_per_slab * j], ys[heads_per_slab * j + 1])
            if fwd:
                y_slab = y_slab + dskip_ref[:, j * LANES:(j + 1) * LANES] * xs[:, j * LANES:(j + 1) * LANES]
            y_ref[r0:r0 + L, j * LANES:(j + 1) * LANES] = y_slab

    @pl.when(leave)
    def _():
        fin_ref[...] = st_ref[...]


def _ssd(geom, direction, xbc, dt, init, conv_w, conv_b, dt_bias, a_log, d_skip):
    nt = geom.n_tiles
    fwd = direction == 0
    tidx = (lambda i: i) if fwd else (lambda i: nt - 1 - i)
    per = TOK_TILE // SSM_HALO
    n_halo = geom.n_tok // SSM_HALO
    n_seq = geom.n_ctx_seq + geom.n_lat_seq
    const = lambda a: pl.BlockSpec(a.shape, lambda i: (0,) * a.ndim)
    st_block = (None, SSM_HEADS, LANES, LANES)
    tile = lambda w: pl.BlockSpec((TOK_TILE, w), lambda i: (tidx(i), 0))
    init_spec = pl.BlockSpec(st_block, lambda i: (jnp.maximum(geom.seq_index(tidx(i)) - geom.n_ctx_seq, 0), 0, 0, 0))
    out_specs = [tile(SSM_WIDTH), pl.BlockSpec(st_block, lambda i: (geom.seq_index(tidx(i)), 0, 0, 0))]
    out_shape = [jax.ShapeDtypeStruct((geom.n_tok, SSM_WIDTH), f32),
                 jax.ShapeDtypeStruct((n_seq, SSM_HEADS, LANES, LANES), f32)]
    state_scratch = pltpu.VMEM((SSM_HEADS, LANES, LANES), f32)
    if fwd:
        in_specs = [
            tile(SSM_XBC),
            pl.BlockSpec((SSM_HALO, SSM_XBC), lambda i: (jnp.maximum(i * per - 1, 0), 0)),
            pl.BlockSpec((SSM_HALO, SSM_XBC), lambda i: (jnp.minimum((i + 1) * per, n_halo - 1), 0)),
            tile(LANES), init_spec, const(conv_w), const(conv_b), const(dt_bias), const(a_log), const(d_skip),
        ]
        args = (xbc, xbc, xbc, dt, init, conv_w, conv_b, dt_bias, a_log, d_skip)
        out_specs.append(tile(SSM_XBC))
        out_shape.append(jax.ShapeDtypeStruct((geom.n_tok, SSM_XBC), f32))
        scratch = [pltpu.VMEM((TOK_TILE + 2 * SSM_HALO, SSM_XBC), f32), state_scratch]
    else:
        in_specs = [tile(SSM_XBC), tile(LANES), init_spec, const(dt_bias), const(a_log), const(d_skip)]
        args = (xbc, dt, init, dt_bias, a_log, d_skip)
        scratch = [state_scratch]
    return pl.pallas_call(
        functools.partial(_ssd_kernel, geom, direction),
        grid=(nt,),
        in_specs=in_specs,
        out_specs=out_specs,
        out_shape=out_shape,
        scratch_shapes=scratch,
        compiler_params=_cparams("arbitrary"),
        name="ssd_scan",
    )(*args)


def _pad_states(s):
    return jnp.tile(jnp.swapaxes(s, -1, -2), (1, 1, LANES // SSM_STATE, LANES // SSM_HEAD_DIM))


def _unpad_states(fin):
    hpg = SSM_HEADS // SSM_GROUPS
    outs = []
    for h in range(SSM_HEADS):
        g, side = h // hpg, h % (LANES // SSM_HEAD_DIM)
        q = fin[:, h, g * SSM_STATE:(g + 1) * SSM_STATE, side * SSM_HEAD_DIM:(side + 1) * SSM_HEAD_DIM]
        outs.append(jnp.swapaxes(q, -1, -2))
    return jnp.stack(outs, axis=1)


def _out_proj_kernel(geom, actx_ref, alat_ref, conv_ref, yf_ref, yb_ref, z_ref, x_ref, mod_ref, sg_ref, w_ref,
                     g2_ref, rwh_ref, rwl_ref, rb_ref, xo_ref, h2_ref, ti_ref, gw_ref, cnt_ref):
    attn = jnp.where(pl.program_id(0) < geom.ctx_tiles, actx_ref[...], alat_ref[...])
    gated = (yf_ref[...] + yb_ref[...]) * _silu(z_ref[...])
    ssm = gated * lax.rsqrt(jnp.mean(gated * gated, axis=-1, keepdims=True) + EPS) * sg_ref[...]
    o1 = ATT_WIDTH
    o2 = ATT_WIDTH + CONV_CH
    mix = (jnp.dot(attn.astype(bf16), w_ref[0:o1, :], preferred_element_type=f32)
           + jnp.dot(conv_ref[...].astype(bf16), w_ref[o1:o2, :], preferred_element_type=f32)
           + jnp.dot(ssm.astype(bf16), w_ref[o2:, :], preferred_element_type=f32))
    x = x_ref[...] + mod_ref[2:3, :] * mix
    xo_ref[...] = x
    y = x * lax.rsqrt(jnp.mean(x * x, axis=-1, keepdims=True) + EPS) * g2_ref[...]
    h2 = y * (1.0 + mod_ref[4:5, :]) + mod_ref[3:4, :]
    h2_ref[...] = h2
    h_hi = h2.astype(bf16)
    h_lo = (h2 - h_hi.astype(f32)).astype(bf16)
    logits = (jnp.dot(h_hi, rwh_ref[...], preferred_element_type=f32)
              + jnp.dot(h_lo, rwh_ref[...], preferred_element_type=f32)
              + jnp.dot(h_hi, rwl_ref[...], preferred_element_type=f32)) + rb_ref[...]
    lane = lax.broadcasted_iota(jnp.int32, (1, LANES), 1)
    lane_f = lane.astype(f32)
    lg = jnp.where(lane < N_EXPERTS, logits, -jnp.inf)
    idx_out = jnp.zeros(logits.shape, jnp.int32)
    gate_out = jnp.zeros(logits.shape, f32)
    top0 = None
    denom = None
    picks = []
    for k in range(TOP_K):
        m = jnp.max(lg, axis=-1, keepdims=True)
        idx = jnp.min(jnp.where(lg == m, lane_f, float(LANES)), axis=-1, keepdims=True).astype(jnp.int32)
        picked = lane == idx
        picks.append(picked)
        lg = jnp.where(picked, -jnp.inf, lg)
        if k == 0:
            top0 = m
        e = jnp.exp(m - top0)
        denom = e if k == 0 else denom + e
        idx_out = jnp.where(lane == k, idx, idx_out)
        gate_out = jnp.where(lane == k, e, gate_out)
    gw_ref[...] = gate_out / denom

    @pl.when(pl.program_id(0) == 0)
    def _():
        cnt_ref[...] = jnp.zeros_like(cnt_ref)

    chosen = picks[0] | picks[1] | picks[2] | picks[3]
    sel = jnp.where(chosen, 1.0, 0.0)
    ri = lax.broadcasted_iota(jnp.int32, (TOK_TILE, TOK_TILE), 0)
    ci = lax.broadcasted_iota(jnp.int32, (TOK_TILE, TOK_TILE), 1)
    before = jnp.where(ci < ri, 1.0, 0.0).astype(bf16)
    prior = jnp.dot(before, sel.astype(bf16), preferred_element_type=f32) + cnt_ref[...]
    for k in range(TOP_K):
        rank = jnp.sum(jnp.where(picks[k], prior, 0.0), axis=-1, keepdims=True).astype(jnp.int32)
        idx_out = jnp.where(lane == TOP_K + k, rank, idx_out)
    ti_ref[...] = idx_out
    cnt_ref[...] = cnt_ref[...] + jnp.sum(sel, axis=0, keepdims=True)


def _out_proj(geom, layer, attn_ctx, attn_lat, conv, y_f, y_b, z, x, mod, ssm_norm_g, w_out, norm2_g, router_w,
              router_b):
    n = geom.n_tok
    tile = lambda w: pl.BlockSpec((TOK_TILE, w), lambda i: (i, 0))
    const = lambda a: pl.BlockSpec(a.shape, lambda i: (0,) * a.ndim)
    nct = geom.ctx_tiles
    rw_hi = router_w.astype(bf16)
    rw_lo = (router_w - rw_hi.astype(f32)).astype(bf16)
    return pl.pallas_call(
        functools.partial(_out_proj_kernel, geom),
        grid=(geom.n_tiles,),
        in_specs=[
            pl.BlockSpec((TOK_TILE, ATT_WIDTH), lambda i: (jnp.minimum(i, nct - 1), 0)),
            pl.BlockSpec((TOK_TILE, ATT_WIDTH), lambda i: (jnp.maximum(i - nct, 0), 0)),
            tile(CONV_CH), tile(SSM_WIDTH), tile(SSM_WIDTH), tile(SSM_WIDTH), tile(D_MODEL),
            pl.BlockSpec((None, None, 6, D_MODEL), lambda i: (layer, geom.cond_row(i), 0, 0)),
            const(ssm_norm_g), const(w_out), const(norm2_g), const(rw_hi), const(rw_lo), const(router_b),
        ],
        out_specs=[tile(D_MODEL), tile(D_MODEL), tile(LANES), tile(LANES),
                   pl.BlockSpec((1, LANES), lambda i: (0, 0))],
        out_shape=[jax.ShapeDtypeStruct((n, D_MODEL), f32), jax.ShapeDtypeStruct((n, D_MODEL), f32),
                   jax.ShapeDtypeStruct((n, LANES), jnp.int32), jax.ShapeDtypeStruct((n, LANES), f32),
                   jax.ShapeDtypeStruct((1, LANES), f32)],
        compiler_params=_cparams("arbitrary"),
        name="out_proj",
    )(attn_ctx, attn_lat, conv, y_f, y_b, z, x, mod, ssm_norm_g, w_out, norm2_g, rw_hi, rw_lo, router_b)


def _moe_kernel(be_ref, nused_ref, rs_ref, x_ref, wg_ref, bg_ref, wu_ref, bu_ref, wd_ref, bd_ref, o_ref,
                wg_s, wu_s, wd_s):
    i = pl.program_id(0)
    prev_e = be_ref[jnp.maximum(i - 1, 0)]
    new_expert = (i == 0) | (be_ref[i] != prev_e)

    @pl.when(new_expert)
    def _():
        wg_s[...] = wg_ref[0].astype(bf16)
        wu_s[...] = wu_ref[0].astype(bf16)
        wd_s[...] = wd_ref[0].astype(bf16)

    @pl.when(i < nused_ref[0])
    def _():
        x = x_ref[...].astype(bf16)
        acc = None
        for c0 in range(0, wg_s.shape[1], MOE_FF_CHUNK):
            sl = slice(c0, c0 + MOE_FF_CHUNK)
            g = jnp.dot(x, wg_s[:, sl], preferred_element_type=f32) + bg_ref[0, :, sl]
            u = jnp.dot(x, wu_s[:, sl], preferred_element_type=f32) + bu_ref[0, :, sl]
            g = jnp.minimum(g, SWIGLU_LIMIT)
            u = jnp.clip(u, -SWIGLU_LIMIT, SWIGLU_LIMIT)
            act = g * jax.nn.sigmoid(SWIGLU_ALPHA * g) * (u + 1.0)
            part = jnp.dot(act.astype(bf16), wd_s[sl, :], preferred_element_type=f32)
            acc = part if acc is None else acc + part
        o_ref[...] = acc + bd_ref[0]

    @pl.when(i >= nused_ref[0])
    def _():
        o_ref[...] = jnp.zeros_like(o_ref)


def _moe(layer, xs, block_e, n_used, row_start, w_gate, b_gate, w_up, b_up, w_down, b_down):
    d = xs.shape[1]
    n_blocks = block_e.shape[0]
    d_ff = w_gate.shape[-1]
    w_spec = lambda r, c: pl.BlockSpec((None, 1, r, c), lambda i, be, nu, rs: (layer, be[i], 0, 0))
    b_spec = lambda c: pl.BlockSpec((None, 1, 1, c), lambda i, be, nu, rs: (layer, be[i], 0, 0))
    depth, n_e = b_gate.shape[:2]
    return pl.pallas_call(
        _moe_kernel,
        grid_spec=pltpu.PrefetchScalarGridSpec(
            num_scalar_prefetch=3,
            grid=(n_blocks,),
            in_specs=[pl.BlockSpec((pl.Element(MOE_TILE), pl.Element(d)),
                                   lambda i, be, nu, rs: (rs[i] * SUBLANES, 0)),
                      w_spec(d, d_ff), b_spec(d_ff), w_spec(d, d_ff), b_spec(d_ff), w_spec(d_ff, d), b_spec(d)],
            out_specs=pl.BlockSpec((MOE_TILE, d), lambda i, be, nu, rs: (i, 0)),
            scratch_shapes=[pltpu.VMEM((d, d_ff), bf16), pltpu.VMEM((d, d_ff), bf16), pltpu.VMEM((d_ff, d), bf16)],
        ),
        out_shape=jax.ShapeDtypeStruct((n_blocks * MOE_TILE, d), f32),
        compiler_params=_cparams("arbitrary"),
        name="moe_experts",
    )(block_e, n_used, row_start, xs, w_gate, b_gate.reshape(depth, n_e, 1, d_ff), w_up,
      b_up.reshape(depth, n_e, 1, d_ff), w_down, b_down.reshape(depth, n_e, 1, d))


def _expert_table(expert_ids, table):
    onehot = expert_ids[:, None] == jnp.arange(N_EXPERTS, dtype=jnp.int32)[None, :]
    return jnp.sum(jnp.where(onehot, table[None, :], 0), axis=1)


def _route(top_idx, rank, counts, n_blocks):
    n = top_idx.shape[0]
    n_assign = n * TOP_K
    i32 = jnp.int32
    flat_e = top_idx.reshape(-1)
    flat_rank = rank.reshape(-1)
    experts = jnp.arange(N_EXPERTS, dtype=i32)
    fill = SUBLANES - 1
    grp_size = (counts + fill) // SUBLANES * SUBLANES
    grp_start = jnp.cumsum(grp_size) - grp_size
    nb = (counts + MOE_TILE - 1) // MOE_TILE
    blk_end = jnp.cumsum(nb)
    blk_start = blk_end - nb
    n_sorted = n_assign + N_EXPERTS * fill
    j = jnp.arange(fill, dtype=i32)
    filler_pos = jnp.where(j[None, :] < (grp_size - counts)[:, None], (grp_start + counts)[:, None] + j[None, :],
                           n_sorted)
    pos = jnp.concatenate([_expert_table(flat_e, grp_start) + flat_rank, filler_pos.reshape(-1)])
    tok = jnp.concatenate([jnp.arange(n_assign, dtype=i32) // TOP_K, jnp.zeros((N_EXPERTS * fill,), i32)])
    assert n < TOKEN_KEY_RANGE and (n_sorted + 1) * TOKEN_KEY_RANGE <= 2 ** 32
    keyed = lax.sort(pos.astype(jnp.uint32) * TOKEN_KEY_RANGE + tok.astype(jnp.uint32))
    sorted_tok = (keyed % TOKEN_KEY_RANGE).astype(i32)
    src_tok = jnp.concatenate([sorted_tok, jnp.zeros((MOE_TILE,), i32)])
    dest = _expert_table(flat_e, blk_start * MOE_TILE) + flat_rank
    blk = jnp.arange(n_blocks, dtype=i32)
    block_e = jnp.minimum(jnp.sum((blk_end[None, :] <= blk[:, None]).astype(i32), axis=1), N_EXPERTS - 1)
    n_used = blk_end[-1:].astype(i32)
    row_start = jnp.where(blk < n_used[0], blk * MOE_TILE - _expert_table(block_e, blk_start * MOE_TILE - grp_start), 0)
    return src_tok, dest.reshape(n, TOP_K), block_e, n_used, row_start // SUBLANES


def _combine_kernel(final, x_ref, rows_ref, gw_ref, mod_ref, *rest):
    ffn = rows_ref[0] * gw_ref[:, 0:1]
    for k in range(1, TOP_K):
        ffn = ffn + rows_ref[k] * gw_ref[:, k:k + 1]
    x = x_ref[...] + mod_ref[5:6, :] * ffn
    if final:
        fg_ref, y_ref = rest
        y_ref[...] = x * lax.rsqrt(jnp.mean(x * x, axis=-1, keepdims=True) + EPS) * fg_ref[...]
    else:
        rest[0][...] = x


def _combine(geom, layer, x, rows, gate_w, mod, final_g):
    n = geom.n_tok
    final = final_g is not None
    tile = pl.BlockSpec((TOK_TILE, D_MODEL), lambda i: (i, 0))
    in_specs = [
        tile,
        pl.BlockSpec((TOP_K, TOK_TILE, D_MODEL), lambda i: (0, i, 0)),
        pl.BlockSpec((TOK_TILE, LANES), lambda i: (i, 0)),
        pl.BlockSpec((None, None, 6, D_MODEL), lambda i: (layer, geom.cond_row(i), 0, 0)),
    ]
    args = [x, rows, gate_w, mod]
    if final:
        in_specs.append(pl.BlockSpec(final_g.shape, lambda i: (0, 0)))
        args.append(final_g)
    return pl.pallas_call(
        functools.partial(_combine_kernel, final),
        grid=(geom.n_tiles,),
        in_specs=in_specs,
        out_specs=tile,
        out_shape=jax.ShapeDtypeStruct((n, D_MODEL), f32),
        compiler_params=_cparams("arbitrary"),
        name="moe_combine",
    )(*args)


def _pad_lanes(a, width=LANES):
    return jnp.pad(a, [(0, 0)] * (a.ndim - 1) + [(0, width - a.shape[-1])])


def kernel(x_prompt, x_sample, c, cache_k, cache_v, state_ssm, c_ctx, norm1_g, norm2_g, w_ada, b_ada, w_in, w_out, attn_lambda, attn_subln_g, conv_dw_w, conv_dw_b, conv_ln_g, conv_ln_b, ssm_conv_w, ssm_conv_b, ssm_dt_bias, ssm_a_log, ssm_d, ssm_norm_g, router_w, router_b, w_gate, b_gate, w_up, b_up, w_down, b_down, final_g):
    n_ctx_seq, ctx_len, d = x_prompt.shape
    n_lat_seq, lat_len, _ = x_sample.shape
    past_len = cache_k.shape[-2]
    geom = _Geom(n_ctx_seq, ctx_len, n_lat_seq, lat_len)
    n = geom.n_tok

    x = jnp.concatenate([x_prompt.reshape(geom.n_ctx, d), x_sample.reshape(geom.n_lat, d)], axis=0)
    cond = jnp.concatenate([c_ctx[None, :], c, jnp.zeros((COND_ROWS - 1 - n_lat_seq, d), f32)], axis=0)
    mod = _ada(cond, w_ada, b_ada).reshape(DEPTH, COND_ROWS, 6, d)
    tables = _rope_tables(geom)

    n_assign = n * TOP_K
    n_blocks = -(-n_assign // MOE_TILE) + N_EXPERTS

    new_k, new_v, new_s = [], [], []
    y_final = None
    for l in range(DEPTH):
        w_main = w_in[l, :, :N_MAIN].astype(bf16)
        w_dt = _pad_lanes(w_in[l, :, N_MAIN:]).astype(bf16)
        q, k, v, glu, z, xbc, dt = _in_proj(geom, l, x, mod, norm1_g[l][None, :], w_main, w_dt, tables)

        lam_init = 0.8 - 0.6 * math.exp(-0.3 * l)
        subln = jnp.tile(attn_subln_g[l], LANES // ATT_V_DIM)[None, :]
        k_cache = cache_k[:, l].transpose(0, 3, 1, 2, 4).reshape(n_lat_seq, past_len, QK_WIDTH)
        v_cache = cache_v[:, l].transpose(0, 2, 1, 3).reshape(n_lat_seq, past_len, ATT_WIDTH)
        k_all = jnp.concatenate([k[geom.n_ctx:].reshape(n_lat_seq, lat_len, QK_WIDTH), k_cache], axis=1).astype(bf16)
        v_all = jnp.concatenate([v[geom.n_ctx:].reshape(n_lat_seq, lat_len, ATT_WIDTH), v_cache], axis=1).astype(bf16)
        attn_ctx = _attention(lam_init, attn_lambda[l], subln, q, k, v, n_ctx_seq, ctx_len, ctx_len, 0, False)
        attn_lat = _attention(lam_init, attn_lambda[l], subln, q, k_all, v_all, n_lat_seq, lat_len,
                              lat_len + past_len, geom.n_ctx, True)

        conv = _cconv(geom, glu, conv_dw_w[l], conv_dw_b[l][None, :], conv_ln_g[l][None, :], conv_ln_b[l][None, :])

        dt_bias = _pad_lanes(ssm_dt_bias[l].reshape(1, 2 * SSM_HEADS))
        a_log = _pad_lanes(ssm_a_log[l].reshape(1, 2 * SSM_HEADS))
        d_skip = jnp.repeat(ssm_d[l], SSM_HEAD_DIM)[None, :]
        conv_w, conv_b = ssm_conv_w[l], ssm_conv_b[l][None, :]
        y_f, fin_f, xbc_act = _ssd(geom, 0, xbc, dt, _pad_states(state_ssm[:, l, 0]), conv_w, conv_b, dt_bias, a_log,
                                   d_skip)
        y_b, fin_b = _ssd(geom, 1, xbc_act, dt, _pad_states(state_ssm[:, l, 1]), conv_w, conv_b, dt_bias, a_log, d_skip)
        ys = [y_f, y_b]
        fins = [_unpad_states(fin_f[:n_ctx_seq]), _unpad_states(fin_b[:n_ctx_seq])]

        x, h2, routed, gate_w, totals = _out_proj(
            geom, l, attn_ctx, attn_lat, conv, ys[0], ys[1], z, x, mod, ssm_norm_g[l][None, :], w_out[l].astype(bf16),
            norm2_g[l][None, :], _pad_lanes(router_w[l]), _pad_lanes(router_b[l][None, :]))

        src_tok, dest, block_e, n_used, row_start = _route(
            routed[:, :TOP_K], routed[:, TOP_K:2 * TOP_K], totals[0, :N_EXPERTS].astype(jnp.int32), n_blocks)
        xs = jnp.take(h2, src_tok, axis=0, mode="clip")
        eo = _moe(l, xs, block_e, n_used, row_start, w_gate, b_gate, w_up, b_up, w_down, b_down)
        rows = jnp.take(eo, dest.T.reshape(-1), axis=0, mode="clip").reshape(TOP_K, n, d)
        if l == DEPTH - 1:
            y_final = _combine(geom, l, x, rows, gate_w, mod, final_g[None, :])
        else:
            x = _combine(geom, l, x, rows, gate_w, mod, None)

        new_k.append(k[:geom.n_ctx].reshape(n_ctx_seq, ctx_len, ATT_HEADS, 2, ATT_QK_DIM).transpose(0, 2, 3, 1, 4))
        new_v.append(v[:geom.n_ctx].reshape(n_ctx_seq, ctx_len, ATT_HEADS, ATT_V_DIM).transpose(0, 2, 1, 3))
        new_s.append(jnp.stack(fins, axis=1))

    y_prompt = y_final[:geom.n_ctx].reshape(n_ctx_seq, ctx_len, d)
    y_sample = y_final[geom.n_ctx:].reshape(n_lat_seq, lat_len, d)
    return (y_prompt, y_sample, jnp.stack(new_k, axis=1), jnp.stack(new_v, axis=1), jnp.stack(new_s, axis=1))
```

```python
import functools
import math

import numpy as np
import jax
import jax.numpy as jnp
from jax import lax
from jax.experimental import pallas as pl
from jax.experimental.pallas import tpu as pltpu

f32 = jnp.float32
bf16 = jnp.bfloat16
HIGHEST = lax.Precision.HIGHEST

D_MODEL = 1024
DEPTH = 2
GRID_W = 64
ATT_HEADS = 4
ATT_QK_DIM = 32
ATT_V_DIM = 64
ATT_WIDTH = 256
QK_WIDTH = 256
ROPE_AXIS_DIM = 16
ROPE_BASE = 10000.0
CONV_CH = 256
CONV_K = 31
SSM_WIDTH = 512
SSM_HEAD_DIM = 64
SSM_HEADS = 8
SSM_GROUPS = 2
SSM_STATE = 64
SSM_CONV_K = 5
SSM_XBC = 768
N_MAIN = 2560
N_EXPERTS = 32
TOP_K = 4
SWIGLU_LIMIT = 7.0
SWIGLU_ALPHA = 1.702
EPS = 1e-6

LANES = 128
SUBLANES = 8
TOK_TILE = 256
SSD_CHUNK = 128
ATT_Q_TILE = 512
CONV_HALO = 16
SSM_HALO = 8
MOE_TILE = 512
MOE_FF_CHUNK = 256
TOKEN_KEY_RANGE = 2 ** 15
COND_ROWS = 16
VMEM_LIMIT = 56 * 1024 * 1024


def _cparams(*sem):
    return pltpu.CompilerParams(dimension_semantics=sem, vmem_limit_bytes=VMEM_LIMIT)


def _silu(x):
    return x * jax.nn.sigmoid(x)


def _ada_kernel(cond_ref, w_ref, b_ref, o_ref):
    s = _silu(cond_ref[...]).astype(bf16)
    o_ref[0] = jnp.dot(s, w_ref[0].astype(bf16), preferred_element_type=f32) + b_ref[0]


def _ada(cond, w_ada, b_ada):
    depth, d, n6 = w_ada.shape
    tn = 1536
    return pl.pallas_call(
        _ada_kernel,
        grid=(depth, n6 // tn),
        in_specs=[
            pl.BlockSpec((COND_ROWS, d), lambda l, j: (0, 0)),
            pl.BlockSpec((1, d, tn), lambda l, j: (l, 0, j)),
            pl.BlockSpec((1, 1, tn), lambda l, j: (l, 0, j)),
        ],
        out_specs=pl.BlockSpec((1, COND_ROWS, tn), lambda l, j: (l, 0, j)),
        out_shape=jax.ShapeDtypeStruct((depth, COND_ROWS, n6), f32),
        compiler_params=_cparams("arbitrary", "arbitrary"),
        name="ada",
    )(cond, w_ada, b_ada.reshape(depth, 1, n6))


class _Geom:
    def __init__(self, n_ctx_seq, ctx_len, n_lat_seq, lat_len):
        self.n_ctx_seq, self.ctx_len, self.n_lat_seq, self.lat_len = n_ctx_seq, ctx_len, n_lat_seq, lat_len
        self.n_ctx = n_ctx_seq * ctx_len
        self.n_lat = n_lat_seq * lat_len
        self.n_tok = self.n_ctx + self.n_lat
        assert ctx_len % TOK_TILE == 0 and lat_len % TOK_TILE == 0
        self.ctx_tiles = self.n_ctx // TOK_TILE
        self.tiles_per_ctx = ctx_len // TOK_TILE
        self.tiles_per_lat = lat_len // TOK_TILE
        self.n_tiles = self.n_tok // TOK_TILE

    def cond_row(self, i):
        return jnp.where(i < self.ctx_tiles, 0, 1 + (i - self.ctx_tiles) // self.tiles_per_lat)

    def seq_pos(self, i):
        j_ctx = i % self.tiles_per_ctx
        j_lat = (i - self.ctx_tiles) % self.tiles_per_lat
        is_ctx = i < self.ctx_tiles
        first = jnp.where(is_ctx, j_ctx == 0, j_lat == 0)
        last = jnp.where(is_ctx, j_ctx == self.tiles_per_ctx - 1, j_lat == self.tiles_per_lat - 1)
        return first, last

    def seq_index(self, i):
        return jnp.where(i < self.ctx_tiles, i // self.tiles_per_ctx,
                         self.n_ctx_seq + (i - self.ctx_tiles) // self.tiles_per_lat)


def _rope_tables(geom):
    t = np.arange(geom.lat_len)
    row = (t // GRID_W).astype(np.float64)
    col = (t % GRID_W).astype(np.float64)
    inv_freq = 1.0 / (ROPE_BASE ** (np.arange(0, ROPE_AXIS_DIM, 2, dtype=np.float64) / ROPE_AXIS_DIM))
    ang_r = row[:, None] * inv_freq[None, :]
    ang_c = col[:, None] * inv_freq[None, :]
    ang = np.concatenate([ang_r, ang_r, ang_c, ang_c], axis=-1)
    cos, sin = np.cos(ang), np.sin(ang)
    half = ROPE_AXIS_DIM // 2
    lane = np.arange(ATT_QK_DIM)
    first_half = (lane % ROPE_AXIS_DIM) < half
    sin_a = np.where(first_half[None, :], -sin, 0.0)
    sin_b = np.where(first_half[None, :], 0.0, sin)
    rep = LANES // ATT_QK_DIM

    def full(tab, ident):
        tab = np.tile(tab, (1, rep))
        return jnp.asarray(np.concatenate([np.full((TOK_TILE, LANES), ident), tab], axis=0), f32)

    return full(cos, 1.0), full(sin_a, 0.0), full(sin_b, 0.0)


def _in_proj_kernel(x_ref, mod_ref, g_ref, w_ref, wdt_ref, cos_ref, sa_ref, sb_ref,
                    q_ref, k_ref, v_ref, glu_ref, z_ref, xbc_ref, dt_ref):
    x = x_ref[...]
    y = x * lax.rsqrt(jnp.mean(x * x, axis=-1, keepdims=True) + EPS) * g_ref[...]
    h = (y * (1.0 + mod_ref[1:2, :]) + mod_ref[0:1, :]).astype(bf16)
    p = jnp.dot(h, w_ref[...], preferred_element_type=f32)
    dt_ref[...] = jnp.dot(h, wdt_ref[...], preferred_element_type=f32)
    cos, sa, sb = cos_ref[...], sa_ref[...], sb_ref[...]
    half = ROPE_AXIS_DIM // 2

    def rope(u):
        return u * cos + pltpu.roll(u, LANES - half, 1) * sa + pltpu.roll(u, half, 1) * sb

    scale = ATT_QK_DIM ** -0.5 * math.log2(math.e)
    for j in range(QK_WIDTH // LANES):
        lo = j * LANES
        q_ref[:, lo:lo + LANES] = (rope(p[:, lo:lo + LANES]) * scale).astype(bf16)
        k_ref[:, lo:lo + LANES] = rope(p[:, QK_WIDTH + lo:QK_WIDTH + lo + LANES])
    o = 2 * QK_WIDTH
    v_ref[...] = p[:, o:o + ATT_WIDTH]
    o += ATT_WIDTH
    glu_ref[...] = p[:, o:o + 2 * CONV_CH]
    o += 2 * CONV_CH
    z_ref[...] = p[:, o:o + SSM_WIDTH]
    o += SSM_WIDTH
    xbc_ref[...] = p[:, o:o + SSM_XBC]


def _in_proj(geom, layer, x, mod, norm_g, w_main, w_dt, tables):
    n = geom.n_tok
    tile = lambda w: pl.BlockSpec((TOK_TILE, w), lambda i: (i, 0))
    const = lambda a: pl.BlockSpec(a.shape, lambda i: (0,) * a.ndim)
    tab_spec = pl.BlockSpec(
        (TOK_TILE, LANES),
        lambda i: (jnp.where(i < geom.ctx_tiles, 0, 1 + (i - geom.ctx_tiles) % geom.tiles_per_lat), 0))
    widths = (QK_WIDTH, QK_WIDTH, ATT_WIDTH, 2 * CONV_CH, SSM_WIDTH, SSM_XBC, LANES)
    dtypes = (bf16, f32, f32, f32, f32, f32, f32)
    return pl.pallas_call(
        _in_proj_kernel,
        grid=(geom.n_tiles,),
        in_specs=[
            tile(D_MODEL),
            pl.BlockSpec((None, None, 6, D_MODEL), lambda i: (layer, geom.cond_row(i), 0, 0)),
            const(norm_g), const(w_main), const(w_dt), tab_spec, tab_spec, tab_spec,
        ],
        out_specs=[tile(w) for w in widths],
        out_shape=[jax.ShapeDtypeStruct((n, w), dt) for w, dt in zip(widths, dtypes)],
        compiler_params=_cparams("arbitrary"),
        name="in_proj",
    )(x, mod, norm_g, w_main, w_dt, *tables)


def _attn_kernel(lam_init, al_ref, g_ref, q_ref, k_ref, v_ref, o_ref):
    al = al_ref[...]
    lam = (jnp.exp(jnp.sum(al[0:1] * al[1:2], axis=-1, keepdims=True))
           - jnp.exp(jnp.sum(al[2:3] * al[3:4], axis=-1, keepdims=True)) + lam_init)
    lane = lax.broadcasted_iota(jnp.int32, (1, LANES), 1)
    heads_per_slab = LANES // ATT_V_DIM
    for j in range(ATT_WIDTH // LANES):
        sl = slice(j * LANES, (j + 1) * LANES)
        qs = q_ref[:, sl]
        ks = k_ref[:, sl].astype(bf16)
        vs = v_ref[:, sl].astype(bf16)
        a_heads = []
        for hl in range(heads_per_slab):
            own = (lane >= hl * ATT_V_DIM) & (lane < (hl + 1) * ATT_V_DIM)
            vh = jnp.where(own, vs, jnp.ones_like(vs))
            sum_lane = (1 - hl) * ATT_V_DIM
            comps = []
            for c in range(2):
                lo = hl * ATT_V_DIM + c * ATT_QK_DIM
                qm = jnp.where((lane >= lo) & (lane < lo + ATT_QK_DIM), qs, jnp.zeros_like(qs))
                s = lax.dot_general(qm, ks, (((1,), (1,)), ((), ())), preferred_element_type=f32)
                e = jnp.exp2(s - jnp.max(s, axis=-1, keepdims=True))
                pv = jnp.dot(e.astype(bf16), vh, preferred_element_type=f32)
                comps.append(pv / pv[:, sum_lane:sum_lane + 1])
            a_heads.append(comps[0] - lam * comps[1])
        a = jnp.where(lane < ATT_V_DIM, a_heads[0], a_heads[1])
        a2 = a * a
        ms0 = jnp.sum(jnp.where(lane < ATT_V_DIM, a2, 0.0), axis=-1, keepdims=True)
        ms1 = jnp.sum(jnp.where(lane < ATT_V_DIM, 0.0, a2), axis=-1, keepdims=True)
        ms = jnp.where(lane < ATT_V_DIM, ms0, ms1) * (1.0 / ATT_V_DIM)
        o_ref[:, sl] = a * lax.rsqrt(ms + EPS) * g_ref[...] * (1.0 - lam_init)


def _attention(lam_init, attn_lambda, subln_g, q, k, v, n_seq, q_len, k_len, q_row0, kv_is_3d):
    tq = min(ATT_Q_TILE, q_len)
    qb = q_len // tq
    q0 = q_row0 // tq
    if kv_is_3d:
        kv_spec = pl.BlockSpec((None, k_len, ATT_WIDTH), lambda b, i: (b, 0, 0))
    else:
        kv_spec = pl.BlockSpec((k_len, ATT_WIDTH), lambda b, i: (q_row0 // k_len + b, 0))
    return pl.pallas_call(
        functools.partial(_attn_kernel, lam_init),
        grid=(n_seq, qb),
        in_specs=[
            pl.BlockSpec(attn_lambda.shape, lambda b, i: (0, 0)),
            pl.BlockSpec(subln_g.shape, lambda b, i: (0, 0)),
            pl.BlockSpec((tq, ATT_WIDTH), lambda b, i: (q0 + b * qb + i, 0)),
            kv_spec, kv_spec,
        ],
        out_specs=pl.BlockSpec((tq, ATT_WIDTH), lambda b, i: (b * qb + i, 0)),
        out_shape=jax.ShapeDtypeStruct((n_seq * q_len, ATT_WIDTH), f32),
        compiler_params=_cparams("arbitrary", "arbitrary"),
        name="attention",
    )(attn_lambda, subln_g, q, k, v)


def _halo_specs(geom, width, halo):
    per = TOK_TILE // halo
    n_halo = geom.n_tok // halo
    prev = pl.BlockSpec((halo, width), lambda i: (jnp.maximum(i * per - 1, 0), 0))
    nxt = pl.BlockSpec((halo, width), lambda i: (jnp.minimum((i + 1) * per, n_halo - 1), 0))
    return prev, nxt


def _cconv_kernel(geom, cur_ref, prev_ref, next_ref, w_ref, b_ref, lg_ref, lb_ref, o_ref, pad_ref, shift_ref):
    first, last = geom.seq_pos(pl.program_id(0))

    def glu(ref):
        return ref[:, :CONV_CH] * jax.nn.sigmoid(ref[:, CONV_CH:])

    pad_ref[0:CONV_HALO, :] = jnp.where(first, 0.0, glu(prev_ref))
    pad_ref[CONV_HALO:CONV_HALO + TOK_TILE, :] = glu(cur_ref)
    pad_ref[CONV_HALO + TOK_TILE:, :] = jnp.where(last, 0.0, glu(next_ref))
    n_shifted = TOK_TILE + 2 * CONV_HALO - SUBLANES
    for j in range(1, SUBLANES):
        shift_ref[j - 1, 0:n_shifted, :] = pad_ref[j:j + n_shifted, :]
    rows = 64
    base = CONV_HALO - CONV_K // 2
    for r0 in range(0, TOK_TILE, rows):
        acc = jnp.broadcast_to(b_ref[...], (rows, CONV_CH))
        for k in range(CONV_K):
            q, j = divmod(r0 + base + k, SUBLANES)
            lo = q * SUBLANES
            window = pad_ref[lo:lo + rows, :] if j == 0 else shift_ref[j - 1, lo:lo + rows, :]
            acc = acc + w_ref[k:k + 1, :] * window
        xc = acc - jnp.mean(acc, axis=-1, keepdims=True)
        var = jnp.mean(xc * xc, axis=-1, keepdims=True)
        o_ref[r0:r0 + rows, :] = _silu(xc * lax.rsqrt(var + EPS) * lg_ref[...] + lb_ref[...])


def _cconv(geom, glu, w, b, ln_g, ln_b):
    prev, nxt = _halo_specs(geom, 2 * CONV_CH, CONV_HALO)
    const = lambda a: pl.BlockSpec(a.shape, lambda i: (0,) * a.ndim)
    return pl.pallas_call(
        functools.partial(_cconv_kernel, geom),
        grid=(geom.n_tiles,),
        in_specs=[pl.BlockSpec((TOK_TILE, 2 * CONV_CH), lambda i: (i, 0)), prev, nxt,
                  const(w), const(b), const(ln_g), const(ln_b)],
        out_specs=pl.BlockSpec((TOK_TILE, CONV_CH), lambda i: (i, 0)),
        out_shape=jax.ShapeDtypeStruct((geom.n_tok, CONV_CH), f32),
        scratch_shapes=[pltpu.VMEM((TOK_TILE + 2 * CONV_HALO, CONV_CH), f32),
                        pltpu.VMEM((SUBLANES - 1, TOK_TILE + 2 * CONV_HALO, CONV_CH), f32)],
        compiler_params=_cparams("arbitrary"),
        name="conformer_conv",
    )(glu, glu, glu, w, b, ln_g, ln_b)


def _softplus(x):
    return jnp.maximum(x, 0.0) + jnp.log1p(jnp.exp(-jnp.abs(x)))


def _ssd_kernel(geom, direction, *refs):
    fwd = direction == 0
    if fwd:
        (cur_ref, prev_ref, next_ref, dt_ref, init_ref, cw_ref, cb_ref, dtb_ref, alog_ref, dskip_ref,
         y_ref, fin_ref, xc_ref, pad_ref, st_ref) = refs
    else:
        xc_ref, dt_ref, init_ref, dtb_ref, alog_ref, dskip_ref, y_ref, fin_ref, st_ref = refs
    i = pl.program_id(0)
    t = i if fwd else geom.n_tiles - 1 - i
    first, last = geom.seq_pos(t)
    enter, leave = (first, last) if fwd else (last, first)
    is_lat = t >= geom.ctx_tiles

    @pl.when(enter)
    def _():
        st_ref[...] = jnp.where(is_lat, init_ref[...], 0.0)

    if fwd:
        pad_ref[0:SSM_HALO, :] = jnp.where(first, 0.0, prev_ref[...])
        pad_ref[SSM_HALO:SSM_HALO + TOK_TILE, :] = cur_ref[...]
        pad_ref[SSM_HALO + TOK_TILE:, :] = jnp.where(last, 0.0, next_ref[...])
        base = SSM_HALO - SSM_CONV_K // 2
        acc = jnp.broadcast_to(cb_ref[...], (TOK_TILE, SSM_XBC))
        for k in range(SSM_CONV_K):
            acc = acc + cw_ref[k:k + 1, :] * pad_ref[base + k:base + k + TOK_TILE, :]
        xc_ref[...] = _silu(acc)

    L = SSD_CHUNK
    a_row = -jnp.exp(alog_ref[...])
    ri = lax.broadcasted_iota(jnp.int32, (L, L), 0)
    ci = lax.broadcasted_iota(jnp.int32, (L, L), 1)
    keep = (ci <= ri) if fwd else (ci >= ri)
    tri_col = keep.astype(f32)
    tri_row = ((ri <= ci) if fwd else (ri >= ci)).astype(f32)
    lane = lax.broadcasted_iota(jnp.int32, (1, LANES), 1)
    er = lax.broadcasted_iota(jnp.int32, (LANES, SSM_WIDTH), 0)
    ec = lax.broadcasted_iota(jnp.int32, (LANES, SSM_WIDTH), 1)
    expand = (er == direction * SSM_HEADS + ec // SSM_HEAD_DIM).astype(f32)
    heads_per_group = SSM_HEADS // SSM_GROUPS
    heads_per_slab = LANES // SSM_HEAD_DIM
    tot_at = L - 1 if fwd else 0

    chunks = range(TOK_TILE // L)
    for cidx in (chunks if fwd else reversed(chunks)):
        r0 = cidx * L
        xs = xc_ref[r0:r0 + L, 0:SSM_WIDTH]
        bm = xc_ref[r0:r0 + L, SSM_WIDTH:SSM_WIDTH + LANES]
        cm = xc_ref[r0:r0 + L, SSM_WIDTH + LANES:SSM_XBC]
        dt = _softplus(dt_ref[r0:r0 + L, :] + dtb_ref[...])
        da = dt * a_row
        acum_col = jnp.dot(tri_col, da, precision=HIGHEST, preferred_element_type=f32)
        acum_row = jnp.dot(da.T, tri_row, precision=HIGHEST, preferred_element_type=f32)
        tot_row = acum_col[tot_at:tot_at + 1, :]
        tot_col = acum_row[:, tot_at:tot_at + 1]
        decay_out = jnp.exp(tot_col - acum_row)
        decay_in = jnp.exp(acum_col)
        chunk_decay = jnp.exp(tot_row)
        xdt = (xs * jnp.dot(dt, expand, precision=HIGHEST, preferred_element_type=f32)).astype(bf16)
        bt = bm.T
        bb = bm.astype(bf16)
        ys = []
        for g in range(SSM_GROUPS):
            in_group = (lane >= g * SSM_STATE) & (lane < (g + 1) * SSM_STATE)
            cg = jnp.where(in_group, cm, 0.0)
            gmat = lax.dot_general(cg.astype(bf16), bb, (((1,), (1,)), ((), ())), preferred_element_type=f32)
            for hh in range(heads_per_group):
                h = g * heads_per_group + hh
                hl = direction * SSM_HEADS + h
                slab = xdt[:, (h // heads_per_slab) * LANES:(h // heads_per_slab + 1) * LANES]
                seg = acum_col[:, hl:hl + 1] - acum_row[hl:hl + 1, :]
                m = (gmat * jnp.exp(jnp.where(keep, seg, -jnp.inf))).astype(bf16)
                y_h = jnp.dot(m, slab, preferred_element_type=f32)
                c_in = (cg * decay_in[:, hl:hl + 1]).astype(bf16)
                state = st_ref[h]
                y_h = y_h + jnp.dot(c_in, state.astype(bf16), preferred_element_type=f32)
                b_out = (bt * decay_out[hl:hl + 1, :]).astype(bf16)
                st_ref[h] = chunk_decay[:, hl:hl + 1] * state + jnp.dot(b_out, slab, preferred_element_type=f32)
                ys.append(y_h)
        for j in range(SSM_WIDTH // LANES):
            y_slab = jnp.where(lane < SSM_HEAD_DIM, ys[heads_per_slab * j], ys[heads_per_slab * j + 1])
            if fwd:
                y_slab = y_slab + dskip_ref[:, j * LANES:(j + 1) * LANES] * xs[:, j * LANES:(j + 1) * LANES]
            y_ref[r0:r0 + L, j * LANES:(j + 1) * LANES] = y_slab

    @pl.when(leave)
    def _():
        fin_ref[...] = st_ref[...]


def _ssd(geom, direction, xbc, dt, init, conv_w, conv_b, dt_bias, a_log, d_skip):
    nt = geom.n_tiles
    fwd = direction == 0
    tidx = (lambda i: i) if fwd else (lambda i: nt - 1 - i)
    per = TOK_TILE // SSM_HALO
    n_halo = geom.n_tok // SSM_HALO
    n_seq = geom.n_ctx_seq + geom.n_lat_seq
    const = lambda a: pl.BlockSpec(a.shape, lambda i: (0,) * a.ndim)
    st_block = (None, SSM_HEADS, LANES, LANES)
    tile = lambda w: pl.BlockSpec((TOK_TILE, w), lambda i: (tidx(i), 0))
    init_spec = pl.BlockSpec(st_block, lambda i: (jnp.maximum(geom.seq_index(tidx(i)) - geom.n_ctx_seq, 0), 0, 0, 0))
    out_specs = [tile(SSM_WIDTH), pl.BlockSpec(st_block, lambda i: (geom.seq_index(tidx(i)), 0, 0, 0))]
    out_shape = [jax.ShapeDtypeStruct((geom.n_tok, SSM_WIDTH), f32),
                 jax.ShapeDtypeStruct((n_seq, SSM_HEADS, LANES, LANES), f32)]
    state_scratch = pltpu.VMEM((SSM_HEADS, LANES, LANES), f32)
    if fwd:
        in_specs = [
            tile(SSM_XBC),
            pl.BlockSpec((SSM_HALO, SSM_XBC), lambda i: (jnp.maximum(i * per - 1, 0), 0)),
            pl.BlockSpec((SSM_HALO, SSM_XBC), lambda i: (jnp.minimum((i + 1) * per, n_halo - 1), 0)),
            tile(LANES), init_spec, const(conv_w), const(conv_b), const(dt_bias), const(a_log), const(d_skip),
        ]
        args = (xbc, xbc, xbc, dt, init, conv_w, conv_b, dt_bias, a_log, d_skip)
        out_specs.append(tile(SSM_XBC))
        out_shape.append(jax.ShapeDtypeStruct((geom.n_tok, SSM_XBC), f32))
        scratch = [pltpu.VMEM((TOK_TILE + 2 * SSM_HALO, SSM_XBC), f32), state_scratch]
    else:
        in_specs = [tile(SSM_XBC), tile(LANES), init_spec, const(dt_bias), const(a_log), const(d_skip)]
        args = (xbc, dt, init, dt_bias, a_log, d_skip)
        scratch = [state_scratch]
    return pl.pallas_call(
        functools.partial(_ssd_kernel, geom, direction),
        grid=(nt,),
        in_specs=in_specs,
        out_specs=out_specs,
        out_shape=out_shape,
        scratch_shapes=scratch,
        compiler_params=_cparams("arbitrary"),
        name="ssd_scan",
    )(*args)


def _pad_states(s):
    return jnp.tile(jnp.swapaxes(s, -1, -2), (1, 1, LANES // SSM_STATE, LANES // SSM_HEAD_DIM))


def _unpad_states(fin):
    hpg = SSM_HEADS // SSM_GROUPS
    outs = []
    for h in range(SSM_HEADS):
        g, side = h // hpg, h % (LANES // SSM_HEAD_DIM)
        q = fin[:, h, g * SSM_STATE:(g + 1) * SSM_STATE, side * SSM_HEAD_DIM:(side + 1) * SSM_HEAD_DIM]
        outs.append(jnp.swapaxes(q, -1, -2))
    return jnp.stack(outs, axis=1)


def _out_proj_kernel(geom, actx_ref, alat_ref, conv_ref, yf_ref, yb_ref, z_ref, x_ref, mod_ref, sg_ref, w_ref,
                     g2_ref, rwh_ref, rwl_ref, rb_ref, xo_ref, h2_ref, ti_ref, gw_ref, cnt_ref):
    attn = jnp.where(pl.program_id(0) < geom.ctx_tiles, actx_ref[...], alat_ref[...])
    gated = (yf_ref[...] + yb_ref[...]) * _silu(z_ref[...])
    ssm = gated * lax.rsqrt(jnp.mean(gated * gated, axis=-1, keepdims=True) + EPS) * sg_ref[...]
    o1 = ATT_WIDTH
    o2 = ATT_WIDTH + CONV_CH
    mix = (jnp.dot(attn.astype(bf16), w_ref[0:o1, :], preferred_element_type=f32)
           + jnp.dot(conv_ref[...].astype(bf16), w_ref[o1:o2, :], preferred_element_type=f32)
           + jnp.dot(ssm.astype(bf16), w_ref[o2:, :], preferred_element_type=f32))
    x = x_ref[...] + mod_ref[2:3, :] * mix
    xo_ref[...] = x
    y = x * lax.rsqrt(jnp.mean(x * x, axis=-1, keepdims=True) + EPS) * g2_ref[...]
    h2 = y * (1.0 + mod_ref[4:5, :]) + mod_ref[3:4, :]
    h2_ref[...] = h2
    h_hi = h2.astype(bf16)
    h_lo = (h2 - h_hi.astype(f32)).astype(bf16)
    logits = (jnp.dot(h_hi, rwh_ref[...], preferred_element_type=f32)
              + jnp.dot(h_lo, rwh_ref[...], preferred_element_type=f32)
              + jnp.dot(h_hi, rwl_ref[...], preferred_element_type=f32)) + rb_ref[...]
    lane = lax.broadcasted_iota(jnp.int32, (1, LANES), 1)
    lane_f = lane.astype(f32)
    lg = jnp.where(lane < N_EXPERTS, logits, -jnp.inf)
    idx_out = jnp.zeros(logits.shape, jnp.int32)
    gate_out = jnp.zeros(logits.shape, f32)
    top0 = None
    denom = None
    picks = []
    for k in range(TOP_K):
        m = jnp.max(lg, axis=-1, keepdims=True)
        idx = jnp.min(jnp.where(lg == m, lane_f, float(LANES)), axis=-1, keepdims=True).astype(jnp.int32)
        picked = lane == idx
        picks.append(picked)
        lg = jnp.where(picked, -jnp.inf, lg)
        if k == 0:
            top0 = m
        e = jnp.exp(m - top0)
        denom = e if k == 0 else denom + e
        idx_out = jnp.where(lane == k, idx, idx_out)
        gate_out = jnp.where(lane == k, e, gate_out)
    gw_ref[...] = gate_out / denom

    @pl.when(pl.program_id(0) == 0)
    def _():
        cnt_ref[...] = jnp.zeros_like(cnt_ref)

    chosen = picks[0] | picks[1] | picks[2] | picks[3]
    sel = jnp.where(chosen, 1.0, 0.0)
    ri = lax.broadcasted_iota(jnp.int32, (TOK_TILE, TOK_TILE), 0)
    ci = lax.broadcasted_iota(jnp.int32, (TOK_TILE, TOK_TILE), 1)
    before = jnp.where(ci < ri, 1.0, 0.0).astype(bf16)
    prior = jnp.dot(before, sel.astype(bf16), preferred_element_type=f32) + cnt_ref[...]
    for k in range(TOP_K):
        rank = jnp.sum(jnp.where(picks[k], prior, 0.0), axis=-1, keepdims=True).astype(jnp.int32)
        idx_out = jnp.where(lane == TOP_K + k, rank, idx_out)
    ti_ref[...] = idx_out.astype(f32).T[0:2 * TOP_K, :].astype(jnp.int32)
    cnt_ref[...] = cnt_ref[...] + jnp.sum(sel, axis=0, keepdims=True)


def _out_proj(geom, layer, attn_ctx, attn_lat, conv, y_f, y_b, z, x, mod, ssm_norm_g, w_out, norm2_g, router_w,
              router_b):
    n = geom.n_tok
    tile = lambda w: pl.BlockSpec((TOK_TILE, w), lambda i: (i, 0))
    const = lambda a: pl.BlockSpec(a.shape, lambda i: (0,) * a.ndim)
    nct = geom.ctx_tiles
    rw_hi = router_w.astype(bf16)
    rw_lo = (router_w - rw_hi.astype(f32)).astype(bf16)
    return pl.pallas_call(
        functools.partial(_out_proj_kernel, geom),
        grid=(geom.n_tiles,),
        in_specs=[
            pl.BlockSpec((TOK_TILE, ATT_WIDTH), lambda i: (jnp.minimum(i, nct - 1), 0)),
            pl.BlockSpec((TOK_TILE, ATT_WIDTH), lambda i: (jnp.maximum(i - nct, 0), 0)),
            tile(CONV_CH), tile(SSM_WIDTH), tile(SSM_WIDTH), tile(SSM_WIDTH), tile(D_MODEL),
            pl.BlockSpec((None, None, 6, D_MODEL), lambda i: (layer, geom.cond_row(i), 0, 0)),
            const(ssm_norm_g), const(w_out), const(norm2_g), const(rw_hi), const(rw_lo), const(router_b),
        ],
        out_specs=[tile(D_MODEL), tile(D_MODEL), pl.BlockSpec((2 * TOP_K, TOK_TILE), lambda i: (0, i)), tile(LANES),
                   pl.BlockSpec((1, LANES), lambda i: (0, 0))],
        out_shape=[jax.ShapeDtypeStruct((n, D_MODEL), f32), jax.ShapeDtypeStruct((n, D_MODEL), f32),
                   jax.ShapeDtypeStruct((2 * TOP_K, n), jnp.int32), jax.ShapeDtypeStruct((n, LANES), f32),
                   jax.ShapeDtypeStruct((1, LANES), f32)],
        compiler_params=_cparams("arbitrary"),
        name="out_proj",
    )(attn_ctx, attn_lat, conv, y_f, y_b, z, x, mod, ssm_norm_g, w_out, norm2_g, rw_hi, rw_lo, router_b)


def _moe_kernel(be_ref, nused_ref, rs_ref, x_ref, wg_ref, bg_ref, wu_ref, bu_ref, wd_ref, bd_ref, o_ref,
                wg_s, wu_s, wd_s):
    i = pl.program_id(0)
    prev_e = be_ref[jnp.maximum(i - 1, 0)]
    new_expert = (i == 0) | (be_ref[i] != prev_e)

    @pl.when(new_expert)
    def _():
        wg_s[...] = wg_ref[0].astype(bf16)
        wu_s[...] = wu_ref[0].astype(bf16)
        wd_s[...] = wd_ref[0].astype(bf16)

    @pl.when(i < nused_ref[0])
    def _():
        x = x_ref[...].astype(bf16)
        acc = None
        for c0 in range(0, wg_s.shape[1], MOE_FF_CHUNK):
            sl = slice(c0, c0 + MOE_FF_CHUNK)
            g = jnp.dot(x, wg_s[:, sl], preferred_element_type=f32) + bg_ref[0, :, sl]
            u = jnp.dot(x, wu_s[:, sl], preferred_element_type=f32) + bu_ref[0, :, sl]
            g = jnp.minimum(g, SWIGLU_LIMIT)
            u = jnp.clip(u, -SWIGLU_LIMIT, SWIGLU_LIMIT)
            act = g * jax.nn.sigmoid(SWIGLU_ALPHA * g) * (u + 1.0)
            part = jnp.dot(act.astype(bf16), wd_s[sl, :], preferred_element_type=f32)
            acc = part if acc is None else acc + part
        o_ref[...] = acc + bd_ref[0]

    @pl.when(i >= nused_ref[0])
    def _():
        o_ref[...] = jnp.zeros_like(o_ref)


def _moe(layer, xs, block_e, n_used, row_start, w_gate, b_gate, w_up, b_up, w_down, b_down):
    d = xs.shape[1]
    n_blocks = block_e.shape[0]
    d_ff = w_gate.shape[-1]
    w_spec = lambda r, c: pl.BlockSpec((None, 1, r, c), lambda i, be, nu, rs: (layer, be[i], 0, 0))
    b_spec = lambda c: pl.BlockSpec((None, 1, 1, c), lambda i, be, nu, rs: (layer, be[i], 0, 0))
    depth, n_e = b_gate.shape[:2]
    return pl.pallas_call(
        _moe_kernel,
        grid_spec=pltpu.PrefetchScalarGridSpec(
            num_scalar_prefetch=3,
            grid=(n_blocks,),
            in_specs=[pl.BlockSpec((pl.Element(MOE_TILE), pl.Element(d)),
                                   lambda i, be, nu, rs: (rs[i] * SUBLANES, 0)),
                      w_spec(d, d_ff), b_spec(d_ff), w_spec(d, d_ff), b_spec(d_ff), w_spec(d_ff, d), b_spec(d)],
            out_specs=pl.BlockSpec((MOE_TILE, d), lambda i, be, nu, rs: (i, 0)),
            scratch_shapes=[pltpu.VMEM((d, d_ff), bf16), pltpu.VMEM((d, d_ff), bf16), pltpu.VMEM((d_ff, d), bf16)],
        ),
        out_shape=jax.ShapeDtypeStruct((n_blocks * MOE_TILE, d), f32),
        compiler_params=_cparams("arbitrary"),
        name="moe_experts",
    )(block_e, n_used, row_start, xs, w_gate, b_gate.reshape(depth, n_e, 1, d_ff), w_up,
      b_up.reshape(depth, n_e, 1, d_ff), w_down, b_down.reshape(depth, n_e, 1, d))


def _expert_table(expert_ids, table):
    onehot = expert_ids[:, None] == jnp.arange(N_EXPERTS, dtype=jnp.int32)[None, :]
    return jnp.sum(jnp.where(onehot, table[None, :], 0), axis=1)


def _route(top_idx, rank, counts, n_blocks):
    n = top_idx.shape[1]
    n_assign = n * TOP_K
    i32 = jnp.int32
    flat_e = top_idx.reshape(-1)
    flat_rank = rank.reshape(-1)
    experts = jnp.arange(N_EXPERTS, dtype=i32)
    fill = SUBLANES - 1
    grp_size = (counts + fill) // SUBLANES * SUBLANES
    grp_start = jnp.cumsum(grp_size) - grp_size
    nb = (counts + MOE_TILE - 1) // MOE_TILE
    blk_end = jnp.cumsum(nb)
    blk_start = blk_end - nb
    n_sorted = n_assign + N_EXPERTS * fill
    j = jnp.arange(fill, dtype=i32)
    filler_pos = jnp.where(j[None, :] < (grp_size - counts)[:, None], (grp_start + counts)[:, None] + j[None, :],
                           n_sorted)
    pos = jnp.concatenate([_expert_table(flat_e, grp_start) + flat_rank, filler_pos.reshape(-1)])
    tok = jnp.concatenate([jnp.arange(n_assign, dtype=i32) % n, jnp.zeros((N_EXPERTS * fill,), i32)])
    assert n < TOKEN_KEY_RANGE and (n_sorted + 1) * TOKEN_KEY_RANGE <= 2 ** 32
    keyed = lax.sort(pos.astype(jnp.uint32) * TOKEN_KEY_RANGE + tok.astype(jnp.uint32))
    sorted_tok = (keyed % TOKEN_KEY_RANGE).astype(i32)
    src_tok = jnp.concatenate([sorted_tok, jnp.zeros((MOE_TILE,), i32)])
    dest = _expert_table(flat_e, blk_start * MOE_TILE) + flat_rank
    blk = jnp.arange(n_blocks, dtype=i32)
    block_e = jnp.minimum(jnp.sum((blk_end[None, :] <= blk[:, None]).astype(i32), axis=1), N_EXPERTS - 1)
    n_used = blk_end[-1:].astype(i32)
    row_start = jnp.where(blk < n_used[0], blk * MOE_TILE - _expert_table(block_e, blk_start * MOE_TILE - grp_start), 0)
    return src_tok, dest, block_e, n_used, row_start // SUBLANES


def _combine_kernel(final, x_ref, rows_ref, gw_ref, mod_ref, *rest):
    ffn = rows_ref[0] * gw_ref[:, 0:1]
    for k in range(1, TOP_K):
        ffn = ffn + rows_ref[k] * gw_ref[:, k:k + 1]
    x = x_ref[...] + mod_ref[5:6, :] * ffn
    if final:
        fg_ref, y_ref = rest
        y_ref[...] = x * lax.rsqrt(jnp.mean(x * x, axis=-1, keepdims=True) + EPS) * fg_ref[...]
    else:
        rest[0][...] = x


def _combine(geom, layer, x, rows, gate_w, mod, final_g):
    n = geom.n_tok
    final = final_g is not None
    tile = pl.BlockSpec((TOK_TILE, D_MODEL), lambda i: (i, 0))
    in_specs = [
        tile,
        pl.BlockSpec((TOP_K, TOK_TILE, D_MODEL), lambda i: (0, i, 0)),
        pl.BlockSpec((TOK_TILE, LANES), lambda i: (i, 0)),
        pl.BlockSpec((None, None, 6, D_MODEL), lambda i: (layer, geom.cond_row(i), 0, 0)),
    ]
    args = [x, rows, gate_w, mod]
    if final:
        in_specs.append(pl.BlockSpec(final_g.shape, lambda i: (0, 0)))
        args.append(final_g)
    return pl.pallas_call(
        functools.partial(_combine_kernel, final),
        grid=(geom.n_tiles,),
        in_specs=in_specs,
        out_specs=tile,
        out_shape=jax.ShapeDtypeStruct((n, D_MODEL), f32),
        compiler_params=_cparams("arbitrary"),
        name="moe_combine",
    )(*args)


def _pad_lanes(a, width=LANES):
    return jnp.pad(a, [(0, 0)] * (a.ndim - 1) + [(0, width - a.shape[-1])])


def kernel(x_prompt, x_sample, c, cache_k, cache_v, state_ssm, c_ctx, norm1_g, norm2_g, w_ada, b_ada, w_in, w_out, attn_lambda, attn_subln_g, conv_dw_w, conv_dw_b, conv_ln_g, conv_ln_b, ssm_conv_w, ssm_conv_b, ssm_dt_bias, ssm_a_log, ssm_d, ssm_norm_g, router_w, router_b, w_gate, b_gate, w_up, b_up, w_down, b_down, final_g):
    n_ctx_seq, ctx_len, d = x_prompt.shape
    n_lat_seq, lat_len, _ = x_sample.shape
    past_len = cache_k.shape[-2]
    geom = _Geom(n_ctx_seq, ctx_len, n_lat_seq, lat_len)
    n = geom.n_tok

    x = jnp.concatenate([x_prompt.reshape(geom.n_ctx, d), x_sample.reshape(geom.n_lat, d)], axis=0)
    cond = jnp.concatenate([c_ctx[None, :], c, jnp.zeros((COND_ROWS - 1 - n_lat_seq, d), f32)], axis=0)
    mod = _ada(cond, w_ada, b_ada).reshape(DEPTH, COND_ROWS, 6, d)
    tables = _rope_tables(geom)

    n_assign = n * TOP_K
    n_blocks = -(-n_assign // MOE_TILE) + N_EXPERTS

    new_k, new_v, new_s = [], [], []
    y_final = None
    for l in range(DEPTH):
        w_main = w_in[l, :, :N_MAIN].astype(bf16)
        w_dt = _pad_lanes(w_in[l, :, N_MAIN:]).astype(bf16)
        q, k, v, glu, z, xbc, dt = _in_proj(geom, l, x, mod, norm1_g[l][None, :], w_main, w_dt, tables)

        lam_init = 0.8 - 0.6 * math.exp(-0.3 * l)
        subln = jnp.tile(attn_subln_g[l], LANES // ATT_V_DIM)[None, :]
        k_cache = cache_k[:, l].transpose(0, 3, 1, 2, 4).reshape(n_lat_seq, past_len, QK_WIDTH)
        v_cache = cache_v[:, l].transpose(0, 2, 1, 3).reshape(n_lat_seq, past_len, ATT_WIDTH)
        k_all = jnp.concatenate([k[geom.n_ctx:].reshape(n_lat_seq, lat_len, QK_WIDTH), k_cache], axis=1).astype(bf16)
        v_all = jnp.concatenate([v[geom.n_ctx:].reshape(n_lat_seq, lat_len, ATT_WIDTH), v_cache], axis=1).astype(bf16)
        attn_ctx = _attention(lam_init, attn_lambda[l], subln, q, k, v, n_ctx_seq, ctx_len, ctx_len, 0, False)
        attn_lat = _attention(lam_init, attn_lambda[l], subln, q, k_all, v_all, n_lat_seq, lat_len,
                              lat_len + past_len, geom.n_ctx, True)

        conv = _cconv(geom, glu, conv_dw_w[l], conv_dw_b[l][None, :], conv_ln_g[l][None, :], conv_ln_b[l][None, :])

        dt_bias = _pad_lanes(ssm_dt_bias[l].reshape(1, 2 * SSM_HEADS))
        a_log = _pad_lanes(ssm_a_log[l].reshape(1, 2 * SSM_HEADS))
        d_skip = jnp.repeat(ssm_d[l], SSM_HEAD_DIM)[None, :]
        conv_w, conv_b = ssm_conv_w[l], ssm_conv_b[l][None, :]
        y_f, fin_f, xbc_act = _ssd(geom, 0, xbc, dt, _pad_states(state_ssm[:, l, 0]), conv_w, conv_b, dt_bias, a_log,
                                   d_skip)
        y_b, fin_b = _ssd(geom, 1, xbc_act, dt, _pad_states(state_ssm[:, l, 1]), conv_w, conv_b, dt_bias, a_log, d_skip)
        ys = [y_f, y_b]
        fins = [_unpad_states(fin_f[:n_ctx_seq]), _unpad_states(fin_b[:n_ctx_seq])]

        x, h2, routed, gate_w, totals = _out_proj(
            geom, l, attn_ctx, attn_lat, conv, ys[0], ys[1], z, x, mod, ssm_norm_g[l][None, :], w_out[l].astype(bf16),
            norm2_g[l][None, :], _pad_lanes(router_w[l]), _pad_lanes(router_b[l][None, :]))

        src_tok, dest, block_e, n_used, row_start = _route(
            routed[:TOP_K], routed[TOP_K:], totals[0, :N_EXPERTS].astype(jnp.int32), n_blocks)
        xs = jnp.take(h2, src_tok, axis=0, mode="clip")
        eo = _moe(l, xs, block_e, n_used, row_start, w_gate, b_gate, w_up, b_up, w_down, b_down)
        rows = jnp.take(eo, dest, axis=0, mode="clip").reshape(TOP_K, n, d)
        if l == DEPTH - 1:
            y_final = _combine(geom, l, x, rows, gate_w, mod, final_g[None, :])
        else:
            x = _combine(geom, l, x, rows, gate_w, mod, None)

        new_k.append(k[:geom.n_ctx].reshape(n_ctx_seq, ctx_len, ATT_HEADS, 2, ATT_QK_DIM).transpose(0, 2, 3, 1, 4))
        new_v.append(v[:geom.n_ctx].reshape(n_ctx_seq, ctx_len, ATT_HEADS, ATT_V_DIM).transpose(0, 2, 1, 3))
        new_s.append(jnp.stack(fins, axis=1))

    y_prompt = y_final[:geom.n_ctx].reshape(n_ctx_seq, ctx_len, d)
    y_sample = y_final[geom.n_ctx:].reshape(n_lat_seq, lat_len, d)
    return (y_prompt, y_sample, jnp.stack(new_k, axis=1), jnp.stack(new_v, axis=1), jnp.stack(new_s, axis=1))
```

```python
import functools
import math

import numpy as np
import jax
import jax.numpy as jnp
from jax import lax
from jax.experimental import pallas as pl
from jax.experimental.pallas import tpu as pltpu

f32 = jnp.float32
bf16 = jnp.bfloat16
HIGHEST = lax.Precision.HIGHEST

D_MODEL = 1024
DEPTH = 2
GRID_W = 64
ATT_HEADS = 4
ATT_QK_DIM = 32
ATT_V_DIM = 64
ATT_WIDTH = 256
QK_WIDTH = 256
ROPE_AXIS_DIM = 16
ROPE_BASE = 10000.0
CONV_CH = 256
CONV_K = 31
SSM_WIDTH = 512
SSM_HEAD_DIM = 64
SSM_HEADS = 8
SSM_GROUPS = 2
SSM_STATE = 64
SSM_CONV_K = 5
SSM_XBC = 768
N_MAIN = 2560
N_EXPERTS = 32
TOP_K = 4
SWIGLU_LIMIT = 7.0
SWIGLU_ALPHA = 1.702
EPS = 1e-6

LANES = 128
SUBLANES = 8
TOK_TILE = 256
SSD_CHUNK = 128
ATT_Q_TILE = 512
CONV_HALO = 16
SSM_HALO = 8
MOE_TILE = 512
MOE_FF_CHUNK = 256
TOKEN_KEY_RANGE = 2 ** 15
COND_ROWS = 16
VMEM_LIMIT = 56 * 1024 * 1024


def _cparams(*sem):
    return pltpu.CompilerParams(dimension_semantics=sem, vmem_limit_bytes=VMEM_LIMIT)


def _silu(x):
    return x * jax.nn.sigmoid(x)


def _ada_kernel(cond_ref, w_ref, b_ref, o_ref):
    s = _silu(cond_ref[...]).astype(bf16)
    o_ref[0] = jnp.dot(s, w_ref[0].astype(bf16), preferred_element_type=f32) + b_ref[0]


def _ada(cond, w_ada, b_ada):
    depth, d, n6 = w_ada.shape
    tn = 1536
    return pl.pallas_call(
        _ada_kernel,
        grid=(depth, n6 // tn),
        in_specs=[
            pl.BlockSpec((COND_ROWS, d), lambda l, j: (0, 0)),
            pl.BlockSpec((1, d, tn), lambda l, j: (l, 0, j)),
            pl.BlockSpec((1, 1, tn), lambda l, j: (l, 0, j)),
        ],
        out_specs=pl.BlockSpec((1, COND_ROWS, tn), lambda l, j: (l, 0, j)),
        out_shape=jax.ShapeDtypeStruct((depth, COND_ROWS, n6), f32),
        compiler_params=_cparams("arbitrary", "arbitrary"),
        name="ada",
    )(cond, w_ada, b_ada.reshape(depth, 1, n6))


class _Geom:
    def __init__(self, n_ctx_seq, ctx_len, n_lat_seq, lat_len):
        self.n_ctx_seq, self.ctx_len, self.n_lat_seq, self.lat_len = n_ctx_seq, ctx_len, n_lat_seq, lat_len
        self.n_ctx = n_ctx_seq * ctx_len
        self.n_lat = n_lat_seq * lat_len
        self.n_tok = self.n_ctx + self.n_lat
        assert ctx_len % TOK_TILE == 0 and lat_len % TOK_TILE == 0
        self.ctx_tiles = self.n_ctx // TOK_TILE
        self.tiles_per_ctx = ctx_len // TOK_TILE
        self.tiles_per_lat = lat_len // TOK_TILE
        self.n_tiles = self.n_tok // TOK_TILE

    def cond_row(self, i):
        return jnp.where(i < self.ctx_tiles, 0, 1 + (i - self.ctx_tiles) // self.tiles_per_lat)

    def seq_pos(self, i):
        j_ctx = i % self.tiles_per_ctx
        j_lat = (i - self.ctx_tiles) % self.tiles_per_lat
        is_ctx = i < self.ctx_tiles
        first = jnp.where(is_ctx, j_ctx == 0, j_lat == 0)
        last = jnp.where(is_ctx, j_ctx == self.tiles_per_ctx - 1, j_lat == self.tiles_per_lat - 1)
        return first, last

    def seq_index(self, i):
        return jnp.where(i < self.ctx_tiles, i // self.tiles_per_ctx,
                         self.n_ctx_seq + (i - self.ctx_tiles) // self.tiles_per_lat)


def _rope_tables(geom):
    t = np.arange(geom.lat_len)
    row = (t // GRID_W).astype(np.float64)
    col = (t % GRID_W).astype(np.float64)
    inv_freq = 1.0 / (ROPE_BASE ** (np.arange(0, ROPE_AXIS_DIM, 2, dtype=np.float64) / ROPE_AXIS_DIM))
    ang_r = row[:, None] * inv_freq[None, :]
    ang_c = col[:, None] * inv_freq[None, :]
    ang = np.concatenate([ang_r, ang_r, ang_c, ang_c], axis=-1)
    cos, sin = np.cos(ang), np.sin(ang)
    half = ROPE_AXIS_DIM // 2
    lane = np.arange(ATT_QK_DIM)
    first_half = (lane % ROPE_AXIS_DIM) < half
    sin_a = np.where(first_half[None, :], -sin, 0.0)
    sin_b = np.where(first_half[None, :], 0.0, sin)
    rep = LANES // ATT_QK_DIM

    def full(tab, ident):
        tab = np.tile(tab, (1, rep))
        return jnp.asarray(np.concatenate([np.full((TOK_TILE, LANES), ident), tab], axis=0), f32)

    return full(cos, 1.0), full(sin_a, 0.0), full(sin_b, 0.0)


def _in_proj_kernel(x_ref, mod_ref, g_ref, w_ref, wdt_ref, cos_ref, sa_ref, sb_ref,
                    q_ref, k_ref, v_ref, glu_ref, z_ref, xbc_ref, dt_ref):
    x = x_ref[...]
    y = x * lax.rsqrt(jnp.mean(x * x, axis=-1, keepdims=True) + EPS) * g_ref[...]
    h = (y * (1.0 + mod_ref[1:2, :]) + mod_ref[0:1, :]).astype(bf16)
    p = jnp.dot(h, w_ref[...], preferred_element_type=f32)
    dt_ref[...] = jnp.dot(h, wdt_ref[...], preferred_element_type=f32)
    cos, sa, sb = cos_ref[...], sa_ref[...], sb_ref[...]
    half = ROPE_AXIS_DIM // 2

    def rope(u):
        return u * cos + pltpu.roll(u, LANES - half, 1) * sa + pltpu.roll(u, half, 1) * sb

    scale = ATT_QK_DIM ** -0.5 * math.log2(math.e)
    for j in range(QK_WIDTH // LANES):
        lo = j * LANES
        q_ref[:, lo:lo + LANES] = (rope(p[:, lo:lo + LANES]) * scale).astype(bf16)
        k_ref[:, lo:lo + LANES] = rope(p[:, QK_WIDTH + lo:QK_WIDTH + lo + LANES])
    o = 2 * QK_WIDTH
    v_ref[...] = p[:, o:o + ATT_WIDTH]
    o += ATT_WIDTH
    glu_ref[...] = p[:, o:o + 2 * CONV_CH]
    o += 2 * CONV_CH
    z_ref[...] = p[:, o:o + SSM_WIDTH]
    o += SSM_WIDTH
    xbc_ref[...] = p[:, o:o + SSM_XBC]


def _in_proj(geom, layer, x, mod, norm_g, w_main, w_dt, tables):
    n = geom.n_tok
    tile = lambda w: pl.BlockSpec((TOK_TILE, w), lambda i: (i, 0))
    const = lambda a: pl.BlockSpec(a.shape, lambda i: (0,) * a.ndim)
    tab_spec = pl.BlockSpec(
        (TOK_TILE, LANES),
        lambda i: (jnp.where(i < geom.ctx_tiles, 0, 1 + (i - geom.ctx_tiles) % geom.tiles_per_lat), 0))
    widths = (QK_WIDTH, QK_WIDTH, ATT_WIDTH, 2 * CONV_CH, SSM_WIDTH, SSM_XBC, LANES)
    dtypes = (bf16, f32, f32, f32, f32, f32, f32)
    return pl.pallas_call(
        _in_proj_kernel,
        grid=(geom.n_tiles,),
        in_specs=[
            tile(D_MODEL),
            pl.BlockSpec((None, None, 6, D_MODEL), lambda i: (layer, geom.cond_row(i), 0, 0)),
            const(norm_g), const(w_main), const(w_dt), tab_spec, tab_spec, tab_spec,
        ],
        out_specs=[tile(w) for w in widths],
        out_shape=[jax.ShapeDtypeStruct((n, w), dt) for w, dt in zip(widths, dtypes)],
        compiler_params=_cparams("arbitrary"),
        name="in_proj",
    )(x, mod, norm_g, w_main, w_dt, *tables)


def _attn_kernel(lam_init, al_ref, g_ref, q_ref, k_ref, v_ref, o_ref):
    al = al_ref[...]
    lam = (jnp.exp(jnp.sum(al[0:1] * al[1:2], axis=-1, keepdims=True))
           - jnp.exp(jnp.sum(al[2:3] * al[3:4], axis=-1, keepdims=True)) + lam_init)
    lane = lax.broadcasted_iota(jnp.int32, (1, LANES), 1)
    heads_per_slab = LANES // ATT_V_DIM
    for j in range(ATT_WIDTH // LANES):
        sl = slice(j * LANES, (j + 1) * LANES)
        qs = q_ref[:, sl]
        ks = k_ref[:, sl].astype(bf16)
        vs = v_ref[:, sl].astype(bf16)
        a_heads = []
        for hl in range(heads_per_slab):
            own = (lane >= hl * ATT_V_DIM) & (lane < (hl + 1) * ATT_V_DIM)
            vh = jnp.where(own, vs, jnp.ones_like(vs))
            sum_lane = (1 - hl) * ATT_V_DIM
            comps = []
            for c in range(2):
                lo = hl * ATT_V_DIM + c * ATT_QK_DIM
                qm = jnp.where((lane >= lo) & (lane < lo + ATT_QK_DIM), qs, jnp.zeros_like(qs))
                s = lax.dot_general(qm, ks, (((1,), (1,)), ((), ())), preferred_element_type=f32)
                e = jnp.exp2(s - jnp.max(s, axis=-1, keepdims=True))
                pv = jnp.dot(e.astype(bf16), vh, preferred_element_type=f32)
                comps.append(pv / pv[:, sum_lane:sum_lane + 1])
            a_heads.append(comps[0] - lam * comps[1])
        a = jnp.where(lane < ATT_V_DIM, a_heads[0], a_heads[1])
        a2 = a * a
        ms0 = jnp.sum(jnp.where(lane < ATT_V_DIM, a2, 0.0), axis=-1, keepdims=True)
        ms1 = jnp.sum(jnp.where(lane < ATT_V_DIM, 0.0, a2), axis=-1, keepdims=True)
        ms = jnp.where(lane < ATT_V_DIM, ms0, ms1) * (1.0 / ATT_V_DIM)
        o_ref[:, sl] = a * lax.rsqrt(ms + EPS) * g_ref[...] * (1.0 - lam_init)


def _attention(lam_init, attn_lambda, subln_g, q, k, v, n_seq, q_len, k_len, q_row0, kv_is_3d):
    tq = min(ATT_Q_TILE, q_len)
    qb = q_len // tq
    q0 = q_row0 // tq
    if kv_is_3d:
        kv_spec = pl.BlockSpec((None, k_len, ATT_WIDTH), lambda b, i: (b, 0, 0))
    else:
        kv_spec = pl.BlockSpec((k_len, ATT_WIDTH), lambda b, i: (q_row0 // k_len + b, 0))
    return pl.pallas_call(
        functools.partial(_attn_kernel, lam_init),
        grid=(n_seq, qb),
        in_specs=[
            pl.BlockSpec(attn_lambda.shape, lambda b, i: (0, 0)),
            pl.BlockSpec(subln_g.shape, lambda b, i: (0, 0)),
            pl.BlockSpec((tq, ATT_WIDTH), lambda b, i: (q0 + b * qb + i, 0)),
            kv_spec, kv_spec,
        ],
        out_specs=pl.BlockSpec((tq, ATT_WIDTH), lambda b, i: (b * qb + i, 0)),
        out_shape=jax.ShapeDtypeStruct((n_seq * q_len, ATT_WIDTH), f32),
        compiler_params=_cparams("arbitrary", "arbitrary"),
        name="attention",
    )(attn_lambda, subln_g, q, k, v)


def _halo_specs(geom, width, halo):
    per = TOK_TILE // halo
    n_halo = geom.n_tok // halo
    prev = pl.BlockSpec((halo, width), lambda i: (jnp.maximum(i * per - 1, 0), 0))
    nxt = pl.BlockSpec((halo, width), lambda i: (jnp.minimum((i + 1) * per, n_halo - 1), 0))
    return prev, nxt


def _cconv_kernel(geom, cur_ref, prev_ref, next_ref, w_ref, b_ref, lg_ref, lb_ref, o_ref, pad_ref, shift_ref):
    first, last = geom.seq_pos(pl.program_id(0))

    def glu(ref):
        return ref[:, :CONV_CH] * jax.nn.sigmoid(ref[:, CONV_CH:])

    pad_ref[0:CONV_HALO, :] = jnp.where(first, 0.0, glu(prev_ref))
    pad_ref[CONV_HALO:CONV_HALO + TOK_TILE, :] = glu(cur_ref)
    pad_ref[CONV_HALO + TOK_TILE:, :] = jnp.where(last, 0.0, glu(next_ref))
    n_shifted = TOK_TILE + 2 * CONV_HALO - SUBLANES
    for j in range(1, SUBLANES):
        shift_ref[j - 1, 0:n_shifted, :] = pad_ref[j:j + n_shifted, :]
    rows = 64
    base = CONV_HALO - CONV_K // 2
    for r0 in range(0, TOK_TILE, rows):
        acc = jnp.broadcast_to(b_ref[...], (rows, CONV_CH))
        for k in range(CONV_K):
            q, j = divmod(r0 + base + k, SUBLANES)
            lo = q * SUBLANES
            window = pad_ref[lo:lo + rows, :] if j == 0 else shift_ref[j - 1, lo:lo + rows, :]
            acc = acc + w_ref[k:k + 1, :] * window
        xc = acc - jnp.mean(acc, axis=-1, keepdims=True)
        var = jnp.mean(xc * xc, axis=-1, keepdims=True)
        o_ref[r0:r0 + rows, :] = _silu(xc * lax.rsqrt(var + EPS) * lg_ref[...] + lb_ref[...])


def _cconv(geom, glu, w, b, ln_g, ln_b):
    prev, nxt = _halo_specs(geom, 2 * CONV_CH, CONV_HALO)
    const = lambda a: pl.BlockSpec(a.shape, lambda i: (0,) * a.ndim)
    return pl.pallas_call(
        functools.partial(_cconv_kernel, geom),
        grid=(geom.n_tiles,),
        in_specs=[pl.BlockSpec((TOK_TILE, 2 * CONV_CH), lambda i: (i, 0)), prev, nxt,
                  const(w), const(b), const(ln_g), const(ln_b)],
        out_specs=pl.BlockSpec((TOK_TILE, CONV_CH), lambda i: (i, 0)),
        out_shape=jax.ShapeDtypeStruct((geom.n_tok, CONV_CH), f32),
        scratch_shapes=[pltpu.VMEM((TOK_TILE + 2 * CONV_HALO, CONV_CH), f32),
                        pltpu.VMEM((SUBLANES - 1, TOK_TILE + 2 * CONV_HALO, CONV_CH), f32)],
        compiler_params=_cparams("arbitrary"),
        name="conformer_conv",
    )(glu, glu, glu, w, b, ln_g, ln_b)


def _softplus(x):
    return jnp.maximum(x, 0.0) + jnp.log1p(jnp.exp(-jnp.abs(x)))


def _ssd_kernel(geom, direction, *refs):
    fwd = direction == 0
    if fwd:
        (cur_ref, prev_ref, next_ref, dt_ref, init_ref, cw_ref, cb_ref, dtb_ref, alog_ref, dskip_ref,
         y_ref, fin_ref, xc_ref, pad_ref, st_ref) = refs
    else:
        xc_ref, dt_ref, init_ref, dtb_ref, alog_ref, dskip_ref, y_ref, fin_ref, st_ref = refs
    i = pl.program_id(0)
    t = i if fwd else geom.n_tiles - 1 - i
    first, last = geom.seq_pos(t)
    enter, leave = (first, last) if fwd else (last, first)
    is_lat = t >= geom.ctx_tiles

    @pl.when(enter)
    def _():
        st_ref[...] = jnp.where(is_lat, init_ref[...], 0.0)

    if fwd:
        pad_ref[0:SSM_HALO, :] = jnp.where(first, 0.0, prev_ref[...])
        pad_ref[SSM_HALO:SSM_HALO + TOK_TILE, :] = cur_ref[...]
        pad_ref[SSM_HALO + TOK_TILE:, :] = jnp.where(last, 0.0, next_ref[...])
        base = SSM_HALO - SSM_CONV_K // 2
        acc = jnp.broadcast_to(cb_ref[...], (TOK_TILE, SSM_XBC))
        for k in range(SSM_CONV_K):
            acc = acc + cw_ref[k:k + 1, :] * pad_ref[base + k:base + k + TOK_TILE, :]
        xc_ref[...] = _silu(acc)

    L = SSD_CHUNK
    a_row = -jnp.exp(alog_ref[...])
    ri = lax.broadcasted_iota(jnp.int32, (L, L), 0)
    ci = lax.broadcasted_iota(jnp.int32, (L, L), 1)
    keep = (ci <= ri) if fwd else (ci >= ri)
    tri_col = keep.astype(f32)
    tri_row = ((ri <= ci) if fwd else (ri >= ci)).astype(f32)
    lane = lax.broadcasted_iota(jnp.int32, (1, LANES), 1)
    er = lax.broadcasted_iota(jnp.int32, (LANES, SSM_WIDTH), 0)
    ec = lax.broadcasted_iota(jnp.int32, (LANES, SSM_WIDTH), 1)
    expand = (er == direction * SSM_HEADS + ec // SSM_HEAD_DIM).astype(f32)
    heads_per_group = SSM_HEADS // SSM_GROUPS
    heads_per_slab = LANES // SSM_HEAD_DIM
    tot_at = L - 1 if fwd else 0

    chunks = range(TOK_TILE // L)
    for cidx in (chunks if fwd else reversed(chunks)):
        r0 = cidx * L
        xs = xc_ref[r0:r0 + L, 0:SSM_WIDTH]
        bm = xc_ref[r0:r0 + L, SSM_WIDTH:SSM_WIDTH + LANES]
        cm = xc_ref[r0:r0 + L, SSM_WIDTH + LANES:SSM_XBC]
        dt = _softplus(dt_ref[r0:r0 + L, :] + dtb_ref[...])
        da = dt * a_row
        acum_col = jnp.dot(tri_col, da, precision=HIGHEST, preferred_element_type=f32)
        acum_row = jnp.dot(da.T, tri_row, precision=HIGHEST, preferred_element_type=f32)
        tot_row = acum_col[tot_at:tot_at + 1, :]
        tot_col = acum_row[:, tot_at:tot_at + 1]
        decay_out = jnp.exp(tot_col - acum_row)
        decay_in = jnp.exp(acum_col)
        chunk_decay = jnp.exp(tot_row)
        xdt = (xs * jnp.dot(dt, expand, precision=HIGHEST, preferred_element_type=f32)).astype(bf16)
        bt = bm.T
        bb = bm.astype(bf16)
        ys = []
        for g in range(SSM_GROUPS):
            in_group = (lane >= g * SSM_STATE) & (lane < (g + 1) * SSM_STATE)
            cg = jnp.where(in_group, cm, 0.0)
            gmat = lax.dot_general(cg.astype(bf16), bb, (((1,), (1,)), ((), ())), preferred_element_type=f32)
            for hh in range(heads_per_group):
                h = g * heads_per_group + hh
                hl = direction * SSM_HEADS + h
                slab = xdt[:, (h // heads_per_slab) * LANES:(h // heads_per_slab + 1) * LANES]
                seg = acum_col[:, hl:hl + 1] - acum_row[hl:hl + 1, :]
                m = (gmat * jnp.exp(jnp.where(keep, seg, -jnp.inf))).astype(bf16)
                y_h = jnp.dot(m, slab, preferred_element_type=f32)
                c_in = (cg * decay_in[:, hl:hl + 1]).astype(bf16)
                state = st_ref[h]
                y_h = y_h + jnp.dot(c_in, state.astype(bf16), preferred_element_type=f32)
                b_out = (bt * decay_out[hl:hl + 1, :]).astype(bf16)
                st_ref[h] = chunk_decay[:, hl:hl + 1] * state + jnp.dot(b_out, slab, preferred_element_type=f32)
                ys.append(y_h)
        for j in range(SSM_WIDTH // LANES):
            y_slab = jnp.where(lane < SSM_HEAD_DIM, ys[heads_per_slab * j], ys[heads_per_slab * j + 1])
            if fwd:
                y_slab = y_slab + dskip_ref[:, j * LANES:(j + 1) * LANES] * xs[:, j * LANES:(j + 1) * LANES]
            y_ref[r0:r0 + L, j * LANES:(j + 1) * LANES] = y_slab

    @pl.when(leave)
    def _():
        fin_ref[...] = st_ref[...]


def _ssd(geom, direction, xbc, dt, init, conv_w, conv_b, dt_bias, a_log, d_skip):
    nt = geom.n_tiles
    fwd = direction == 0
    tidx = (lambda i: i) if fwd else (lambda i: nt - 1 - i)
    per = TOK_TILE // SSM_HALO
    n_halo = geom.n_tok // SSM_HALO
    n_seq = geom.n_ctx_seq + geom.n_lat_seq
    const = lambda a: pl.BlockSpec(a.shape, lambda i: (0,) * a.ndim)
    st_block = (None, SSM_HEADS, LANES, LANES)
    tile = lambda w: pl.BlockSpec((TOK_TILE, w), lambda i: (tidx(i), 0))
    init_spec = pl.BlockSpec(st_block, lambda i: (jnp.maximum(geom.seq_index(tidx(i)) - geom.n_ctx_seq, 0), 0, 0, 0))
    out_specs = [tile(SSM_WIDTH), pl.BlockSpec(st_block, lambda i: (geom.seq_index(tidx(i)), 0, 0, 0))]
    out_shape = [jax.ShapeDtypeStruct((geom.n_tok, SSM_WIDTH), f32),
                 jax.ShapeDtypeStruct((n_seq, SSM_HEADS, LANES, LANES), f32)]
    state_scratch = pltpu.VMEM((SSM_HEADS, LANES, LANES), f32)
    if fwd:
        in_specs = [
            tile(SSM_XBC),
            pl.BlockSpec((SSM_HALO, SSM_XBC), lambda i: (jnp.maximum(i * per - 1, 0), 0)),
            pl.BlockSpec((SSM_HALO, SSM_XBC), lambda i: (jnp.minimum((i + 1) * per, n_halo - 1), 0)),
            tile(LANES), init_spec, const(conv_w), const(conv_b), const(dt_bias), const(a_log), const(d_skip),
        ]
        args = (xbc, xbc, xbc, dt, init, conv_w, conv_b, dt_bias, a_log, d_skip)
        out_specs.append(tile(SSM_XBC))
        out_shape.append(jax.ShapeDtypeStruct((geom.n_tok, SSM_XBC), f32))
        scratch = [pltpu.VMEM((TOK_TILE + 2 * SSM_HALO, SSM_XBC), f32), state_scratch]
    else:
        in_specs = [tile(SSM_XBC), tile(LANES), init_spec, const(dt_bias), const(a_log), const(d_skip)]
        args = (xbc, dt, init, dt_bias, a_log, d_skip)
        scratch = [state_scratch]
    return pl.pallas_call(
        functools.partial(_ssd_kernel, geom, direction),
        grid=(nt,),
        in_specs=in_specs,
        out_specs=out_specs,
        out_shape=out_shape,
        scratch_shapes=scratch,
        compiler_params=_cparams("arbitrary"),
        name="ssd_scan",
    )(*args)


def _pad_states(s):
    return jnp.tile(jnp.swapaxes(s, -1, -2), (1, 1, LANES // SSM_STATE, LANES // SSM_HEAD_DIM))


def _unpad_states(fin):
    hpg = SSM_HEADS // SSM_GROUPS
    outs = []
    for h in range(SSM_HEADS):
        g, side = h // hpg, h % (LANES // SSM_HEAD_DIM)
        q = fin[:, h, g * SSM_STATE:(g + 1) * SSM_STATE, side * SSM_HEAD_DIM:(side + 1) * SSM_HEAD_DIM]
        outs.append(jnp.swapaxes(q, -1, -2))
    return jnp.stack(outs, axis=1)


def _out_proj_kernel(geom, actx_ref, alat_ref, conv_ref, yf_ref, yb_ref, z_ref, x_ref, mod_ref, sg_ref, w_ref,
                     g2_ref, rwh_ref, rwl_ref, rb_ref, xo_ref, h2_ref, ti_ref, gw_ref, cnt_ref):
    attn = jnp.where(pl.program_id(0) < geom.ctx_tiles, actx_ref[...], alat_ref[...])
    gated = (yf_ref[...] + yb_ref[...]) * _silu(z_ref[...])
    ssm = gated * lax.rsqrt(jnp.mean(gated * gated, axis=-1, keepdims=True) + EPS) * sg_ref[...]
    o1 = ATT_WIDTH
    o2 = ATT_WIDTH + CONV_CH
    mix = (jnp.dot(attn.astype(bf16), w_ref[0:o1, :], preferred_element_type=f32)
           + jnp.dot(conv_ref[...].astype(bf16), w_ref[o1:o2, :], preferred_element_type=f32)
           + jnp.dot(ssm.astype(bf16), w_ref[o2:, :], preferred_element_type=f32))
    x = x_ref[...] + mod_ref[2:3, :] * mix
    xo_ref[...] = x
    y = x * lax.rsqrt(jnp.mean(x * x, axis=-1, keepdims=True) + EPS) * g2_ref[...]
    h2 = y * (1.0 + mod_ref[4:5, :]) + mod_ref[3:4, :]
    h2_ref[...] = h2
    h_hi = h2.astype(bf16)
    h_lo = (h2 - h_hi.astype(f32)).astype(bf16)
    logits = (jnp.dot(h_hi, rwh_ref[...], preferred_element_type=f32)
              + jnp.dot(h_lo, rwh_ref[...], preferred_element_type=f32)
              + jnp.dot(h_hi, rwl_ref[...], preferred_element_type=f32)) + rb_ref[...]
    lane = lax.broadcasted_iota(jnp.int32, (1, LANES), 1)
    lane_f = lane.astype(f32)
    lg = jnp.where(lane < N_EXPERTS, logits, -jnp.inf)
    idx_out = jnp.zeros(logits.shape, jnp.int32)
    gate_out = jnp.zeros(logits.shape, f32)
    top0 = None
    denom = None
    picks = []
    for k in range(TOP_K):
        m = jnp.max(lg, axis=-1, keepdims=True)
        idx = jnp.min(jnp.where(lg == m, lane_f, float(LANES)), axis=-1, keepdims=True).astype(jnp.int32)
        picked = lane == idx
        picks.append(picked)
        lg = jnp.where(picked, -jnp.inf, lg)
        if k == 0:
            top0 = m
        e = jnp.exp(m - top0)
        denom = e if k == 0 else denom + e
        idx_out = jnp.where(lane == k, idx, idx_out)
        gate_out = jnp.where(lane == k, e, gate_out)
    gw_ref[...] = gate_out / denom

    @pl.when(pl.program_id(0) == 0)
    def _():
        cnt_ref[...] = jnp.zeros_like(cnt_ref)

    chosen = picks[0] | picks[1] | picks[2] | picks[3]
    sel = jnp.where(chosen, 1.0, 0.0)
    ri = lax.broadcasted_iota(jnp.int32, (TOK_TILE, TOK_TILE), 0)
    ci = lax.broadcasted_iota(jnp.int32, (TOK_TILE, TOK_TILE), 1)
    before = jnp.where(ci < ri, 1.0, 0.0).astype(bf16)
    prior = jnp.dot(before, sel.astype(bf16), preferred_element_type=f32) + cnt_ref[...]
    for k in range(TOP_K):
        rank = jnp.sum(jnp.where(picks[k], prior, 0.0), axis=-1, keepdims=True).astype(jnp.int32)
        idx_out = jnp.where(lane == TOP_K + k, rank, idx_out)
    ti_ref[...] = idx_out.astype(f32).T[0:2 * TOP_K, :].astype(jnp.int32)
    cnt_ref[...] = cnt_ref[...] + jnp.sum(sel, axis=0, keepdims=True)


def _out_proj(geom, layer, attn_ctx, attn_lat, conv, y_f, y_b, z, x, mod, ssm_norm_g, w_out, norm2_g, router_w,
              router_b):
    n = geom.n_tok
    tile = lambda w: pl.BlockSpec((TOK_TILE, w), lambda i: (i, 0))
    const = lambda a: pl.BlockSpec(a.shape, lambda i: (0,) * a.ndim)
    nct = geom.ctx_tiles
    rw_hi = router_w.astype(bf16)
    rw_lo = (router_w - rw_hi.astype(f32)).astype(bf16)
    return pl.pallas_call(
        functools.partial(_out_proj_kernel, geom),
        grid=(geom.n_tiles,),
        in_specs=[
            pl.BlockSpec((TOK_TILE, ATT_WIDTH), lambda i: (jnp.minimum(i, nct - 1), 0)),
            pl.BlockSpec((TOK_TILE, ATT_WIDTH), lambda i: (jnp.maximum(i - nct, 0), 0)),
            tile(CONV_CH), tile(SSM_WIDTH), tile(SSM_WIDTH), tile(SSM_WIDTH), tile(D_MODEL),
            pl.BlockSpec((None, None, 6, D_MODEL), lambda i: (layer, geom.cond_row(i), 0, 0)),
            const(ssm_norm_g), const(w_out), const(norm2_g), const(rw_hi), const(rw_lo), const(router_b),
        ],
        out_specs=[tile(D_MODEL), tile(D_MODEL), pl.BlockSpec((2 * TOP_K, TOK_TILE), lambda i: (0, i)), tile(LANES),
                   pl.BlockSpec((1, LANES), lambda i: (0, 0))],
        out_shape=[jax.ShapeDtypeStruct((n, D_MODEL), f32), jax.ShapeDtypeStruct((n, D_MODEL), f32),
                   jax.ShapeDtypeStruct((2 * TOP_K, n), jnp.int32), jax.ShapeDtypeStruct((n, LANES), f32),
                   jax.ShapeDtypeStruct((1, LANES), f32)],
        compiler_params=_cparams("arbitrary"),
        name="out_proj",
    )(attn_ctx, attn_lat, conv, y_f, y_b, z, x, mod, ssm_norm_g, w_out, norm2_g, rw_hi, rw_lo, router_b)


def _moe_kernel(layer, be_ref, nused_ref, rs_ref, meta_ref, x_ref, wg_hbm, bg_ref, wu_hbm, bu_ref, wd_hbm, bd_ref,
                o_ref, wg_s, wu_s, wd_s, wbuf, sems):
    i = pl.program_id(0)
    first = meta_ref[4 * i] == 1
    slot = meta_ref[4 * i + 1]
    next_e = meta_ref[4 * i + 2]
    has_next = meta_ref[4 * i + 3] == 1

    def weight_copies(e, s):
        return [pltpu.make_async_copy(w.at[layer, e], wbuf.at[s, j], sems.at[s, j])
                for j, w in enumerate((wg_hbm, wu_hbm, wd_hbm))]

    @pl.when(i == 0)
    def _():
        for cp in weight_copies(be_ref[0], 0):
            cp.start()

    @pl.when(first)
    def _():
        for cp in weight_copies(be_ref[i], slot):
            cp.wait()

        @pl.when(has_next)
        def _():
            for cp in weight_copies(next_e, 1 - slot):
                cp.start()

        wg_s[...] = wbuf[slot, 0].astype(bf16)
        wu_s[...] = wbuf[slot, 1].astype(bf16)
        wd_s[...] = wbuf[slot, 2].astype(bf16)

    @pl.when(i < nused_ref[0])
    def _():
        x = x_ref[...].astype(bf16)
        acc = None
        for c0 in range(0, wg_s.shape[1], MOE_FF_CHUNK):
            sl = slice(c0, c0 + MOE_FF_CHUNK)
            g = jnp.dot(x, wg_s[:, sl], preferred_element_type=f32) + bg_ref[0, :, sl]
            u = jnp.dot(x, wu_s[:, sl], preferred_element_type=f32) + bu_ref[0, :, sl]
            g = jnp.minimum(g, SWIGLU_LIMIT)
            u = jnp.clip(u, -SWIGLU_LIMIT, SWIGLU_LIMIT)
            act = g * jax.nn.sigmoid(SWIGLU_ALPHA * g) * (u + 1.0)
            part = jnp.dot(act.astype(bf16), wd_s[sl, :], preferred_element_type=f32)
            acc = part if acc is None else acc + part
        o_ref[...] = acc + bd_ref[0]

    @pl.when(i >= nused_ref[0])
    def _():
        o_ref[...] = jnp.zeros_like(o_ref)


def _moe(layer, xs, block_e, n_used, row_start, w_gate, b_gate, w_up, b_up, w_down, b_down):
    d = xs.shape[1]
    n_blocks = block_e.shape[0]
    d_ff = w_gate.shape[-1]
    assert d == d_ff
    w_spec = pl.BlockSpec(memory_space=pl.ANY)
    b_spec = lambda c: pl.BlockSpec((None, 1, 1, c), lambda i, be, nu, rs, meta: (layer, be[i], 0, 0))
    depth, n_e = b_gate.shape[:2]
    i32 = jnp.int32
    blk = jnp.arange(n_blocks, dtype=i32)
    prev_e = jnp.concatenate([jnp.full((1,), -1, i32), block_e[:-1]])
    first = (blk < n_used[0]) & (block_e != prev_e)
    slot = (jnp.cumsum(first.astype(i32)) - 1) % 2
    first_at_or_after = lax.cummin(jnp.where(first, blk, n_blocks), reverse=True)
    next_first = jnp.concatenate([first_at_or_after[1:], jnp.full((1,), n_blocks, i32)])
    has_next = next_first < n_blocks
    next_e = block_e[jnp.minimum(next_first, n_blocks - 1)]
    meta = jnp.stack([first.astype(i32), slot, next_e, has_next.astype(i32)], axis=1).reshape(-1)
    return pl.pallas_call(
        functools.partial(_moe_kernel, layer),
        grid_spec=pltpu.PrefetchScalarGridSpec(
            num_scalar_prefetch=4,
            grid=(n_blocks,),
            in_specs=[pl.BlockSpec((pl.Element(MOE_TILE), pl.Element(d)),
                                   lambda i, be, nu, rs, meta: (rs[i] * SUBLANES, 0)),
                      w_spec, b_spec(d_ff), w_spec, b_spec(d_ff), w_spec, b_spec(d)],
            out_specs=pl.BlockSpec((MOE_TILE, d), lambda i, be, nu, rs, meta: (i, 0)),
            scratch_shapes=[pltpu.VMEM((d, d_ff), bf16), pltpu.VMEM((d, d_ff), bf16), pltpu.VMEM((d_ff, d), bf16),
                            pltpu.VMEM((2, 3, d, d_ff), f32), pltpu.SemaphoreType.DMA((2, 3))],
        ),
        out_shape=jax.ShapeDtypeStruct((n_blocks * MOE_TILE, d), f32),
        compiler_params=_cparams("arbitrary"),
        name="moe_experts",
    )(block_e, n_used, row_start, meta, xs, w_gate, b_gate.reshape(depth, n_e, 1, d_ff), w_up,
      b_up.reshape(depth, n_e, 1, d_ff), w_down, b_down.reshape(depth, n_e, 1, d))


def _expert_table(expert_ids, table):
    onehot = expert_ids[:, None] == jnp.arange(N_EXPERTS, dtype=jnp.int32)[None, :]
    return jnp.sum(jnp.where(onehot, table[None, :], 0), axis=1)


def _route(top_idx, rank, counts, n_blocks):
    n = top_idx.shape[1]
    n_assign = n * TOP_K
    i32 = jnp.int32
    flat_e = top_idx.reshape(-1)
    flat_rank = rank.reshape(-1)
    experts = jnp.arange(N_EXPERTS, dtype=i32)
    fill = SUBLANES - 1
    grp_size = (counts + fill) // SUBLANES * SUBLANES
    grp_start = jnp.cumsum(grp_size) - grp_size
    nb = (counts + MOE_TILE - 1) // MOE_TILE
    blk_end = jnp.cumsum(nb)
    blk_start = blk_end - nb
    n_sorted = n_assign + N_EXPERTS * fill
    j = jnp.arange(fill, dtype=i32)
    filler_pos = jnp.where(j[None, :] < (grp_size - counts)[:, None], (grp_start + counts)[:, None] + j[None, :],
                           n_sorted)
    pos = jnp.concatenate([_expert_table(flat_e, grp_start) + flat_rank, filler_pos.reshape(-1)])
    tok = jnp.concatenate([jnp.arange(n_assign, dtype=i32) % n, jnp.zeros((N_EXPERTS * fill,), i32)])
    assert n < TOKEN_KEY_RANGE and (n_sorted + 1) * TOKEN_KEY_RANGE <= 2 ** 32
    keyed = lax.sort(pos.astype(jnp.uint32) * TOKEN_KEY_RANGE + tok.astype(jnp.uint32))
    sorted_tok = (keyed % TOKEN_KEY_RANGE).astype(i32)
    src_tok = jnp.concatenate([sorted_tok, jnp.zeros((MOE_TILE,), i32)])
    dest = _expert_table(flat_e, blk_start * MOE_TILE) + flat_rank
    blk = jnp.arange(n_blocks, dtype=i32)
    block_e = jnp.minimum(jnp.sum((blk_end[None, :] <= blk[:, None]).astype(i32), axis=1), N_EXPERTS - 1)
    n_used = blk_end[-1:].astype(i32)
    row_start = jnp.where(blk < n_used[0], blk * MOE_TILE - _expert_table(block_e, blk_start * MOE_TILE - grp_start), 0)
    return src_tok, dest, block_e, n_used, row_start // SUBLANES


def _combine_kernel(final, x_ref, rows_ref, gw_ref, mod_ref, *rest):
    ffn = rows_ref[0] * gw_ref[:, 0:1]
    for k in range(1, TOP_K):
        ffn = ffn + rows_ref[k] * gw_ref[:, k:k + 1]
    x = x_ref[...] + mod_ref[5:6, :] * ffn
    if final:
        fg_ref, y_ref = rest
        y_ref[...] = x * lax.rsqrt(jnp.mean(x * x, axis=-1, keepdims=True) + EPS) * fg_ref[...]
    else:
        rest[0][...] = x


def _combine(geom, layer, x, rows, gate_w, mod, final_g):
    n = geom.n_tok
    final = final_g is not None
    tile = pl.BlockSpec((TOK_TILE, D_MODEL), lambda i: (i, 0))
    in_specs = [
        tile,
        pl.BlockSpec((TOP_K, TOK_TILE, D_MODEL), lambda i: (0, i, 0)),
        pl.BlockSpec((TOK_TILE, LANES), lambda i: (i, 0)),
        pl.BlockSpec((None, None, 6, D_MODEL), lambda i: (layer, geom.cond_row(i), 0, 0)),
    ]
    args = [x, rows, gate_w, mod]
    if final:
        in_specs.append(pl.BlockSpec(final_g.shape, lambda i: (0, 0)))
        args.append(final_g)
    return pl.pallas_call(
        functools.partial(_combine_kernel, final),
        grid=(geom.n_tiles,),
        in_specs=in_specs,
        out_specs=tile,
        out_shape=jax.ShapeDtypeStruct((n, D_MODEL), f32),
        compiler_params=_cparams("arbitrary"),
        name="moe_combine",
    )(*args)


def _pad_lanes(a, width=LANES):
    return jnp.pad(a, [(0, 0)] * (a.ndim - 1) + [(0, width - a.shape[-1])])


def kernel(x_prompt, x_sample, c, cache_k, cache_v, state_ssm, c_ctx, norm1_g, norm2_g, w_ada, b_ada, w_in, w_out, attn_lambda, attn_subln_g, conv_dw_w, conv_dw_b, conv_ln_g, conv_ln_b, ssm_conv_w, ssm_conv_b, ssm_dt_bias, ssm_a_log, ssm_d, ssm_norm_g, router_w, router_b, w_gate, b_gate, w_up, b_up, w_down, b_down, final_g):
    n_ctx_seq, ctx_len, d = x_prompt.shape
    n_lat_seq, lat_len, _ = x_sample.shape
    past_len = cache_k.shape[-2]
    geom = _Geom(n_ctx_seq, ctx_len, n_lat_seq, lat_len)
    n = geom.n_tok

    x = jnp.concatenate([x_prompt.reshape(geom.n_ctx, d), x_sample.reshape(geom.n_lat, d)], axis=0)
    cond = jnp.concatenate([c_ctx[None, :], c, jnp.zeros((COND_ROWS - 1 - n_lat_seq, d), f32)], axis=0)
    mod = _ada(cond, w_ada, b_ada).reshape(DEPTH, COND_ROWS, 6, d)
    tables = _rope_tables(geom)

    n_assign = n * TOP_K
    n_blocks = -(-n_assign // MOE_TILE) + N_EXPERTS

    new_k, new_v, new_s = [], [], []
    y_final = None
    for l in range(DEPTH):
        w_main = w_in[l, :, :N_MAIN].astype(bf16)
        w_dt = _pad_lanes(w_in[l, :, N_MAIN:]).astype(bf16)
        q, k, v, glu, z, xbc, dt = _in_proj(geom, l, x, mod, norm1_g[l][None, :], w_main, w_dt, tables)

        lam_init = 0.8 - 0.6 * math.exp(-0.3 * l)
        subln = jnp.tile(attn_subln_g[l], LANES // ATT_V_DIM)[None, :]
        k_cache = cache_k[:, l].transpose(0, 3, 1, 2, 4).reshape(n_lat_seq, past_len, QK_WIDTH)
        v_cache = cache_v[:, l].transpose(0, 2, 1, 3).reshape(n_lat_seq, past_len, ATT_WIDTH)
        k_all = jnp.concatenate([k[geom.n_ctx:].reshape(n_lat_seq, lat_len, QK_WIDTH), k_cache], axis=1).astype(bf16)
        v_all = jnp.concatenate([v[geom.n_ctx:].reshape(n_lat_seq, lat_len, ATT_WIDTH), v_cache], axis=1).astype(bf16)
        attn_ctx = _attention(lam_init, attn_lambda[l], subln, q, k, v, n_ctx_seq, ctx_len, ctx_len, 0, False)
        attn_lat = _attention(lam_init, attn_lambda[l], subln, q, k_all, v_all, n_lat_seq, lat_len,
                              lat_len + past_len, geom.n_ctx, True)

        conv = _cconv(geom, glu, conv_dw_w[l], conv_dw_b[l][None, :], conv_ln_g[l][None, :], conv_ln_b[l][None, :])

        dt_bias = _pad_lanes(ssm_dt_bias[l].reshape(1, 2 * SSM_HEADS))
        a_log = _pad_lanes(ssm_a_log[l].reshape(1, 2 * SSM_HEADS))
        d_skip = jnp.repeat(ssm_d[l], SSM_HEAD_DIM)[None, :]
        conv_w, conv_b = ssm_conv_w[l], ssm_conv_b[l][None, :]
        y_f, fin_f, xbc_act = _ssd(geom, 0, xbc, dt, _pad_states(state_ssm[:, l, 0]), conv_w, conv_b, dt_bias, a_log,
                                   d_skip)
        y_b, fin_b = _ssd(geom, 1, xbc_act, dt, _pad_states(state_ssm[:, l, 1]), conv_w, conv_b, dt_bias, a_log, d_skip)
        ys = [y_f, y_b]
        fins = [_unpad_states(fin_f[:n_ctx_seq]), _unpad_states(fin_b[:n_ctx_seq])]

        x, h2, routed, gate_w, totals = _out_proj(
            geom, l, attn_ctx, attn_lat, conv, ys[0], ys[1], z, x, mod, ssm_norm_g[l][None, :], w_out[l].astype(bf16),
            norm2_g[l][None, :], _pad_lanes(router_w[l]), _pad_lanes(router_b[l][None, :]))

        src_tok, dest, block_e, n_used, row_start = _route(
            routed[:TOP_K], routed[TOP_K:], totals[0, :N_EXPERTS].astype(jnp.int32), n_blocks)
        xs = jnp.take(h2, src_tok, axis=0, mode="clip")
        eo = _moe(l, xs, block_e, n_used, row_start, w_gate, b_gate, w_up, b_up, w_down, b_down)
        rows = jnp.take(eo, dest, axis=0, mode="clip").reshape(TOP_K, n, d)
        if l == DEPTH - 1:
            y_final = _combine(geom, l, x, rows, gate_w, mod, final_g[None, :])
        else:
            x = _combine(geom, l, x, rows, gate_w, mod, None)

        new_k.append(k[:geom.n_ctx].reshape(n_ctx_seq, ctx_len, ATT_HEADS, 2, ATT_QK_DIM).transpose(0, 2, 3, 1, 4))
        new_v.append(v[:geom.n_ctx].reshape(n_ctx_seq, ctx_len, ATT_HEADS, ATT_V_DIM).transpose(0, 2, 1, 3))
        new_s.append(jnp.stack(fins, axis=1))

    y_prompt = y_final[:geom.n_ctx].reshape(n_ctx_seq, ctx_len, d)
    y_sample = y_final[geom.n_ctx:].reshape(n_lat_seq, lat_len, d)
    return (y_prompt, y_sample, jnp.stack(new_k, axis=1), jnp.stack(new_v, axis=1), jnp.stack(new_s, axis=1))
```
